```python
import math
import jax, jax.numpy as jnp
from jax import lax
import numpy as np

D_MODEL = 1024
BATCH = 2
SEQ = 8192
DEPTH = 4

N_MIXERS = 3
MIXER_ORDER = ('gla', 'diff', 'sgu')
EPS = 1e-6
NEG_INF = -1e30
D_FF = 4 * D_MODEL

GLA_HEADS = 4
GLA_DK = D_MODEL // 2
GLA_DV = D_MODEL
GLA_HK = GLA_DK // GLA_HEADS
GLA_HV = GLA_DV // GLA_HEADS
GLA_RANK = 16
GLA_GATE_NORM = 16.0
GLA_CHUNK = 64
GLA_IN = 2 * GLA_DK + 2 * GLA_DV

DIFF_HEAD_DIM = 64
DIFF_HEADS = D_MODEL // (2 * DIFF_HEAD_DIM)
DIFF_QBLOCK = 128
DIFF_IN = 4 * DIFF_HEADS * DIFF_HEAD_DIM + 2 * DIFF_HEADS * DIFF_HEAD_DIM

SGU_CHUNK = 128
SGU_WIDTH = D_MODEL
SGU_GROUPS = 8
SGU_GROUP_DIM = SGU_WIDTH // SGU_GROUPS

kernel_name = 'hybrid_gla_diffattn_sgu_trunk'


def rmsnorm(x, g):
    xf = x.astype(jnp.float32)
    y = xf * lax.rsqrt(jnp.mean(xf * xf, axis=-1, keepdims=True) + EPS)
    return (y * g.astype(jnp.float32)).astype(x.dtype)


def head_rms(o):
    return o * lax.rsqrt(jnp.mean(o * o, axis=-1, keepdims=True) + EPS)


def alibi_slopes(n_heads):
    return 2.0 ** (-8.0 * jnp.arange(1, n_heads + 1, dtype=jnp.float32) / n_heads)


def gla_mixer(h, w_in, gate_w1, gate_w2, gate_b, head_norm, w_out):
    B, T, _ = h.shape
    C = GLA_CHUNK
    nC = T // C
    f32 = jnp.float32
    proj = h @ w_in
    q, k, v, g = jnp.split(proj, [GLA_DK, 2 * GLA_DK, 2 * GLA_DK + GLA_DV], axis=-1)
    log_a = jax.nn.log_sigmoid(((h @ gate_w1) @ gate_w2 + gate_b).astype(f32)) / GLA_GATE_NORM

    def chunked(t, d):
        return t.reshape(B, nC, C, GLA_HEADS, d).transpose(0, 3, 1, 2, 4).astype(f32)

    q = chunked(q, GLA_HK) * (GLA_HK ** -0.5)
    k = chunked(k, GLA_HK)
    v = chunked(v, GLA_HV)
    b = jnp.cumsum(chunked(log_a, GLA_HK), axis=3)
    b_last = b[:, :, :, -1:, :]
    q_dec = q * jnp.exp(b)
    k_inv = k * jnp.exp(-b)
    k_to_end = k * jnp.exp(b_last - b)
    causal = jnp.tril(jnp.ones((C, C), dtype=bool))
    att = jnp.where(causal, jnp.einsum('bhncd,bhnsd->bhncs', q_dec, k_inv), 0.0)
    o_intra = jnp.einsum('bhncs,bhnsv->bhncv', att, v)
    kv = jnp.einsum('bhncd,bhncv->bhndv', k_to_end, v)
    decay = jnp.exp(b_last[:, :, :, 0, :])

    def step(S, inp):
        kv_n, dec_n = inp
        return dec_n[..., None] * S + kv_n, S

    S0 = jnp.zeros((B, GLA_HEADS, GLA_HK, GLA_HV), f32)
    _, S_prev = lax.scan(step, S0, (kv.transpose(2, 0, 1, 3, 4), decay.transpose(2, 0, 1, 3)))
    S_prev = S_prev.transpose(1, 2, 0, 3, 4)
    o = o_intra + jnp.einsum('bhncd,bhndv->bhncv', q_dec, S_prev)
    o = head_rms(o) * head_norm.astype(f32)
    o = o.transpose(0, 2, 3, 1, 4).reshape(B, T, GLA_DV)
    o = o * jax.nn.silu(g.astype(f32))
    return o.astype(h.dtype) @ w_out


def diff_mixer(h, w_in, lq1, lk1, lq2, lk2, head_norm, w_out, layer_idx):
    B, T, _ = h.shape
    H, Dh, QB = DIFF_HEADS, DIFF_HEAD_DIM, DIFF_QBLOCK
    nQ = T // QB
    f32 = jnp.float32
    lambda_init = 0.8 - 0.6 * math.exp(-0.3 * layer_idx)
    lam = (jnp.exp(jnp.sum(lq1.astype(f32) * lk1.astype(f32)))
           - jnp.exp(jnp.sum(lq2.astype(f32) * lk2.astype(f32))) + lambda_init)
    proj = h @ w_in
    q, k, v = jnp.split(proj, [2 * H * Dh, 4 * H * Dh], axis=-1)
    q = q.reshape(B, T, H, 2, Dh) * (Dh ** -0.5)
    k = k.reshape(B, T, H, 2, Dh)
    v = v.reshape(B, T, H, 2 * Dh)
    slopes = alibi_slopes(H)[:, None, None, None]
    pos_k = jnp.arange(T)
    q_blocks = jnp.moveaxis(q.reshape(B, nQ, QB, H, 2, Dh), 1, 0)

    def attend_block(args):
        q_blk, blk = args
        pos_q = blk * QB + jnp.arange(QB)
        dist = pos_q[:, None] - pos_k[None, :]
        s = jnp.einsum('bqhrd,bkhrd->bhrqk', q_blk, k).astype(f32)
        s = jnp.where(dist >= 0, s - slopes * dist.astype(f32), NEG_INF)
        p = jax.nn.softmax(s, axis=-1)
        p = p[:, :, 0] - lam * p[:, :, 1]
        return jnp.einsum('bhqk,bkhv->bqhv', p.astype(v.dtype), v)

    o = lax.map(attend_block, (q_blocks, jnp.arange(nQ)))
    o = jnp.moveaxis(o, 0, 1).reshape(B, T, H, 2 * Dh).astype(f32)
    o = head_rms(o) * head_norm.astype(f32) * (1.0 - lambda_init)
    return o.reshape(B, T, H * 2 * Dh).astype(h.dtype) @ w_out


def sgu_mixer(h, w_in, b_in, v_norm, w_s, b_s, w_out):
    B, T, _ = h.shape
    C = SGU_CHUNK
    nC = T // C
    uv = jax.nn.gelu(h @ w_in + b_in)
    u, v = jnp.split(uv, 2, axis=-1)
    v = rmsnorm(v, v_norm)
    vc = v.reshape(B, nC, C, SGU_GROUPS, SGU_GROUP_DIM)
    w = w_s * jnp.tril(jnp.ones((C, C), w_s.dtype))
    s = jnp.einsum('gts,bnsgd->bntgd', w, vc) + b_s.T[:, :, None]
    return (u * s.reshape(B, T, SGU_WIDTH)) @ w_out


def sq_relu_mlp(h, w1, w2):
    a = jax.nn.relu(h @ w1)
    return (a * a) @ w2


def setup_inputs(seed: int = 0) -> dict:
    key = jax.random.key(seed)
    keys = iter(jax.random.split(key, 64))

    def nrm(shape, scale):
        return scale * jax.random.normal(next(keys), shape, jnp.float32)

    def gain(n):
        return 1.0 + 0.05 * jax.random.normal(next(keys), (n,), jnp.float32)

    inp = {'x': jax.random.normal(next(keys), (BATCH, SEQ, D_MODEL), jnp.float32)}
    for i in range(DEPTH):
        p = 'l%d_' % i
        kind = MIXER_ORDER[i % N_MIXERS]
        inp[p + 'norm1'] = gain(D_MODEL)
        if kind == 'gla':
            inp[p + 'w_in'] = nrm((D_MODEL, GLA_IN), D_MODEL ** -0.5)
            inp[p + 'gate_w1'] = nrm((D_MODEL, GLA_RANK), D_MODEL ** -0.5)
            inp[p + 'gate_w2'] = nrm((GLA_RANK, GLA_DK), GLA_RANK ** -0.5)
            inp[p + 'gate_b'] = nrm((GLA_DK,), 0.1)
            inp[p + 'head_norm'] = gain(GLA_HV)
            inp[p + 'w_out'] = nrm((GLA_DV, D_MODEL), GLA_DV ** -0.5)
        elif kind == 'diff':
            inp[p + 'w_in'] = nrm((D_MODEL, DIFF_IN), D_MODEL ** -0.5)
            for nm in ('lambda_q1', 'lambda_k1', 'lambda_q2', 'lambda_k2'):
                inp[p + nm] = nrm((DIFF_HEAD_DIM,), 0.1)
            inp[p + 'head_norm'] = gain(2 * DIFF_HEAD_DIM)
            inp[p + 'w_out'] = nrm((2 * DIFF_HEADS * DIFF_HEAD_DIM, D_MODEL), D_MODEL ** -0.5)
        else:
            inp[p + 'w_in'] = nrm((D_MODEL, 2 * SGU_WIDTH), D_MODEL ** -0.5)
            inp[p + 'b_in'] = nrm((2 * SGU_WIDTH,), 0.02)
            inp[p + 'v_norm'] = gain(SGU_WIDTH)
            inp[p + 'w_s'] = nrm((SGU_GROUPS, SGU_CHUNK, SGU_CHUNK), 0.5 * SGU_CHUNK ** -0.5)
            inp[p + 'b_s'] = 1.0 + nrm((SGU_GROUPS, SGU_CHUNK), 0.05)
            inp[p + 'w_out'] = nrm((SGU_WIDTH, D_MODEL), SGU_WIDTH ** -0.5)
        inp[p + 'norm2'] = gain(D_MODEL)
        inp[p + 'mlp_w1'] = nrm((D_MODEL, D_FF), D_MODEL ** -0.5)
        inp[p + 'mlp_w2'] = nrm((D_FF, D_MODEL), D_FF ** -0.5)
    inp['final_norm'] = gain(D_MODEL)
    return inp


def reference(x,
              l0_norm1, l0_w_in, l0_gate_w1, l0_gate_w2, l0_gate_b, l0_head_norm, l0_w_out,
              l0_norm2, l0_mlp_w1, l0_mlp_w2,
              l1_norm1, l1_w_in, l1_lambda_q1, l1_lambda_k1, l1_lambda_q2, l1_lambda_k2,
              l1_head_norm, l1_w_out, l1_norm2, l1_mlp_w1, l1_mlp_w2,
              l2_norm1, l2_w_in, l2_b_in, l2_v_norm, l2_w_s, l2_b_s, l2_w_out,
              l2_norm2, l2_mlp_w1, l2_mlp_w2,
              l3_norm1, l3_w_in, l3_gate_w1, l3_gate_w2, l3_gate_b, l3_head_norm, l3_w_out,
              l3_norm2, l3_mlp_w1, l3_mlp_w2,
              final_norm):
    layers = (
        (l0_norm1, lambda h: gla_mixer(h, l0_w_in, l0_gate_w1, l0_gate_w2, l0_gate_b, l0_head_norm, l0_w_out),
         l0_norm2, l0_mlp_w1, l0_mlp_w2),
        (l1_norm1, lambda h: diff_mixer(h, l1_w_in, l1_lambda_q1, l1_lambda_k1, l1_lambda_q2, l1_lambda_k2,
                                        l1_head_norm, l1_w_out, 1),
         l1_norm2, l1_mlp_w1, l1_mlp_w2),
        (l2_norm1, lambda h: sgu_mixer(h, l2_w_in, l2_b_in, l2_v_norm, l2_w_s, l2_b_s, l2_w_out),
         l2_norm2, l2_mlp_w1, l2_mlp_w2),
        (l3_norm1, lambda h: gla_mixer(h, l3_w_in, l3_gate_w1, l3_gate_w2, l3_gate_b, l3_head_norm, l3_w_out),
         l3_norm2, l3_mlp_w1, l3_mlp_w2),
    )
    for i in range(DEPTH):
        norm1, mixer, norm2, w1, w2 = layers[i]
        x = x + mixer(rmsnorm(x, norm1))
        x = x + sq_relu_mlp(rmsnorm(x, norm2), w1, w2)
    return rmsnorm(x, final_norm)
```

```python
import functools
import math

import jax
import jax.numpy as jnp
from jax import lax
from jax.experimental import pallas as pl
from jax.experimental.pallas import tpu as pltpu

F32 = jnp.float32
BF16 = jnp.bfloat16

EPS = 1e-6
NEG_INF = -1e30

GLA_HEADS = 4
GLA_RANK = 16
GLA_GATE_NORM = 16.0
GLA_CHUNK = 64

DIFF_HEAD_DIM = 64
DIFF_HEADS = 8

SGU_CHUNK = 128
SGU_GROUPS = 8

LANES = 128
VMEM_LIMIT = 52 * 1024 * 1024


def _params(*sem):
    return pltpu.CompilerParams(dimension_semantics=sem, vmem_limit_bytes=VMEM_LIMIT)


def _dot(a, b):
    return jnp.dot(a, b, preferred_element_type=F32)


def _dot_nt(a, b):
    return lax.dot_general(a, b, (((1,), (1,)), ((), ())), preferred_element_type=F32)


def _dot_tn(a, b):
    return lax.dot_general(a, b, (((0,), (0,)), ((), ())), preferred_element_type=F32)


def _rms(x, g):
    return x * lax.rsqrt(jnp.mean(x * x, axis=-1, keepdims=True) + EPS) * g


def _gla_inproj_body(x_ref, g_ref, w_ref, gw1_ref, gw2_ref, gb_ref, q_ref, k_ref, v_ref, og_ref, la_ref):
    dk = q_ref.shape[1]
    dv = v_ref.shape[1]
    h = _rms(x_ref[...], g_ref[...]).astype(BF16)
    q_ref[...] = _dot(h, w_ref[:, 0:dk])
    k_ref[...] = _dot(h, w_ref[:, dk:2 * dk])
    v_ref[...] = _dot(h, w_ref[:, 2 * dk:2 * dk + dv]).astype(BF16)
    og_ref[...] = _dot(h, w_ref[:, 2 * dk + dv:2 * dk + 2 * dv])
    t = _dot(h, gw1_ref[...]).astype(BF16)
    z = _dot(t, gw2_ref[...]) + gb_ref[...]
    la_ref[...] = (jnp.minimum(z, 0.0) - jnp.log(1.0 + jnp.exp(-jnp.abs(z)))) / GLA_GATE_NORM


def _gla_inproj(x, gain, w_in, gw1, gw2, gb, tm=512):
    m, d = x.shape
    dk = gw2.shape[1]
    dv = (w_in.shape[1] - 2 * dk) // 2
    gw1p = jnp.zeros((d, LANES), BF16).at[:, :GLA_RANK].set(gw1.astype(BF16))
    gw2p = jnp.zeros((LANES, dk), BF16).at[:GLA_RANK, :].set(gw2.astype(BF16))
    row = lambda i: (i, 0)
    fixed = lambda i: (0, 0)
    return pl.pallas_call(
        _gla_inproj_body,
        grid=(m // tm,),
        in_specs=[
            pl.BlockSpec((tm, d), row),
            pl.BlockSpec((1, d), fixed),
            pl.BlockSpec(w_in.shape, fixed),
            pl.BlockSpec(gw1p.shape, fixed),
            pl.BlockSpec(gw2p.shape, fixed),
            pl.BlockSpec((1, dk), fixed),
        ],
        out_specs=[
            pl.BlockSpec((tm, dk), row),
            pl.BlockSpec((tm, dk), row),
            pl.BlockSpec((tm, dv), row),
            pl.BlockSpec((tm, dv), row),
            pl.BlockSpec((tm, dk), row),
        ],
        out_shape=[
            jax.ShapeDtypeStruct((m, dk), F32),
            jax.ShapeDtypeStruct((m, dk), F32),
            jax.ShapeDtypeStruct((m, dv), BF16),
            jax.ShapeDtypeStruct((m, dv), F32),
            jax.ShapeDtypeStruct((m, dk), F32),
        ],
        compiler_params=_params("parallel"),
        name="gla_inproj",
    )(x, gain.reshape(1, d), w_in.astype(BF16), gw1p, gw2p, gb.reshape(1, dk))


def _gla_body(tri_ref, q_ref, k_ref, v_ref, og_ref, la_ref, hn_ref, o_ref, st_ref):
    c = GLA_CHUNK
    tc, hk = q_ref.shape
    n_chunks = tc // c

    @pl.when(pl.program_id(2) == 0)
    def _():
        st_ref[...] = jnp.zeros_like(st_ref)

    la = la_ref[...]
    la_hi = la.astype(BF16)
    la_lo = (la - la_hi.astype(F32)).astype(BF16)
    tri = tri_ref[...]
    b = _dot(tri, la_hi) + _dot(tri, la_lo)

    q = q_ref[...] * (hk ** -0.5)
    k = k_ref[...]
    q_dec = (q * jnp.exp(b)).astype(BF16)
    k_inv = (k * jnp.exp(-b)).astype(BF16)

    row = lax.broadcasted_iota(jnp.int32, (c, c), 0)
    col = lax.broadcasted_iota(jnp.int32, (c, c), 1)
    causal = col <= row
    hn = hn_ref[...]

    for n in range(n_chunks):
        sl = slice(n * c, (n + 1) * c)
        b_n = b[sl]
        b_last = b_n[c - 1:c, :]
        k_end = (k[sl] * jnp.exp(b_last - b_n)).astype(BF16)
        qd = q_dec[sl]
        v_n = v_ref[sl, :]
        att = jnp.where(causal, _dot_nt(qd, k_inv[sl]), 0.0).astype(BF16)
        st = st_ref[...]
        o = _dot(att, v_n) + _dot_nt(qd, st.astype(BF16))
        st_ref[...] = jnp.exp(b_last) * st + _dot_tn(v_n, k_end)
        o = o * lax.rsqrt(jnp.mean(o * o, axis=-1, keepdims=True) + EPS) * hn
        g = og_ref[sl, :]
        o = o * (g / (1.0 + jnp.exp(-g)))
        o_ref[sl, :] = o.astype(o_ref.dtype)


def _gla_core(q, k, v, og, la, head_norm, batch, seq, tc=256):
    m, dk = q.shape
    dv = v.shape[1]
    hk = dk // GLA_HEADS
    hv = dv // GLA_HEADS
    nt = seq // tc
    idx = jnp.arange(tc)
    tri = ((idx[:, None] // GLA_CHUNK == idx[None, :] // GLA_CHUNK)
           & (idx[None, :] <= idx[:, None])).astype(BF16)
    off = dk // hk
    rows = lambda b, h, t: b * nt + t
    return pl.pallas_call(
        _gla_body,
        grid=(batch, GLA_HEADS, nt),
        in_specs=[
            pl.BlockSpec((tc, tc), lambda b, h, t: (0, 0)),
            pl.BlockSpec((tc, hk), lambda b, h, t: (rows(b, h, t), h)),
            pl.BlockSpec((tc, hk), lambda b, h, t: (rows(b, h, t), h)),
            pl.BlockSpec((tc, hv), lambda b, h, t: (rows(b, h, t), h)),
            pl.BlockSpec((tc, hv), lambda b, h, t: (rows(b, h, t), h)),
            pl.BlockSpec((tc, hk), lambda b, h, t: (rows(b, h, t), h)),
            pl.BlockSpec((1, hv), lambda b, h, t: (0, 0)),
        ],
        out_specs=pl.BlockSpec((tc, hv), lambda b, h, t: (rows(b, h, t), h)),
        out_shape=jax.ShapeDtypeStruct((m, dv), BF16),
        scratch_shapes=[pltpu.VMEM((hv, hk), F32)],
        compiler_params=_params("parallel", "parallel", "arbitrary"),
        name="gla_core",
    )(tri, q, k, v, og, la, head_norm.reshape(1, hv))


def _outproj_body(o_ref, w_ref, x_ref, y_ref):
    y_ref[...] = x_ref[...] + _dot(o_ref[...], w_ref[...])


def _outproj(o, w, x, tm=512):
    m, d = x.shape
    kdim = o.shape[1]
    return pl.pallas_call(
        _outproj_body,
        grid=(m // tm,),
        in_specs=[
            pl.BlockSpec((tm, kdim), lambda i: (i, 0)),
            pl.BlockSpec((kdim, d), lambda i: (0, 0)),
            pl.BlockSpec((tm, d), lambda i: (i, 0)),
        ],
        out_specs=pl.BlockSpec((tm, d), lambda i: (i, 0)),
        out_shape=jax.ShapeDtypeStruct((m, d), F32),
        compiler_params=_params("parallel"),
        name="outproj",
    )(o, w.astype(BF16), x)


def _mlp_body(final, x_ref, g_ref, w1_ref, w2_ref, gf_ref, y_ref, h_scr, acc_scr):
    f = pl.program_id(1)

    @pl.when(f == 0)
    def _():
        h_scr[...] = _rms(x_ref[...], g_ref[...]).astype(BF16)
        acc_scr[...] = jnp.zeros_like(acc_scr)

    a = jnp.maximum(_dot(h_scr[...], w1_ref[...]), 0.0)
    acc_scr[...] += _dot((a * a).astype(BF16), w2_ref[...])

    @pl.when(f == pl.num_programs(1) - 1)
    def _():
        y = x_ref[...] + acc_scr[...]
        if final:
            y = _rms(y, gf_ref[...])
        y_ref[...] = y


def _mlp(x, gain, w1, w2, final_gain=None, tm=1024, tf=512):
    m, d = x.shape
    dff = w1.shape[1]
    final = final_gain is not None
    gf = (final_gain if final else gain).reshape(1, d)
    return pl.pallas_call(
        functools.partial(_mlp_body, final),
        grid=(m // tm, dff // tf),
        in_specs=[
            pl.BlockSpec((tm, d), lambda i, f: (i, 0)),
            pl.BlockSpec((1, d), lambda i, f: (0, 0)),
            pl.BlockSpec((d, tf), lambda i, f: (0, f)),
            pl.BlockSpec((tf, d), lambda i, f: (f, 0)),
            pl.BlockSpec((1, d), lambda i, f: (0, 0)),
        ],
        out_specs=pl.BlockSpec((tm, d), lambda i, f: (i, 0)),
        out_shape=jax.ShapeDtypeStruct((m, d), F32),
        scratch_shapes=[pltpu.VMEM((tm, d), BF16), pltpu.VMEM((tm, d), F32)],
        compiler_params=_params("parallel", "arbitrary"),
        name="mlp",
    )(x, gain.reshape(1, d), w1.astype(BF16), w2.astype(BF16), gf)


ATT_BLOCK = 256


def _diff_inproj_body(x_ref, g_ref, wqt_ref, wk_ref, wvt_ref, qt_ref, k_ref, vt_ref):
    nb = qt_ref.shape[0]
    tb = qt_ref.shape[2]
    h = _rms(x_ref[...], g_ref[...]).astype(BF16)
    k_ref[...] = _dot(h, wk_ref[...]).astype(BF16)
    qt = _dot_nt(wqt_ref[...], h) * (DIFF_HEAD_DIM ** -0.5)
    vt = _dot_nt(wvt_ref[...], h)
    for j in range(nb):
        qt_ref[j] = qt[:, j * tb:(j + 1) * tb].astype(BF16)
        vt_ref[j] = vt[:, j * tb:(j + 1) * tb].astype(BF16)


def _diff_inproj(x, gain, w_in, tm=512):
    m, d = x.shape
    dq = 2 * DIFF_HEADS * DIFF_HEAD_DIM
    wqt = w_in[:, :dq].T.astype(BF16)
    wk = w_in[:, dq:2 * dq].astype(BF16)
    wvt = w_in[:, 2 * dq:].T.astype(BF16)
    dv = wvt.shape[0]
    tb = ATT_BLOCK
    nb = tm // tb
    fixed = lambda i: (0, 0)
    return pl.pallas_call(
        _diff_inproj_body,
        grid=(m // tm,),
        in_specs=[
            pl.BlockSpec((tm, d), lambda i: (i, 0)),
            pl.BlockSpec((1, d), fixed),
            pl.BlockSpec(wqt.shape, fixed),
            pl.BlockSpec(wk.shape, fixed),
            pl.BlockSpec(wvt.shape, fixed),
        ],
        out_specs=[
            pl.BlockSpec((nb, dq, tb), lambda i: (i, 0, 0)),
            pl.BlockSpec((tm, dq), lambda i: (i, 0)),
            pl.BlockSpec((nb, dv, tb), lambda i: (i, 0, 0)),
        ],
        out_shape=[
            jax.ShapeDtypeStruct((m // tb, dq, tb), BF16),
            jax.ShapeDtypeStruct((m, dq), BF16),
            jax.ShapeDtypeStruct((m // tb, dv, tb), BF16),
        ],
        compiler_params=_params("parallel"),
        name="diff_inproj",
    )(x, gain.reshape(1, d), wqt, wk, wvt)


def _diff_attn_body(lam_init, slopes_ref, qt_ref, k_ref, vt_ref, lq1_ref, lk1_ref, lq2_ref, lk2_ref, hn_ref,
                    o_ref, m_scr, l_scr, acc_scr):
    tb = ATT_BLOCK
    qi = pl.program_id(2)
    slope = slopes_ref[pl.program_id(1)]

    qt = qt_ref[0]
    half = lax.broadcasted_iota(jnp.int32, qt.shape, 0) < DIFF_HEAD_DIM
    zero = jnp.zeros_like(qt)
    q2t = jnp.concatenate([jnp.where(half, qt, zero), jnp.where(half, zero, qt)], axis=1)

    key = lax.broadcasted_iota(jnp.int32, (tb, 2 * tb), 0)
    qry = lax.broadcasted_iota(jnp.int32, (tb, 2 * tb), 1) % tb
    rel = slope * (key - qry).astype(F32)

    m_scr[...] = jnp.full_like(m_scr, NEG_INF)
    l_scr[...] = jnp.zeros_like(l_scr)
    acc_scr[...] = jnp.zeros_like(acc_scr)

    def step(kb, t_bias, shift):
        m_old = m_scr[...]
        m_new = jnp.maximum(m_old, jnp.max(t_bias, axis=0, keepdims=True) + shift)
        alpha = jnp.exp(m_old - m_new)
        p = jnp.exp(t_bias - (m_new - shift))
        l_scr[...] = alpha * l_scr[...] + jnp.sum(p, axis=0, keepdims=True)
        acc_scr[...] = alpha * acc_scr[...] + _dot(vt_ref[kb], p.astype(BF16))
        m_scr[...] = m_new

    def scores(kb):
        k_blk = k_ref[pl.ds(pl.multiple_of(kb * tb, tb), tb), :]
        return _dot(k_blk, q2t) + rel

    def off_diag(kb, carry):
        step(kb, scores(kb), slope * ((kb - qi) * tb).astype(F32))
        return carry

    lax.fori_loop(0, qi, off_diag, 0)
    step(qi, jnp.where(key <= qry, scores(qi), NEG_INF), 0.0)

    lam = (jnp.exp(jnp.sum(lq1_ref[...] * lk1_ref[...], keepdims=True))
           - jnp.exp(jnp.sum(lq2_ref[...] * lk2_ref[...], keepdims=True)) + lam_init)
    o = acc_scr[...] / l_scr[...]
    ot = o[:, :tb] - lam * o[:, tb:]
    ot = ot * lax.rsqrt(jnp.mean(ot * ot, axis=0, keepdims=True) + EPS)
    ot = ot * hn_ref[...] * (1.0 - lam_init)
    o_ref[...] = ot.T.astype(o_ref.dtype)


def _diff_attn(qt, k, vt, lq1, lk1, lq2, lk2, head_norm, batch, seq, layer_idx):
    tb = ATT_BLOCK
    m, dq = k.shape
    hd = 2 * DIFF_HEAD_DIM
    nq = seq // tb
    lam_init = 0.8 - 0.6 * math.exp(-0.3 * layer_idx)
    slopes = 2.0 ** (-8.0 * jnp.arange(1, DIFF_HEADS + 1, dtype=F32) / DIFF_HEADS)
    vec = lambda a: a.reshape(1, DIFF_HEAD_DIM).astype(F32)
    lam_spec = pl.BlockSpec((1, DIFF_HEAD_DIM), lambda b, h, i: (0, 0))
    return pl.pallas_call(
        functools.partial(_diff_attn_body, lam_init),
        grid=(batch, DIFF_HEADS, nq),
        in_specs=[
            pl.BlockSpec(memory_space=pltpu.SMEM),
            pl.BlockSpec((1, hd, tb), lambda b, h, i: (b * nq + i, h, 0)),
            pl.BlockSpec((seq, hd), lambda b, h, i: (b, h)),
            pl.BlockSpec((nq, hd, tb), lambda b, h, i: (b, h, 0)),
            lam_spec, lam_spec, lam_spec, lam_spec,
            pl.BlockSpec((hd, 1), lambda b, h, i: (0, 0)),
        ],
        out_specs=pl.BlockSpec((tb, hd), lambda b, h, i: (b * nq + i, h)),
        out_shape=jax.ShapeDtypeStruct((m, dq), BF16),
        scratch_shapes=[
            pltpu.VMEM((1, 2 * tb), F32),
            pltpu.VMEM((1, 2 * tb), F32),
            pltpu.VMEM((hd, 2 * tb), F32),
        ],
        compiler_params=_params("parallel", "parallel", "arbitrary"),
        name="diff_attn",
    )(slopes, qt, k, vt, vec(lq1), vec(lk1), vec(lq2), vec(lk2), head_norm.reshape(hd, 1).astype(F32))


def _sgu_body(x_ref, g_ref, win_ref, bin_ref, vn_ref, ws_ref, bs_ref, wout_ref, y_ref):
    tm = x_ref.shape[0]
    width = vn_ref.shape[1]
    gd = width // SGU_GROUPS
    c = SGU_CHUNK
    x = x_ref[...]
    h = _rms(x, g_ref[...]).astype(BF16)
    uv = _dot(h, win_ref[...]) + bin_ref[...]
    uv = 0.5 * uv * (1.0 + jnp.tanh(math.sqrt(2.0 / math.pi) * (uv + 0.044715 * (uv * uv * uv))))
    u = uv[:, :width]
    v = _rms(uv[:, width:], vn_ref[...]).astype(BF16)
    row = lax.broadcasted_iota(jnp.int32, (c, c), 0)
    col = lax.broadcasted_iota(jnp.int32, (c, c), 1)
    causal = col <= row
    cols = []
    for gi in range(SGU_GROUPS):
        w = jnp.where(causal, ws_ref[gi], 0.0).astype(BF16)
        bias = bs_ref[gi]
        rows = [_dot(w, v[n * c:(n + 1) * c, gi * gd:(gi + 1) * gd]) + bias for n in range(tm // c)]
        cols.append(jnp.concatenate(rows, axis=0))
    s = jnp.concatenate(cols, axis=1)
    y_ref[...] = x + _dot((u * s).astype(BF16), wout_ref[...])


def _sgu(x, gain, w_in, b_in, v_norm, w_s, b_s, w_out, tm=256):
    m, d = x.shape
    width = v_norm.shape[0]
    gd = width // SGU_GROUPS
    bs = jnp.broadcast_to(b_s[:, :, None], (SGU_GROUPS, SGU_CHUNK, gd)).astype(F32)
    fixed2 = lambda i: (0, 0)
    fixed3 = lambda i: (0, 0, 0)
    return pl.pallas_call(
        _sgu_body,
        grid=(m // tm,),
        in_specs=[
            pl.BlockSpec((tm, d), lambda i: (i, 0)),
            pl.BlockSpec((1, d), fixed2),
            pl.BlockSpec(w_in.shape, fixed2),
            pl.BlockSpec((1, 2 * width), fixed2),
            pl.BlockSpec((1, width), fixed2),
            pl.BlockSpec(w_s.shape, fixed3),
            pl.BlockSpec(bs.shape, fixed3),
            pl.BlockSpec(w_out.shape, fixed2),
        ],
        out_specs=pl.BlockSpec((tm, d), lambda i: (i, 0)),
        out_shape=jax.ShapeDtypeStruct((m, d), F32),
        compiler_params=_params("parallel"),
        name="sgu",
    )(x, gain.reshape(1, d), w_in.astype(BF16), b_in.reshape(1, 2 * width), v_norm.reshape(1, width),
      w_s, bs, w_out.astype(BF16))


def _gla_layer(x, batch, seq, norm1, w_in, gw1, gw2, gb, head_norm, w_out):
    q, k, v, og, la = _gla_inproj(x, norm1, w_in, gw1, gw2, gb)
    o = _gla_core(q, k, v, og, la, head_norm, batch, seq)
    return _outproj(o, w_out, x)


def _diff_layer(x, batch, seq, norm1, w_in, lq1, lk1, lq2, lk2, head_norm, w_out, layer_idx):
    qt, k, vt = _diff_inproj(x, norm1, w_in)
    o = _diff_attn(qt, k, vt, lq1, lk1, lq2, lk2, head_norm, batch, seq, layer_idx)
    return _outproj(o, w_out, x)


def kernel(x, l0_norm1, l0_w_in, l0_gate_w1, l0_gate_w2, l0_gate_b, l0_head_norm, l0_w_out, l0_norm2, l0_mlp_w1, l0_mlp_w2, l1_norm1, l1_w_in, l1_lambda_q1, l1_lambda_k1, l1_lambda_q2, l1_lambda_k2, l1_head_norm, l1_w_out, l1_norm2, l1_mlp_w1, l1_mlp_w2, l2_norm1, l2_w_in, l2_b_in, l2_v_norm, l2_w_s, l2_b_s, l2_w_out, l2_norm2, l2_mlp_w1, l2_mlp_w2, l3_norm1, l3_w_in, l3_gate_w1, l3_gate_w2, l3_gate_b, l3_head_norm, l3_w_out, l3_norm2, l3_mlp_w1, l3_mlp_w2, final_norm):
    batch, seq, d = x.shape
    h = x.reshape(batch * seq, d)
    h = _gla_layer(h, batch, seq, l0_norm1, l0_w_in, l0_gate_w1, l0_gate_w2, l0_gate_b, l0_head_norm, l0_w_out)
    h = _mlp(h, l0_norm2, l0_mlp_w1, l0_mlp_w2)
    h = _diff_layer(h, batch, seq, l1_norm1, l1_w_in, l1_lambda_q1, l1_lambda_k1, l1_lambda_q2, l1_lambda_k2,
                    l1_head_norm, l1_w_out, 1)
    h = _mlp(h, l1_norm2, l1_mlp_w1, l1_mlp_w2)
    h = _sgu(h, l2_norm1, l2_w_in, l2_b_in, l2_v_norm, l2_w_s, l2_b_s, l2_w_out)
    h = _mlp(h, l2_norm2, l2_mlp_w1, l2_mlp_w2)
    h = _gla_layer(h, batch, seq, l3_norm1, l3_w_in, l3_gate_w1, l3_gate_w2, l3_gate_b, l3_head_norm, l3_w_out)
    h = _mlp(h, l3_norm2, l3_mlp_w1, l3_mlp_w2, final_gain=final_norm)
    return h.reshape(batch, seq, d)
```

```python
import functools
import math

import jax
import jax.numpy as jnp
from jax import lax
from jax.experimental import pallas as pl
from jax.experimental.pallas import tpu as pltpu

F32 = jnp.float32
BF16 = jnp.bfloat16

EPS = 1e-6
NEG_INF = -1e30

GLA_HEADS = 4
GLA_RANK = 16
GLA_GATE_NORM = 16.0
GLA_CHUNK = 64

DIFF_HEAD_DIM = 64
DIFF_HEADS = 8

SGU_CHUNK = 128
SGU_GROUPS = 8

LANES = 128
LOG2E = 1.4426950408889634
VMEM_LIMIT = 52 * 1024 * 1024


def _params(*sem):
    return pltpu.CompilerParams(dimension_semantics=sem, vmem_limit_bytes=VMEM_LIMIT)


def _dot(a, b):
    return jnp.dot(a, b, preferred_element_type=F32)


def _dot_nt(a, b):
    return lax.dot_general(a, b, (((1,), (1,)), ((), ())), preferred_element_type=F32)


def _dot_tn(a, b):
    return lax.dot_general(a, b, (((0,), (0,)), ((), ())), preferred_element_type=F32)


def _rms(x, g):
    return x * lax.rsqrt(jnp.mean(x * x, axis=-1, keepdims=True) + EPS) * g


def _gla_inproj_body(x_ref, g_ref, w_ref, gw1_ref, gw2_ref, gb_ref, q_ref, k_ref, v_ref, og_ref, la_ref):
    dk = q_ref.shape[1]
    dv = v_ref.shape[1]
    h = _rms(x_ref[...], g_ref[...]).astype(BF16)
    q_ref[...] = _dot(h, w_ref[:, 0:dk])
    k_ref[...] = _dot(h, w_ref[:, dk:2 * dk])
    v_ref[...] = _dot(h, w_ref[:, 2 * dk:2 * dk + dv]).astype(BF16)
    og_ref[...] = _dot(h, w_ref[:, 2 * dk + dv:2 * dk + 2 * dv])
    t = _dot(h, gw1_ref[...]).astype(BF16)
    z = _dot(t, gw2_ref[...]) + gb_ref[...]
    la_ref[...] = (jnp.minimum(z, 0.0) - jnp.log(1.0 + jnp.exp(-jnp.abs(z)))) / GLA_GATE_NORM


def _gla_inproj(x, gain, w_in, gw1, gw2, gb, tm=512):
    m, d = x.shape
    dk = gw2.shape[1]
    dv = (w_in.shape[1] - 2 * dk) // 2
    gw1p = jnp.zeros((d, LANES), BF16).at[:, :GLA_RANK].set(gw1.astype(BF16))
    gw2p = jnp.zeros((LANES, dk), BF16).at[:GLA_RANK, :].set(gw2.astype(BF16))
    row = lambda i: (i, 0)
    fixed = lambda i: (0, 0)
    return pl.pallas_call(
        _gla_inproj_body,
        grid=(m // tm,),
        in_specs=[
            pl.BlockSpec((tm, d), row),
            pl.BlockSpec((1, d), fixed),
            pl.BlockSpec(w_in.shape, fixed),
            pl.BlockSpec(gw1p.shape, fixed),
            pl.BlockSpec(gw2p.shape, fixed),
            pl.BlockSpec((1, dk), fixed),
        ],
        out_specs=[
            pl.BlockSpec((tm, dk), row),
            pl.BlockSpec((tm, dk), row),
            pl.BlockSpec((tm, dv), row),
            pl.BlockSpec((tm, dv), row),
            pl.BlockSpec((tm, dk), row),
        ],
        out_shape=[
            jax.ShapeDtypeStruct((m, dk), F32),
            jax.ShapeDtypeStruct((m, dk), F32),
            jax.ShapeDtypeStruct((m, dv), BF16),
            jax.ShapeDtypeStruct((m, dv), F32),
            jax.ShapeDtypeStruct((m, dk), F32),
        ],
        compiler_params=_params("parallel"),
        name="gla_inproj",
    )(x, gain.reshape(1, d), w_in.astype(BF16), gw1p, gw2p, gb.reshape(1, dk))


def _gla_body(tri_ref, q_ref, k_ref, v_ref, og_ref, la_ref, hn_ref, o_ref, st_ref):
    c = GLA_CHUNK
    tc, hk = q_ref.shape
    n_chunks = tc // c

    @pl.when(pl.program_id(2) == 0)
    def _():
        st_ref[...] = jnp.zeros_like(st_ref)

    la = la_ref[...]
    la_hi = la.astype(BF16)
    la_lo = (la - la_hi.astype(F32)).astype(BF16)
    tri = tri_ref[...]
    b = _dot(tri, la_hi) + _dot(tri, la_lo)

    q = q_ref[...] * (hk ** -0.5)
    k = k_ref[...]
    q_dec = (q * jnp.exp(b)).astype(BF16)
    k_inv = (k * jnp.exp(-b)).astype(BF16)

    row = lax.broadcasted_iota(jnp.int32, (c, c), 0)
    col = lax.broadcasted_iota(jnp.int32, (c, c), 1)
    causal = col <= row
    hn = hn_ref[...]

    for n in range(n_chunks):
        sl = slice(n * c, (n + 1) * c)
        b_n = b[sl]
        b_last = b_n[c - 1:c, :]
        k_end = (k[sl] * jnp.exp(b_last - b_n)).astype(BF16)
        qd = q_dec[sl]
        v_n = v_ref[sl, :]
        att = jnp.where(causal, _dot_nt(qd, k_inv[sl]), 0.0).astype(BF16)
        st = st_ref[...]
        o = _dot(att, v_n) + _dot_nt(qd, st.astype(BF16))
        st_ref[...] = jnp.exp(b_last) * st + _dot_tn(v_n, k_end)
        o = o * lax.rsqrt(jnp.mean(o * o, axis=-1, keepdims=True) + EPS) * hn
        g = og_ref[sl, :]
        o = o * (g / (1.0 + jnp.exp(-g)))
        o_ref[sl, :] = o.astype(o_ref.dtype)


def _gla_core(q, k, v, og, la, head_norm, batch, seq, tc=256):
    m, dk = q.shape
    dv = v.shape[1]
    hk = dk // GLA_HEADS
    hv = dv // GLA_HEADS
    nt = seq // tc
    idx = jnp.arange(tc)
    tri = ((idx[:, None] // GLA_CHUNK == idx[None, :] // GLA_CHUNK)
           & (idx[None, :] <= idx[:, None])).astype(BF16)
    off = dk // hk
    rows = lambda b, h, t: b * nt + t
    return pl.pallas_call(
        _gla_body,
        grid=(batch, GLA_HEADS, nt),
        in_specs=[
            pl.BlockSpec((tc, tc), lambda b, h, t: (0, 0)),
            pl.BlockSpec((tc, hk), lambda b, h, t: (rows(b, h, t), h)),
            pl.BlockSpec((tc, hk), lambda b, h, t: (rows(b, h, t), h)),
            pl.BlockSpec((tc, hv), lambda b, h, t: (rows(b, h, t), h)),
            pl.BlockSpec((tc, hv), lambda b, h, t: (rows(b, h, t), h)),
            pl.BlockSpec((tc, hk), lambda b, h, t: (rows(b, h, t), h)),
            pl.BlockSpec((1, hv), lambda b, h, t: (0, 0)),
        ],
        out_specs=pl.BlockSpec((tc, hv), lambda b, h, t: (rows(b, h, t), h)),
        out_shape=jax.ShapeDtypeStruct((m, dv), BF16),
        scratch_shapes=[pltpu.VMEM((hv, hk), F32)],
        compiler_params=_params("parallel", "parallel", "arbitrary"),
        name="gla_core",
    )(tri, q, k, v, og, la, head_norm.reshape(1, hv))


def _outproj_body(o_ref, w_ref, x_ref, y_ref):
    y_ref[...] = x_ref[...] + _dot(o_ref[...], w_ref[...])


def _outproj(o, w, x, tm=512):
    m, d = x.shape
    kdim = o.shape[1]
    return pl.pallas_call(
        _outproj_body,
        grid=(m // tm,),
        in_specs=[
            pl.BlockSpec((tm, kdim), lambda i: (i, 0)),
            pl.BlockSpec((kdim, d), lambda i: (0, 0)),
            pl.BlockSpec((tm, d), lambda i: (i, 0)),
        ],
        out_specs=pl.BlockSpec((tm, d), lambda i: (i, 0)),
        out_shape=jax.ShapeDtypeStruct((m, d), F32),
        compiler_params=_params("parallel"),
        name="outproj",
    )(o, w.astype(BF16), x)


def _mlp_body(final, x_ref, g_ref, w1_ref, w2_ref, gf_ref, y_ref, h_scr, acc_scr):
    f = pl.program_id(1)

    @pl.when(f == 0)
    def _():
        h_scr[...] = _rms(x_ref[...], g_ref[...]).astype(BF16)
        acc_scr[...] = jnp.zeros_like(acc_scr)

    a = jnp.maximum(_dot(h_scr[...], w1_ref[...]), 0.0)
    acc_scr[...] += _dot((a * a).astype(BF16), w2_ref[...])

    @pl.when(f == pl.num_programs(1) - 1)
    def _():
        y = x_ref[...] + acc_scr[...]
        if final:
            y = _rms(y, gf_ref[...])
        y_ref[...] = y


def _mlp(x, gain, w1, w2, final_gain=None, tm=1024, tf=512):
    m, d = x.shape
    dff = w1.shape[1]
    final = final_gain is not None
    gf = (final_gain if final else gain).reshape(1, d)
    return pl.pallas_call(
        functools.partial(_mlp_body, final),
        grid=(m // tm, dff // tf),
        in_specs=[
            pl.BlockSpec((tm, d), lambda i, f: (i, 0)),
            pl.BlockSpec((1, d), lambda i, f: (0, 0)),
            pl.BlockSpec((d, tf), lambda i, f: (0, f)),
            pl.BlockSpec((tf, d), lambda i, f: (f, 0)),
            pl.BlockSpec((1, d), lambda i, f: (0, 0)),
        ],
        out_specs=pl.BlockSpec((tm, d), lambda i, f: (i, 0)),
        out_shape=jax.ShapeDtypeStruct((m, d), F32),
        scratch_shapes=[pltpu.VMEM((tm, d), BF16), pltpu.VMEM((tm, d), F32)],
        compiler_params=_params("parallel", "arbitrary"),
        name="mlp",
    )(x, gain.reshape(1, d), w1.astype(BF16), w2.astype(BF16), gf)


ATT_BLOCK = 256


def _diff_inproj_body(x_ref, g_ref, wqt_ref, wk_ref, wvt_ref, qt_ref, k_ref, vt_ref):
    nb = qt_ref.shape[0]
    tb = qt_ref.shape[2]
    h = _rms(x_ref[...], g_ref[...]).astype(BF16)
    k_ref[...] = _dot(h, wk_ref[...]).astype(BF16)
    qt = _dot_nt(wqt_ref[...], h) * (LOG2E * DIFF_HEAD_DIM ** -0.5)
    vt = _dot_nt(wvt_ref[...], h)
    for j in range(nb):
        qt_ref[j] = qt[:, j * tb:(j + 1) * tb].astype(BF16)
        vt_ref[j] = vt[:, j * tb:(j + 1) * tb].astype(BF16)


def _diff_inproj(x, gain, w_in, tm=512):
    m, d = x.shape
    dq = 2 * DIFF_HEADS * DIFF_HEAD_DIM
    wqt = w_in[:, :dq].T.astype(BF16)
    wk = w_in[:, dq:2 * dq].astype(BF16)
    wvt = w_in[:, 2 * dq:].T.astype(BF16)
    dv = wvt.shape[0]
    tb = ATT_BLOCK
    nb = tm // tb
    fixed = lambda i: (0, 0)
    return pl.pallas_call(
        _diff_inproj_body,
        grid=(m // tm,),
        in_specs=[
            pl.BlockSpec((tm, d), lambda i: (i, 0)),
            pl.BlockSpec((1, d), fixed),
            pl.BlockSpec(wqt.shape, fixed),
            pl.BlockSpec(wk.shape, fixed),
            pl.BlockSpec(wvt.shape, fixed),
        ],
        out_specs=[
            pl.BlockSpec((nb, dq, tb), lambda i: (i, 0, 0)),
            pl.BlockSpec((tm, dq), lambda i: (i, 0)),
            pl.BlockSpec((nb, dv, tb), lambda i: (i, 0, 0)),
        ],
        out_shape=[
            jax.ShapeDtypeStruct((m // tb, dq, tb), BF16),
            jax.ShapeDtypeStruct((m, dq), BF16),
            jax.ShapeDtypeStruct((m // tb, dv, tb), BF16),
        ],
        compiler_params=_params("parallel"),
        name="diff_inproj",
    )(x, gain.reshape(1, d), wqt, wk, wvt)


ATT_HEADS_PER_STEP = 4


def _diff_attn_body(lam_init, slopes_ref, qt_ref, k_ref, vt_ref, lq1_ref, lk1_ref, lq2_ref, lk2_ref, hn_ref,
                    o_ref, m_scr, l_scr, acc_scr):
    tb = ATT_BLOCK
    hd = 2 * DIFF_HEAD_DIM
    nh = ATT_HEADS_PER_STEP
    qi = pl.program_id(2)
    hg = pl.program_id(1)

    key = lax.broadcasted_iota(jnp.int32, (tb, 2 * tb), 0)
    qry = lax.broadcasted_iota(jnp.int32, (tb, 2 * tb), 1) % tb
    visible = key <= qry
    key_f = lax.broadcasted_iota(jnp.int32, (tb, LANES), 0).astype(F32)
    half = lax.broadcasted_iota(jnp.int32, (hd, tb), 0) < DIFF_HEAD_DIM

    heads = []
    for j in range(nh):
        slope = slopes_ref[hg * nh + j] * LOG2E
        qt = qt_ref[0, j * hd:(j + 1) * hd, :]
        zero = jnp.zeros_like(qt)
        q2t = jnp.concatenate([jnp.where(half, qt, zero), jnp.where(half, zero, qt)], axis=1)
        bias = jnp.concatenate([slope * key_f] * (2 * tb // LANES), axis=1)
        heads.append((slope, q2t, bias))

    m_scr[...] = jnp.full_like(m_scr, NEG_INF)
    l_scr[...] = jnp.zeros_like(l_scr)
    acc_scr[...] = jnp.zeros_like(acc_scr)

    def scores(j, kb):
        k_blk = k_ref[pl.ds(pl.multiple_of(kb * tb, tb), tb), j * hd:(j + 1) * hd]
        return _dot(k_blk, heads[j][1]) + heads[j][2]

    def update(j, kb, t, shift):
        m_old = m_scr[j]
        m_new = jnp.maximum(m_old, jnp.max(t, axis=0, keepdims=True) + shift)
        alpha = jnp.exp2(m_old - m_new)
        p = jnp.exp2(t - (m_new - shift))
        l_scr[j] = alpha * l_scr[j] + jnp.sum(p, axis=0, keepdims=True)
        acc_scr[j] = alpha * acc_scr[j] + _dot(vt_ref[kb, j * hd:(j + 1) * hd, :], p.astype(BF16))
        m_scr[j] = m_new

    def off_diag(kb, carry):
        ts = [scores(j, kb) for j in range(nh)]
        for j in range(nh):
            update(j, kb, ts[j], heads[j][0] * ((kb - qi) * tb).astype(F32))
        return carry

    lax.fori_loop(0, qi, off_diag, 0)
    ts = [scores(j, qi) for j in range(nh)]
    for j in range(nh):
        update(j, qi, jnp.where(visible, ts[j], NEG_INF), 0.0)

    lam = (jnp.exp(jnp.sum(lq1_ref[...] * lk1_ref[...], keepdims=True))
           - jnp.exp(jnp.sum(lq2_ref[...] * lk2_ref[...], keepdims=True)) + lam_init)
    for j in range(nh):
        o = acc_scr[j] / l_scr[j]
        ot = o[:, :tb] - lam * o[:, tb:]
        ot = ot * lax.rsqrt(jnp.mean(ot * ot, axis=0, keepdims=True) + EPS)
        ot = ot * hn_ref[...] * (1.0 - lam_init)
        o_ref[:, j * hd:(j + 1) * hd] = ot.T.astype(o_ref.dtype)


def _diff_attn(qt, k, vt, lq1, lk1, lq2, lk2, head_norm, batch, seq, layer_idx):
    tb = ATT_BLOCK
    nh = ATT_HEADS_PER_STEP
    m, dq = k.shape
    hd = 2 * DIFF_HEAD_DIM
    nq = seq // tb
    lam_init = 0.8 - 0.6 * math.exp(-0.3 * layer_idx)
    slopes = 2.0 ** (-8.0 * jnp.arange(1, DIFF_HEADS + 1, dtype=F32) / DIFF_HEADS)
    vec = lambda a: a.reshape(1, DIFF_HEAD_DIM).astype(F32)
    lam_spec = pl.BlockSpec((1, DIFF_HEAD_DIM), lambda b, h, i: (0, 0))
    return pl.pallas_call(
        functools.partial(_diff_attn_body, lam_init),
        grid=(batch, DIFF_HEADS // nh, nq),
        in_specs=[
            pl.BlockSpec(memory_space=pltpu.SMEM),
            pl.BlockSpec((1, nh * hd, tb), lambda b, h, i: (b * nq + i, h, 0)),
            pl.BlockSpec((seq, nh * hd), lambda b, h, i: (b, h)),
            pl.BlockSpec((nq, nh * hd, tb), lambda b, h, i: (b, h, 0)),
            lam_spec, lam_spec, lam_spec, lam_spec,
            pl.BlockSpec((hd, 1), lambda b, h, i: (0, 0)),
        ],
        out_specs=pl.BlockSpec((tb, nh * hd), lambda b, h, i: (b * nq + i, h)),
        out_shape=jax.ShapeDtypeStruct((m, dq), BF16),
        scratch_shapes=[
            pltpu.VMEM((nh, 1, 2 * tb), F32),
            pltpu.VMEM((nh, 1, 2 * tb), F32),
            pltpu.VMEM((nh, hd, 2 * tb), F32),
        ],
        compiler_params=_params("parallel", "parallel", "arbitrary"),
        name="diff_attn",
    )(slopes, qt, k, vt, vec(lq1), vec(lk1), vec(lq2), vec(lk2), head_norm.reshape(hd, 1).astype(F32))


def _sgu_body(x_ref, g_ref, win_ref, bin_ref, vn_ref, ws_ref, bs_ref, wout_ref, y_ref):
    tm = x_ref.shape[0]
    width = vn_ref.shape[1]
    gd = width // SGU_GROUPS
    c = SGU_CHUNK
    x = x_ref[...]
    h = _rms(x, g_ref[...]).astype(BF16)
    uv = _dot(h, win_ref[...]) + bin_ref[...]
    uv = 0.5 * uv * (1.0 + jnp.tanh(math.sqrt(2.0 / math.pi) * (uv + 0.044715 * (uv * uv * uv))))
    u = uv[:, :width]
    v = _rms(uv[:, width:], vn_ref[...]).astype(BF16)
    row = lax.broadcasted_iota(jnp.int32, (c, c), 0)
    col = lax.broadcasted_iota(jnp.int32, (c, c), 1)
    causal = col <= row
    cols = []
    for gi in range(SGU_GROUPS):
        w = jnp.where(causal, ws_ref[gi], 0.0).astype(BF16)
        bias = bs_ref[gi]
        rows = [_dot(w, v[n * c:(n + 1) * c, gi * gd:(gi + 1) * gd]) + bias for n in range(tm // c)]
        cols.append(jnp.concatenate(rows, axis=0))
    s = jnp.concatenate(cols, axis=1)
    y_ref[...] = x + _dot((u * s).astype(BF16), wout_ref[...])


def _sgu(x, gain, w_in, b_in, v_norm, w_s, b_s, w_out, tm=256):
    m, d = x.shape
    width = v_norm.shape[0]
    gd = width // SGU_GROUPS
    bs = jnp.broadcast_to(b_s[:, :, None], (SGU_GROUPS, SGU_CHUNK, gd)).astype(F32)
    fixed2 = lambda i: (0, 0)
    fixed3 = lambda i: (0, 0, 0)
    return pl.pallas_call(
        _sgu_body,
        grid=(m // tm,),
        in_specs=[
            pl.BlockSpec((tm, d), lambda i: (i, 0)),
            pl.BlockSpec((1, d), fixed2),
            pl.BlockSpec(w_in.shape, fixed2),
            pl.BlockSpec((1, 2 * width), fixed2),
            pl.BlockSpec((1, width), fixed2),
            pl.BlockSpec(w_s.shape, fixed3),
            pl.BlockSpec(bs.shape, fixed3),
            pl.BlockSpec(w_out.shape, fixed2),
        ],
        out_specs=pl.BlockSpec((tm, d), lambda i: (i, 0)),
        out_shape=jax.ShapeDtypeStruct((m, d), F32),
        compiler_params=_params("parallel"),
        name="sgu",
    )(x, gain.reshape(1, d), w_in.astype(BF16), b_in.reshape(1, 2 * width), v_norm.reshape(1, width),
      w_s, bs, w_out.astype(BF16))


def _gla_layer(x, batch, seq, norm1, w_in, gw1, gw2, gb, head_norm, w_out):
    q, k, v, og, la = _gla_inproj(x, norm1, w_in, gw1, gw2, gb)
    o = _gla_core(q, k, v, og, la, head_norm, batch, seq)
    return _outproj(o, w_out, x)


def _diff_layer(x, batch, seq, norm1, w_in, lq1, lk1, lq2, lk2, head_norm, w_out, layer_idx):
    qt, k, vt = _diff_inproj(x, norm1, w_in)
    o = _diff_attn(qt, k, vt, lq1, lk1, lq2, lk2, head_norm, batch, seq, layer_idx)
    return _outproj(o, w_out, x)


def kernel(x, l0_norm1, l0_w_in, l0_gate_w1, l0_gate_w2, l0_gate_b, l0_head_norm, l0_w_out, l0_norm2, l0_mlp_w1, l0_mlp_w2, l1_norm1, l1_w_in, l1_lambda_q1, l1_lambda_k1, l1_lambda_q2, l1_lambda_k2, l1_head_norm, l1_w_out, l1_norm2, l1_mlp_w1, l1_mlp_w2, l2_norm1, l2_w_in, l2_b_in, l2_v_norm, l2_w_s, l2_b_s, l2_w_out, l2_norm2, l2_mlp_w1, l2_mlp_w2, l3_norm1, l3_w_in, l3_gate_w1, l3_gate_w2, l3_gate_b, l3_head_norm, l3_w_out, l3_norm2, l3_mlp_w1, l3_mlp_w2, final_norm):
    batch, seq, d = x.shape
    h = x.reshape(batch * seq, d)
    h = _gla_layer(h, batch, seq, l0_norm1, l0_w_in, l0_gate_w1, l0_gate_w2, l0_gate_b, l0_head_norm, l0_w_out)
    h = _mlp(h, l0_norm2, l0_mlp_w1, l0_mlp_w2)
    h = _diff_layer(h, batch, seq, l1_norm1, l1_w_in, l1_lambda_q1, l1_lambda_k1, l1_lambda_q2, l1_lambda_k2,
                    l1_head_norm, l1_w_out, 1)
    h = _mlp(h, l1_norm2, l1_mlp_w1, l1_mlp_w2)
    h = _sgu(h, l2_norm1, l2_w_in, l2_b_in, l2_v_norm, l2_w_s, l2_b_s, l2_w_out)
    h = _mlp(h, l2_norm2, l2_mlp_w1, l2_mlp_w2)
    h = _gla_layer(h, batch, seq, l3_norm1, l3_w_in, l3_gate_w1, l3_gate_w2, l3_gate_b, l3_head_norm, l3_w_out)
    h = _mlp(h, l3_norm2, l3_mlp_w1, l3_mlp_w2, final_gain=final_norm)
    return h.reshape(batch, seq, d)
```

```python
import functools
import math

import jax
import jax.numpy as jnp
from jax import lax
from jax.experimental import pallas as pl
from jax.experimental.pallas import tpu as pltpu

F32 = jnp.float32
BF16 = jnp.bfloat16

EPS = 1e-6
NEG_INF = -1e30

GLA_HEADS = 4
GLA_RANK = 16
GLA_GATE_NORM = 16.0
GLA_CHUNK = 64

DIFF_HEAD_DIM = 64
DIFF_HEADS = 8

SGU_CHUNK = 128
SGU_GROUPS = 8

LANES = 128
LOG2E = 1.4426950408889634
VMEM_LIMIT = 52 * 1024 * 1024


def _params(*sem):
    return pltpu.CompilerParams(dimension_semantics=sem, vmem_limit_bytes=VMEM_LIMIT)


def _dot(a, b):
    return jnp.dot(a, b, preferred_element_type=F32)


def _dot_nt(a, b):
    return lax.dot_general(a, b, (((1,), (1,)), ((), ())), preferred_element_type=F32)


def _dot_tn(a, b):
    return lax.dot_general(a, b, (((0,), (0,)), ((), ())), preferred_element_type=F32)


def _rms(x, g):
    return x * lax.rsqrt(jnp.mean(x * x, axis=-1, keepdims=True) + EPS) * g


def _gla_inproj_body(x_ref, g_ref, w_ref, gw1_ref, gw2_ref, gb_ref, q_ref, k_ref, v_ref, og_ref, la_ref):
    dk = q_ref.shape[1]
    dv = v_ref.shape[1]
    h = _rms(x_ref[...], g_ref[...]).astype(BF16)
    q_ref[...] = _dot(h, w_ref[:, 0:dk])
    k_ref[...] = _dot(h, w_ref[:, dk:2 * dk])
    v_ref[...] = _dot(h, w_ref[:, 2 * dk:2 * dk + dv]).astype(BF16)
    og_ref[...] = _dot(h, w_ref[:, 2 * dk + dv:2 * dk + 2 * dv])
    t = _dot(h, gw1_ref[...]).astype(BF16)
    z = _dot(t, gw2_ref[...]) + gb_ref[...]
    la_ref[...] = (jnp.minimum(z, 0.0) - jnp.log(1.0 + jnp.exp(-jnp.abs(z)))) / GLA_GATE_NORM


def _gla_inproj(x, gain, w_in, gw1, gw2, gb, tm=512):
    m, d = x.shape
    dk = gw2.shape[1]
    dv = (w_in.shape[1] - 2 * dk) // 2
    gw1p = jnp.zeros((d, LANES), BF16).at[:, :GLA_RANK].set(gw1.astype(BF16))
    gw2p = jnp.zeros((LANES, dk), BF16).at[:GLA_RANK, :].set(gw2.astype(BF16))
    row = lambda i: (i, 0)
    fixed = lambda i: (0, 0)
    return pl.pallas_call(
        _gla_inproj_body,
        grid=(m // tm,),
        in_specs=[
            pl.BlockSpec((tm, d), row),
            pl.BlockSpec((1, d), fixed),
            pl.BlockSpec(w_in.shape, fixed),
            pl.BlockSpec(gw1p.shape, fixed),
            pl.BlockSpec(gw2p.shape, fixed),
            pl.BlockSpec((1, dk), fixed),
        ],
        out_specs=[
            pl.BlockSpec((tm, dk), row),
            pl.BlockSpec((tm, dk), row),
            pl.BlockSpec((tm, dv), row),
            pl.BlockSpec((tm, dv), row),
            pl.BlockSpec((tm, dk), row),
        ],
        out_shape=[
            jax.ShapeDtypeStruct((m, dk), F32),
            jax.ShapeDtypeStruct((m, dk), F32),
            jax.ShapeDtypeStruct((m, dv), BF16),
            jax.ShapeDtypeStruct((m, dv), F32),
            jax.ShapeDtypeStruct((m, dk), F32),
        ],
        compiler_params=_params("parallel"),
        name="gla_inproj",
    )(x, gain.reshape(1, d), w_in.astype(BF16), gw1p, gw2p, gb.reshape(1, dk))


def _gla_body(tri_ref, q_ref, k_ref, v_ref, og_ref, la_ref, hn_ref, o_ref, st_ref):
    c = GLA_CHUNK
    tc, hk = q_ref.shape
    n_chunks = tc // c

    @pl.when(pl.program_id(2) == 0)
    def _():
        st_ref[...] = jnp.zeros_like(st_ref)

    la = la_ref[...]
    la_hi = la.astype(BF16)
    la_lo = (la - la_hi.astype(F32)).astype(BF16)
    tri = tri_ref[...]
    b = _dot(tri, la_hi) + _dot(tri, la_lo)

    q = q_ref[...] * (hk ** -0.5)
    k = k_ref[...]
    q_dec = (q * jnp.exp(b)).astype(BF16)
    k_inv = (k * jnp.exp(-b)).astype(BF16)

    row = lax.broadcasted_iota(jnp.int32, (c, c), 0)
    col = lax.broadcasted_iota(jnp.int32, (c, c), 1)
    causal = col <= row
    hn = hn_ref[...]

    for n in range(n_chunks):
        sl = slice(n * c, (n + 1) * c)
        b_n = b[sl]
        b_last = b_n[c - 1:c, :]
        k_end = (k[sl] * jnp.exp(b_last - b_n)).astype(BF16)
        qd = q_dec[sl]
        v_n = v_ref[sl, :]
        att = jnp.where(causal, _dot_nt(qd, k_inv[sl]), 0.0).astype(BF16)
        st = st_ref[...]
        o = _dot(att, v_n) + _dot_nt(qd, st.astype(BF16))
        st_ref[...] = jnp.exp(b_last) * st + _dot_tn(v_n, k_end)
        o = o * lax.rsqrt(jnp.mean(o * o, axis=-1, keepdims=True) + EPS) * hn
        g = og_ref[sl, :]
        o = o * (g / (1.0 + jnp.exp(-g)))
        o_ref[sl, :] = o.astype(o_ref.dtype)


def _gla_core(q, k, v, og, la, head_norm, batch, seq, tc=256):
    m, dk = q.shape
    dv = v.shape[1]
    hk = dk // GLA_HEADS
    hv = dv // GLA_HEADS
    nt = seq // tc
    idx = jnp.arange(tc)
    tri = ((idx[:, None] // GLA_CHUNK == idx[None, :] // GLA_CHUNK)
           & (idx[None, :] <= idx[:, None])).astype(BF16)
    blk = lambda b, h, t: (b * nt + t, h)
    return pl.pallas_call(
        _gla_body,
        grid=(batch, GLA_HEADS, nt),
        in_specs=[
            pl.BlockSpec((tc, tc), lambda b, h, t: (0, 0)),
            pl.BlockSpec((tc, hk), blk),
            pl.BlockSpec((tc, hk), blk),
            pl.BlockSpec((tc, hv), blk),
            pl.BlockSpec((tc, hv), blk),
            pl.BlockSpec((tc, hk), blk),
            pl.BlockSpec((1, hv), lambda b, h, t: (0, 0)),
        ],
        out_specs=pl.BlockSpec((tc, hv), blk),
        out_shape=jax.ShapeDtypeStruct((m, dv), BF16),
        scratch_shapes=[pltpu.VMEM((hv, hk), F32)],
        compiler_params=_params("parallel", "parallel", "arbitrary"),
        name="gla_core",
    )(tri, q, k, v, og, la, head_norm.reshape(1, hv))


def _outproj_body(o_ref, w_ref, x_ref, y_ref):
    y_ref[...] = x_ref[...] + _dot(o_ref[...], w_ref[...])


def _outproj(o, w, x, tm=512):
    m, d = x.shape
    kdim = o.shape[1]
    return pl.pallas_call(
        _outproj_body,
        grid=(m // tm,),
        in_specs=[
            pl.BlockSpec((tm, kdim), lambda i: (i, 0)),
            pl.BlockSpec((kdim, d), lambda i: (0, 0)),
            pl.BlockSpec((tm, d), lambda i: (i, 0)),
        ],
        out_specs=pl.BlockSpec((tm, d), lambda i: (i, 0)),
        out_shape=jax.ShapeDtypeStruct((m, d), F32),
        compiler_params=_params("parallel"),
        name="outproj",
    )(o, w.astype(BF16), x)


def _mlp_body(final, x_ref, g_ref, w1_ref, w2_ref, gf_ref, y_ref, h_scr, acc_scr):
    f = pl.program_id(1)

    @pl.when(f == 0)
    def _():
        h_scr[...] = _rms(x_ref[...], g_ref[...]).astype(BF16)
        acc_scr[...] = jnp.zeros_like(acc_scr)

    a = jnp.maximum(_dot(h_scr[...], w1_ref[...]), 0.0)
    acc_scr[...] += _dot((a * a).astype(BF16), w2_ref[...])

    @pl.when(f == pl.num_programs(1) - 1)
    def _():
        y = x_ref[...] + acc_scr[...]
        if final:
            y = _rms(y, gf_ref[...])
        y_ref[...] = y


def _mlp(x, gain, w1, w2, final_gain=None, tm=1024, tf=512):
    m, d = x.shape
    dff = w1.shape[1]
    final = final_gain is not None
    gf = (final_gain if final else gain).reshape(1, d)
    return pl.pallas_call(
        functools.partial(_mlp_body, final),
        grid=(m // tm, dff // tf),
        in_specs=[
            pl.BlockSpec((tm, d), lambda i, f: (i, 0)),
            pl.BlockSpec((1, d), lambda i, f: (0, 0)),
            pl.BlockSpec((d, tf), lambda i, f: (0, f)),
            pl.BlockSpec((tf, d), lambda i, f: (f, 0)),
            pl.BlockSpec((1, d), lambda i, f: (0, 0)),
        ],
        out_specs=pl.BlockSpec((tm, d), lambda i, f: (i, 0)),
        out_shape=jax.ShapeDtypeStruct((m, d), F32),
        scratch_shapes=[pltpu.VMEM((tm, d), BF16), pltpu.VMEM((tm, d), F32)],
        compiler_params=_params("parallel", "arbitrary"),
        name="mlp",
    )(x, gain.reshape(1, d), w1.astype(BF16), w2.astype(BF16), gf)


Q_BLOCK = 256
KEY_BLOCK = 512


def _diff_inproj_body(x_ref, g_ref, wqt_ref, wk_ref, wvt_ref, qt_ref, k_ref, vt_ref):
    tq = qt_ref.shape[2]
    tk = vt_ref.shape[2]
    h = _rms(x_ref[...], g_ref[...]).astype(BF16)
    k_ref[...] = _dot(h, wk_ref[...]).astype(BF16)
    qt = _dot_nt(wqt_ref[...], h) * (LOG2E * DIFF_HEAD_DIM ** -0.5)
    vt = _dot_nt(wvt_ref[...], h)
    for j in range(qt_ref.shape[0]):
        qt_ref[j] = qt[:, j * tq:(j + 1) * tq].astype(BF16)
    for j in range(vt_ref.shape[0]):
        vt_ref[j] = vt[:, j * tk:(j + 1) * tk].astype(BF16)


def _diff_inproj(x, gain, w_in, tm=512):
    m, d = x.shape
    dq = 2 * DIFF_HEADS * DIFF_HEAD_DIM
    wqt = w_in[:, :dq].T.astype(BF16)
    wk = w_in[:, dq:2 * dq].astype(BF16)
    wvt = w_in[:, 2 * dq:].T.astype(BF16)
    dv = wvt.shape[0]
    tq, tk = Q_BLOCK, KEY_BLOCK
    fixed = lambda i: (0, 0)
    return pl.pallas_call(
        _diff_inproj_body,
        grid=(m // tm,),
        in_specs=[
            pl.BlockSpec((tm, d), lambda i: (i, 0)),
            pl.BlockSpec((1, d), fixed),
            pl.BlockSpec(wqt.shape, fixed),
            pl.BlockSpec(wk.shape, fixed),
            pl.BlockSpec(wvt.shape, fixed),
        ],
        out_specs=[
            pl.BlockSpec((tm // tq, dq, tq), lambda i: (i, 0, 0)),
            pl.BlockSpec((tm, dq), lambda i: (i, 0)),
            pl.BlockSpec((tm // tk, dv, tk), lambda i: (i, 0, 0)),
        ],
        out_shape=[
            jax.ShapeDtypeStruct((m // tq, dq, tq), BF16),
            jax.ShapeDtypeStruct((m, dq), BF16),
            jax.ShapeDtypeStruct((m // tk, dv, tk), BF16),
        ],
        compiler_params=_params("parallel"),
        name="diff_inproj",
    )(x, gain.reshape(1, d), wqt, wk, wvt)


ATT_HEADS_PER_STEP = 4
KEY_SPLIT = 32
ONES_ROWS = 16


def _bf16_pieces(c):
    c1 = c.astype(BF16).astype(F32)
    c2 = (c - c1).astype(BF16).astype(F32)
    c3 = (c - c1 - c2).astype(BF16).astype(F32)
    return c1, c2, c3


def _diff_attn_body(lam_init, slopes_ref, qt_ref, k_ref, vt_ref, lq1_ref, lk1_ref, lq2_ref, lk2_ref, hn_ref,
                    o_ref, m_scr, acc_scr):
    tq, tk = Q_BLOCK, KEY_BLOCK
    hd = 2 * DIFF_HEAD_DIM
    nh = ATT_HEADS_PER_STEP
    qi = pl.program_id(2)
    hg = pl.program_id(1)
    n_full = qi // (tk // tq)
    odd = qi % (tk // tq)

    key = lax.broadcasted_iota(jnp.int32, (tk, 2 * tq), 0)
    qry = lax.broadcasted_iota(jnp.int32, (tk, 2 * tq), 1) % tq
    visible = (key - qry) <= odd * tq

    kj = lax.broadcasted_iota(jnp.int32, (tk, LANES), 0)
    kl = lax.broadcasted_iota(jnp.int32, (tk, LANES), 1)
    kfeat = jnp.where(kl < 3, kj // KEY_SPLIT, jnp.where(kl < 6, kj % KEY_SPLIT, 0)).astype(F32).astype(BF16)
    ones = jnp.ones((ONES_ROWS, tk), BF16)
    half = lax.broadcasted_iota(jnp.int32, (hd, tq), 0) < DIFF_HEAD_DIM
    frow = lax.broadcasted_iota(jnp.int32, (hd, 2 * tq), 0)

    heads = []
    for j in range(nh):
        slope = slopes_ref[hg * nh + j] * LOG2E
        qt = qt_ref[0, j * hd:(j + 1) * hd, :]
        zero = jnp.zeros_like(qt)
        q2t = jnp.concatenate([jnp.where(half, qt, zero), jnp.where(half, zero, qt)], axis=1)
        c1, c2, c3 = _bf16_pieces(jnp.full((hd, 2 * tq), slope, F32))
        s = float(KEY_SPLIT)
        qfeat = jnp.where(frow == 0, s * c1, jnp.where(frow == 1, s * c2, jnp.where(frow == 2, s * c3,
                jnp.where(frow == 3, c1, jnp.where(frow == 4, c2, jnp.where(frow == 5, c3, 0.0))))))
        heads.append((slope, jnp.concatenate([q2t, qfeat.astype(BF16)], axis=0)))

    m_scr[...] = jnp.full_like(m_scr, NEG_INF)
    acc_scr[...] = jnp.zeros_like(acc_scr)

    def scores(j, kb):
        k_blk = k_ref[pl.ds(pl.multiple_of(kb * tk, tk), tk), j * hd:(j + 1) * hd]
        return _dot(jnp.concatenate([k_blk, kfeat], axis=1), heads[j][1])

    def update(j, kb, t, shift):
        m_old = m_scr[j]
        m_new = jnp.maximum(m_old, jnp.max(t, axis=0, keepdims=True) + shift)
        alpha = jnp.exp2(m_old - m_new)
        p = jnp.exp2(t - (m_new - shift)).astype(BF16)
        vt_aug = jnp.concatenate([vt_ref[kb, j * hd:(j + 1) * hd, :], ones], axis=0)
        acc_scr[j] = alpha * acc_scr[j] + _dot(vt_aug, p)
        m_scr[j] = m_new

    def tile_shift(j, kb):
        return heads[j][0] * (kb * tk - qi * tq).astype(F32)

    def full_tile(kb, carry):
        ts = [scores(j, kb) for j in range(nh)]
        for j in range(nh):
            update(j, kb, ts[j], tile_shift(j, kb))
        return carry

    lax.fori_loop(0, n_full, full_tile, 0)
    ts = [scores(j, n_full) for j in range(nh)]
    for j in range(nh):
        update(j, n_full, jnp.where(visible, ts[j], NEG_INF), tile_shift(j, n_full))

    lam = (jnp.exp(jnp.sum(lq1_ref[...] * lk1_ref[...], keepdims=True))
           - jnp.exp(jnp.sum(lq2_ref[...] * lk2_ref[...], keepdims=True)) + lam_init)
    for j in range(nh):
        acc = acc_scr[j]
        o = acc[:hd] / acc[hd:hd + 1]
        ot = o[:, :tq] - lam * o[:, tq:]
        ot = ot * lax.rsqrt(jnp.mean(ot * ot, axis=0, keepdims=True) + EPS)
        ot = ot * hn_ref[...] * (1.0 - lam_init)
        o_ref[:, j * hd:(j + 1) * hd] = ot.T.astype(o_ref.dtype)


def _diff_attn(qt, k, vt, lq1, lk1, lq2, lk2, head_norm, batch, seq, layer_idx):
    tq, tk = Q_BLOCK, KEY_BLOCK
    nh = ATT_HEADS_PER_STEP
    m, dq = k.shape
    hd = 2 * DIFF_HEAD_DIM
    nq = seq // tq
    lam_init = 0.8 - 0.6 * math.exp(-0.3 * layer_idx)
    slopes = 2.0 ** (-8.0 * jnp.arange(1, DIFF_HEADS + 1, dtype=F32) / DIFF_HEADS)
    vec = lambda a: a.reshape(1, DIFF_HEAD_DIM).astype(F32)
    lam_spec = pl.BlockSpec((1, DIFF_HEAD_DIM), lambda b, h, i: (0, 0))
    return pl.pallas_call(
        functools.partial(_diff_attn_body, lam_init),
        grid=(batch, DIFF_HEADS // nh, nq),
        in_specs=[
            pl.BlockSpec(memory_space=pltpu.SMEM),
            pl.BlockSpec((1, nh * hd, tq), lambda b, h, i: (b * nq + i, h, 0)),
            pl.BlockSpec((seq, nh * hd), lambda b, h, i: (b, h)),
            pl.BlockSpec((seq // tk, nh * hd, tk), lambda b, h, i: (b, h, 0)),
            lam_spec, lam_spec, lam_spec, lam_spec,
            pl.BlockSpec((hd, 1), lambda b, h, i: (0, 0)),
        ],
        out_specs=pl.BlockSpec((tq, nh * hd), lambda b, h, i: (b * nq + i, h)),
        out_shape=jax.ShapeDtypeStruct((m, dq), BF16),
        scratch_shapes=[
            pltpu.VMEM((nh, 1, 2 * tq), F32),
            pltpu.VMEM((nh, hd + ONES_ROWS, 2 * tq), F32),
        ],
        compiler_params=_params("parallel", "parallel", "arbitrary"),
        name="diff_attn",
    )(slopes, qt, k, vt, vec(lq1), vec(lk1), vec(lq2), vec(lk2), head_norm.reshape(hd, 1).astype(F32))


def _sgu_body(x_ref, g_ref, win_ref, bin_ref, vn_ref, ws_ref, bs_ref, wout_ref, y_ref):
    tm = x_ref.shape[0]
    width = vn_ref.shape[1]
    gd = width // SGU_GROUPS
    c = SGU_CHUNK
    x = x_ref[...]
    h = _rms(x, g_ref[...]).astype(BF16)
    uv = _dot(h, win_ref[...]) + bin_ref[...]
    uv = 0.5 * uv * (1.0 + jnp.tanh(math.sqrt(2.0 / math.pi) * (uv + 0.044715 * (uv * uv * uv))))
    u = uv[:, :width]
    v = _rms(uv[:, width:], vn_ref[...]).astype(BF16)
    row = lax.broadcasted_iota(jnp.int32, (c, c), 0)
    col = lax.broadcasted_iota(jnp.int32, (c, c), 1)
    causal = col <= row
    cols = []
    for gi in range(SGU_GROUPS):
        w = jnp.where(causal, ws_ref[gi], 0.0).astype(BF16)
        bias = bs_ref[gi]
        rows = [_dot(w, v[n * c:(n + 1) * c, gi * gd:(gi + 1) * gd]) + bias for n in range(tm // c)]
        cols.append(jnp.concatenate(rows, axis=0))
    s = jnp.concatenate(cols, axis=1)
    y_ref[...] = x + _dot((u * s).astype(BF16), wout_ref[...])


def _sgu(x, gain, w_in, b_in, v_norm, w_s, b_s, w_out, tm=256):
    m, d = x.shape
    width = v_norm.shape[0]
    gd = width // SGU_GROUPS
    bs = jnp.broadcast_to(b_s[:, :, None], (SGU_GROUPS, SGU_CHUNK, gd)).astype(F32)
    fixed2 = lambda i: (0, 0)
    fixed3 = lambda i: (0, 0, 0)
    return pl.pallas_call(
        _sgu_body,
        grid=(m // tm,),
        in_specs=[
            pl.BlockSpec((tm, d), lambda i: (i, 0)),
            pl.BlockSpec((1, d), fixed2),
            pl.BlockSpec(w_in.shape, fixed2),
            pl.BlockSpec((1, 2 * width), fixed2),
            pl.BlockSpec((1, width), fixed2),
            pl.BlockSpec(w_s.shape, fixed3),
            pl.BlockSpec(bs.shape, fixed3),
            pl.BlockSpec(w_out.shape, fixed2),
        ],
        out_specs=pl.BlockSpec((tm, d), lambda i: (i, 0)),
        out_shape=jax.ShapeDtypeStruct((m, d), F32),
        compiler_params=_params("parallel"),
        name="sgu",
    )(x, gain.reshape(1, d), w_in.astype(BF16), b_in.reshape(1, 2 * width), v_norm.reshape(1, width),
      w_s, bs, w_out.astype(BF16))


def _gla_layer(x, batch, seq, norm1, w_in, gw1, gw2, gb, head_norm, w_out):
    q, k, v, og, la = _gla_inproj(x, norm1, w_in, gw1, gw2, gb)
    o = _gla_core(q, k, v, og, la, head_norm, batch, seq)
    return _outproj(o, w_out, x)


def _diff_layer(x, batch, seq, norm1, w_in, lq1, lk1, lq2, lk2, head_norm, w_out, layer_idx):
    qt, k, vt = _diff_inproj(x, norm1, w_in)
    o = _diff_attn(qt, k, vt, lq1, lk1, lq2, lk2, head_norm, batch, seq, layer_idx)
    return _outproj(o, w_out, x)


def kernel(x, l0_norm1, l0_w_in, l0_gate_w1, l0_gate_w2, l0_gate_b, l0_head_norm, l0_w_out, l0_norm2, l0_mlp_w1, l0_mlp_w2, l1_norm1, l1_w_in, l1_lambda_q1, l1_lambda_k1, l1_lambda_q2, l1_lambda_k2, l1_head_norm, l1_w_out, l1_norm2, l1_mlp_w1, l1_mlp_w2, l2_norm1, l2_w_in, l2_b_in, l2_v_norm, l2_w_s, l2_b_s, l2_w_out, l2_norm2, l2_mlp_w1, l2_mlp_w2, l3_norm1, l3_w_in, l3_gate_w1, l3_gate_w2, l3_gate_b, l3_head_norm, l3_w_out, l3_norm2, l3_mlp_w1, l3_mlp_w2, final_norm):
    batch, seq, d = x.shape
    h = x.reshape(batch * seq, d)
    h = _gla_layer(h, batch, seq, l0_norm1, l0_w_in, l0_gate_w1, l0_gate_w2, l0_gate_b, l0_head_norm, l0_w_out)
    h = _mlp(h, l0_norm2, l0_mlp_w1, l0_mlp_w2)
    h = _diff_layer(h, batch, seq, l1_norm1, l1_w_in, l1_lambda_q1, l1_lambda_k1, l1_lambda_q2, l1_lambda_k2,
                    l1_head_norm, l1_w_out, 1)
    h = _mlp(h, l1_norm2, l1_mlp_w1, l1_mlp_w2)
    h = _sgu(h, l2_norm1, l2_w_in, l2_b_in, l2_v_norm, l2_w_s, l2_b_s, l2_w_out)
    h = _mlp(h, l2_norm2, l2_mlp_w1, l2_mlp_w2)
    h = _gla_layer(h, batch, seq, l3_norm1, l3_w_in, l3_gate_w1, l3_gate_w2, l3_gate_b, l3_head_norm, l3_w_out)
    h = _mlp(h, l3_norm2, l3_mlp_w1, l3_mlp_w2, final_gain=final_norm)
    return h.reshape(batch, seq, d)
```

```python
import functools
import math

import jax
import jax.numpy as jnp
from jax import lax
from jax.experimental import pallas as pl
from jax.experimental.pallas import tpu as pltpu

F32 = jnp.float32
BF16 = jnp.bfloat16

EPS = 1e-6
NEG_INF = -1e30

GLA_HEADS = 4
GLA_RANK = 16
GLA_GATE_NORM = 16.0
GLA_CHUNK = 64

DIFF_HEAD_DIM = 64
DIFF_HEADS = 8

SGU_CHUNK = 128
SGU_GROUPS = 8

LANES = 128
LOG2E = 1.4426950408889634
VMEM_LIMIT = 52 * 1024 * 1024


def _params(*sem):
    return pltpu.CompilerParams(dimension_semantics=sem, vmem_limit_bytes=VMEM_LIMIT)


def _dot(a, b):
    return jnp.dot(a, b, preferred_element_type=F32)


def _dot_nt(a, b):
    return lax.dot_general(a, b, (((1,), (1,)), ((), ())), preferred_element_type=F32)


def _dot_tn(a, b):
    return lax.dot_general(a, b, (((0,), (0,)), ((), ())), preferred_element_type=F32)


def _rms(x, g):
    return x * lax.rsqrt(jnp.mean(x * x, axis=-1, keepdims=True) + EPS) * g


def _gla_inproj_body(x_ref, g_ref, w_ref, gw1_ref, gw2_ref, gb_ref, q_ref, k_ref, v_ref, og_ref, la_ref):
    dk = q_ref.shape[1]
    dv = v_ref.shape[1]
    h = _rms(x_ref[...], g_ref[...]).astype(BF16)
    q_ref[...] = _dot(h, w_ref[:, 0:dk])
    k_ref[...] = _dot(h, w_ref[:, dk:2 * dk])
    v_ref[...] = _dot(h, w_ref[:, 2 * dk:2 * dk + dv]).astype(BF16)
    og_ref[...] = _dot(h, w_ref[:, 2 * dk + dv:2 * dk + 2 * dv])
    t = _dot(h, gw1_ref[...]).astype(BF16)
    z = _dot(t, gw2_ref[...]) + gb_ref[...]
    la_ref[...] = (jnp.minimum(z, 0.0) - jnp.log(1.0 + jnp.exp(-jnp.abs(z)))) / GLA_GATE_NORM


def _gla_inproj(x, gain, w_in, gw1, gw2, gb, tm=512):
    m, d = x.shape
    dk = gw2.shape[1]
    dv = (w_in.shape[1] - 2 * dk) // 2
    gw1p = jnp.zeros((d, LANES), BF16).at[:, :GLA_RANK].set(gw1.astype(BF16))
    gw2p = jnp.zeros((LANES, dk), BF16).at[:GLA_RANK, :].set(gw2.astype(BF16))
    row = lambda i: (i, 0)
    fixed = lambda i: (0, 0)
    return pl.pallas_call(
        _gla_inproj_body,
        grid=(m // tm,),
        in_specs=[
            pl.BlockSpec((tm, d), row),
            pl.BlockSpec((1, d), fixed),
            pl.BlockSpec(w_in.shape, fixed),
            pl.BlockSpec(gw1p.shape, fixed),
            pl.BlockSpec(gw2p.shape, fixed),
            pl.BlockSpec((1, dk), fixed),
        ],
        out_specs=[
            pl.BlockSpec((tm, dk), row),
            pl.BlockSpec((tm, dk), row),
            pl.BlockSpec((tm, dv), row),
            pl.BlockSpec((tm, dv), row),
            pl.BlockSpec((tm, dk), row),
        ],
        out_shape=[
            jax.ShapeDtypeStruct((m, dk), F32),
            jax.ShapeDtypeStruct((m, dk), F32),
            jax.ShapeDtypeStruct((m, dv), BF16),
            jax.ShapeDtypeStruct((m, dv), F32),
            jax.ShapeDtypeStruct((m, dk), F32),
        ],
        compiler_params=_params("parallel"),
        name="gla_inproj",
    )(x, gain.reshape(1, d), w_in.astype(BF16), gw1p, gw2p, gb.reshape(1, dk))


def _gla_body(tri_ref, q_ref, k_ref, v_ref, og_ref, la_ref, hn_ref, o_ref, st_ref):
    c = GLA_CHUNK
    tc, hk = q_ref.shape
    n_chunks = tc // c

    @pl.when(pl.program_id(2) == 0)
    def _():
        st_ref[...] = jnp.zeros_like(st_ref)

    tri = tri_ref[...]
    tg = tri.shape[0]
    la = la_ref[...]
    la_hi = la.astype(BF16)
    la_lo = (la - la_hi.astype(F32)).astype(BF16)
    b = jnp.concatenate([_dot(tri, la_hi[r:r + tg]) + _dot(tri, la_lo[r:r + tg]) for r in range(0, tc, tg)],
                        axis=0)
    b_lasts = [b[(n + 1) * c - 1:(n + 1) * c, :] for n in range(n_chunks)]
    b_end = jnp.concatenate([jnp.broadcast_to(bl, (c, hk)) for bl in b_lasts], axis=0)

    q = q_ref[...] * (hk ** -0.5)
    k = k_ref[...]
    q_dec = (q * jnp.exp(b)).astype(BF16)
    k_inv = (k * jnp.exp(-b)).astype(BF16)
    k_end = (k * jnp.exp(b_end - b)).astype(BF16)

    row = lax.broadcasted_iota(jnp.int32, (c, c), 0)
    col = lax.broadcasted_iota(jnp.int32, (c, c), 1)
    causal = col <= row
    chunk = lambda a, n: a[n * c:(n + 1) * c]

    atts = [jnp.where(causal, _dot_nt(chunk(q_dec, n), chunk(k_inv, n)), 0.0).astype(BF16)
            for n in range(n_chunks)]
    kvs = [_dot_tn(v_ref[n * c:(n + 1) * c, :], chunk(k_end, n)) for n in range(n_chunks)]
    intra = [_dot(atts[n], v_ref[n * c:(n + 1) * c, :]) for n in range(n_chunks)]

    st = st_ref[...]
    outs = []
    for n in range(n_chunks):
        outs.append(intra[n] + _dot_nt(chunk(q_dec, n), st.astype(BF16)))
        st = jnp.exp(b_lasts[n]) * st + kvs[n]
    st_ref[...] = st

    o = jnp.concatenate(outs, axis=0)
    o = o * lax.rsqrt(jnp.mean(o * o, axis=-1, keepdims=True) + EPS) * hn_ref[...]
    g = og_ref[...]
    o_ref[...] = (o * (g / (1.0 + jnp.exp(-g)))).astype(o_ref.dtype)


GLA_CUMSUM_ROWS = 256


def _gla_core(q, k, v, og, la, head_norm, batch, seq, tc=2048):
    tc = min(tc, seq)
    m, dk = q.shape
    dv = v.shape[1]
    hk = dk // GLA_HEADS
    hv = dv // GLA_HEADS
    nt = seq // tc
    idx = jnp.arange(GLA_CUMSUM_ROWS)
    tri = ((idx[:, None] // GLA_CHUNK == idx[None, :] // GLA_CHUNK)
           & (idx[None, :] <= idx[:, None])).astype(BF16)
    blk = lambda b, h, t: (b * nt + t, h)
    return pl.pallas_call(
        _gla_body,
        grid=(batch, GLA_HEADS, nt),
        in_specs=[
            pl.BlockSpec(tri.shape, lambda b, h, t: (0, 0)),
            pl.BlockSpec((tc, hk), blk),
            pl.BlockSpec((tc, hk), blk),
            pl.BlockSpec((tc, hv), blk),
            pl.BlockSpec((tc, hv), blk),
            pl.BlockSpec((tc, hk), blk),
            pl.BlockSpec((1, hv), lambda b, h, t: (0, 0)),
        ],
        out_specs=pl.BlockSpec((tc, hv), blk),
        out_shape=jax.ShapeDtypeStruct((m, dv), BF16),
        scratch_shapes=[pltpu.VMEM((hv, hk), F32)],
        compiler_params=_params("parallel", "parallel", "arbitrary"),
        name="gla_core",
    )(tri, q, k, v, og, la, head_norm.reshape(1, hv))


def _mlp_body(final, x_ref, o_ref, wo_ref, g_ref, w1_ref, w2_ref, gf_ref, y_ref, x1_scr, h_scr, acc_scr):
    f = pl.program_id(1)

    @pl.when(f == 0)
    def _():
        x1 = x_ref[...] + _dot(o_ref[...], wo_ref[...])
        x1_scr[...] = x1
        h_scr[...] = _rms(x1, g_ref[...]).astype(BF16)
        acc_scr[...] = jnp.zeros_like(acc_scr)

    a = jnp.maximum(_dot(h_scr[...], w1_ref[...]), 0.0)
    acc_scr[...] += _dot((a * a).astype(BF16), w2_ref[...])

    @pl.when(f == pl.num_programs(1) - 1)
    def _():
        y = x1_scr[...] + acc_scr[...]
        if final:
            y = _rms(y, gf_ref[...])
        y_ref[...] = y


def _mlp(x, o, w_out, gain, w1, w2, final_gain=None, tm=1024, tf=512):
    m, d = x.shape
    kdim = o.shape[1]
    dff = w1.shape[1]
    final = final_gain is not None
    gf = (final_gain if final else gain).reshape(1, d)
    return pl.pallas_call(
        functools.partial(_mlp_body, final),
        grid=(m // tm, dff // tf),
        in_specs=[
            pl.BlockSpec((tm, d), lambda i, f: (i, 0)),
            pl.BlockSpec((tm, kdim), lambda i, f: (i, 0)),
            pl.BlockSpec((kdim, d), lambda i, f: (0, 0)),
            pl.BlockSpec((1, d), lambda i, f: (0, 0)),
            pl.BlockSpec((d, tf), lambda i, f: (0, f)),
            pl.BlockSpec((tf, d), lambda i, f: (f, 0)),
            pl.BlockSpec((1, d), lambda i, f: (0, 0)),
        ],
        out_specs=pl.BlockSpec((tm, d), lambda i, f: (i, 0)),
        out_shape=jax.ShapeDtypeStruct((m, d), F32),
        scratch_shapes=[pltpu.VMEM((tm, d), F32), pltpu.VMEM((tm, d), BF16), pltpu.VMEM((tm, d), F32)],
        compiler_params=_params("parallel", "arbitrary"),
        name="mlp",
    )(x, o, w_out.astype(BF16), gain.reshape(1, d), w1.astype(BF16), w2.astype(BF16), gf)


Q_BLOCK = 256
KEY_BLOCK = 512


def _diff_inproj_body(x_ref, g_ref, wqt_ref, wk_ref, wvt_ref, qt_ref, k_ref, vt_ref):
    tq = qt_ref.shape[2]
    tk = vt_ref.shape[2]
    h = _rms(x_ref[...], g_ref[...]).astype(BF16)
    k_ref[...] = _dot(h, wk_ref[...]).astype(BF16)
    qt = _dot_nt(wqt_ref[...], h) * (LOG2E * DIFF_HEAD_DIM ** -0.5)
    vt = _dot_nt(wvt_ref[...], h)
    for j in range(qt_ref.shape[0]):
        qt_ref[j] = qt[:, j * tq:(j + 1) * tq].astype(BF16)
    for j in range(vt_ref.shape[0]):
        vt_ref[j] = vt[:, j * tk:(j + 1) * tk].astype(BF16)


def _diff_inproj(x, gain, w_in, tm=512):
    m, d = x.shape
    dq = 2 * DIFF_HEADS * DIFF_HEAD_DIM
    wqt = w_in[:, :dq].T.astype(BF16)
    wk = w_in[:, dq:2 * dq].astype(BF16)
    wvt = w_in[:, 2 * dq:].T.astype(BF16)
    dv = wvt.shape[0]
    tq, tk = Q_BLOCK, KEY_BLOCK
    fixed = lambda i: (0, 0)
    return pl.pallas_call(
        _diff_inproj_body,
        grid=(m // tm,),
        in_specs=[
            pl.BlockSpec((tm, d), lambda i: (i, 0)),
            pl.BlockSpec((1, d), fixed),
            pl.BlockSpec(wqt.shape, fixed),
            pl.BlockSpec(wk.shape, fixed),
            pl.BlockSpec(wvt.shape, fixed),
        ],
        out_specs=[
            pl.BlockSpec((tm // tq, dq, tq), lambda i: (i, 0, 0)),
            pl.BlockSpec((tm, dq), lambda i: (i, 0)),
            pl.BlockSpec((tm // tk, dv, tk), lambda i: (i, 0, 0)),
        ],
        out_shape=[
            jax.ShapeDtypeStruct((m // tq, dq, tq), BF16),
            jax.ShapeDtypeStruct((m, dq), BF16),
            jax.ShapeDtypeStruct((m // tk, dv, tk), BF16),
        ],
        compiler_params=_params("parallel"),
        name="diff_inproj",
    )(x, gain.reshape(1, d), wqt, wk, wvt)


ATT_HEADS_PER_STEP = 4
KEY_SPLIT = 32
ONES_ROWS = 16


def _bf16_pieces(c):
    c1 = c.astype(BF16).astype(F32)
    c2 = (c - c1).astype(BF16).astype(F32)
    c3 = (c - c1 - c2).astype(BF16).astype(F32)
    return c1, c2, c3


def _diff_attn_body(lam_init, slopes_ref, qt_ref, k_ref, vt_ref, lq1_ref, lk1_ref, lq2_ref, lk2_ref, hn_ref,
                    o_ref, m_scr, acc_scr):
    tq, tk = Q_BLOCK, KEY_BLOCK
    hd = 2 * DIFF_HEAD_DIM
    nh = ATT_HEADS_PER_STEP
    qi = pl.program_id(2)
    hg = pl.program_id(1)
    n_full = qi // (tk // tq)
    odd = qi % (tk // tq)

    key = lax.broadcasted_iota(jnp.int32, (tk, 2 * tq), 0)
    qry = lax.broadcasted_iota(jnp.int32, (tk, 2 * tq), 1) % tq
    visible = (key - qry) <= odd * tq

    kj = lax.broadcasted_iota(jnp.int32, (tk, LANES), 0)
    kl = lax.broadcasted_iota(jnp.int32, (tk, LANES), 1)
    kfeat = jnp.where(kl < 3, kj // KEY_SPLIT, jnp.where(kl < 6, kj % KEY_SPLIT, 0)).astype(F32).astype(BF16)
    ones = jnp.ones((ONES_ROWS, tk), BF16)
    half = lax.broadcasted_iota(jnp.int32, (hd, tq), 0) < DIFF_HEAD_DIM
    frow = lax.broadcasted_iota(jnp.int32, (hd, 2 * tq), 0)

    heads = []
    for j in range(nh):
        slope = slopes_ref[hg * nh + j] * LOG2E
        qt = qt_ref[0, j * hd:(j + 1) * hd, :]
        zero = jnp.zeros_like(qt)
        q2t = jnp.concatenate([jnp.where(half, qt, zero), jnp.where(half, zero, qt)], axis=1)
        c1, c2, c3 = _bf16_pieces(jnp.full((hd, 2 * tq), slope, F32))
        s = float(KEY_SPLIT)
        qfeat = jnp.where(frow == 0, s * c1, jnp.where(frow == 1, s * c2, jnp.where(frow == 2, s * c3,
                jnp.where(frow == 3, c1, jnp.where(frow == 4, c2, jnp.where(frow == 5, c3, 0.0))))))
        heads.append((slope, jnp.concatenate([q2t, qfeat.astype(BF16)], axis=0)))

    m_scr[...] = jnp.full_like(m_scr, NEG_INF)
    acc_scr[...] = jnp.zeros_like(acc_scr)

    def scores(j, kb):
        k_blk = k_ref[pl.ds(pl.multiple_of(kb * tk, tk), tk), j * hd:(j + 1) * hd]
        return _dot(jnp.concatenate([k_blk, kfeat], axis=1), heads[j][1])

    def update(j, kb, t, shift):
        m_old = m_scr[j]
        m_new = jnp.maximum(m_old, jnp.max(t, axis=0, keepdims=True) + shift)
        alpha = jnp.exp2(m_old - m_new)
        p = jnp.exp2(t - (m_new - shift)).astype(BF16)
        vt_aug = jnp.concatenate([vt_ref[kb, j * hd:(j + 1) * hd, :], ones], axis=0)
        acc_scr[j] = alpha * acc_scr[j] + _dot(vt_aug, p)
        m_scr[j] = m_new

    def tile_shift(j, kb):
        return heads[j][0] * (kb * tk - qi * tq).astype(F32)

    def full_tile(kb, carry):
        ts = [scores(j, kb) for j in range(nh)]
        for j in range(nh):
            update(j, kb, ts[j], tile_shift(j, kb))
        return carry

    lax.fori_loop(0, n_full, full_tile, 0)
    ts = [scores(j, n_full) for j in range(nh)]
    for j in range(nh):
        update(j, n_full, jnp.where(visible, ts[j], NEG_INF), tile_shift(j, n_full))

    lam = (jnp.exp(jnp.sum(lq1_ref[...] * lk1_ref[...], keepdims=True))
           - jnp.exp(jnp.sum(lq2_ref[...] * lk2_ref[...], keepdims=True)) + lam_init)
    for j in range(nh):
        acc = acc_scr[j]
        o = acc[:hd] / acc[hd:hd + 1]
        ot = o[:, :tq] - lam * o[:, tq:]
        ot = ot * lax.rsqrt(jnp.mean(ot * ot, axis=0, keepdims=True) + EPS)
        ot = ot * hn_ref[...] * (1.0 - lam_init)
        o_ref[:, j * hd:(j + 1) * hd] = ot.T.astype(o_ref.dtype)


def _diff_attn(qt, k, vt, lq1, lk1, lq2, lk2, head_norm, batch, seq, layer_idx):
    tq, tk = Q_BLOCK, KEY_BLOCK
    nh = ATT_HEADS_PER_STEP
    m, dq = k.shape
    hd = 2 * DIFF_HEAD_DIM
    nq = seq // tq
    lam_init = 0.8 - 0.6 * math.exp(-0.3 * layer_idx)
    slopes = 2.0 ** (-8.0 * jnp.arange(1, DIFF_HEADS + 1, dtype=F32) / DIFF_HEADS)
    vec = lambda a: a.reshape(1, DIFF_HEAD_DIM).astype(F32)
    lam_spec = pl.BlockSpec((1, DIFF_HEAD_DIM), lambda b, h, i: (0, 0))
    return pl.pallas_call(
        functools.partial(_diff_attn_body, lam_init),
        grid=(batch, DIFF_HEADS // nh, nq),
        in_specs=[
            pl.BlockSpec(memory_space=pltpu.SMEM),
            pl.BlockSpec((1, nh * hd, tq), lambda b, h, i: (b * nq + i, h, 0)),
            pl.BlockSpec((seq, nh * hd), lambda b, h, i: (b, h)),
            pl.BlockSpec((seq // tk, nh * hd, tk), lambda b, h, i: (b, h, 0)),
            lam_spec, lam_spec, lam_spec, lam_spec,
            pl.BlockSpec((hd, 1), lambda b, h, i: (0, 0)),
        ],
        out_specs=pl.BlockSpec((tq, nh * hd), lambda b, h, i: (b * nq + i, h)),
        out_shape=jax.ShapeDtypeStruct((m, dq), BF16),
        scratch_shapes=[
            pltpu.VMEM((nh, 1, 2 * tq), F32),
            pltpu.VMEM((nh, hd + ONES_ROWS, 2 * tq), F32),
        ],
        compiler_params=_params("parallel", "parallel", "arbitrary"),
        name="diff_attn",
    )(slopes, qt, k, vt, vec(lq1), vec(lk1), vec(lq2), vec(lk2), head_norm.reshape(hd, 1).astype(F32))


def _sgu_body(x_ref, g_ref, win_ref, bin_ref, vn_ref, ws_ref, bs_ref, y_ref):
    tm = x_ref.shape[0]
    width = vn_ref.shape[1]
    gd = width // SGU_GROUPS
    c = SGU_CHUNK
    h = _rms(x_ref[...], g_ref[...]).astype(BF16)
    uv = _dot(h, win_ref[...]) + bin_ref[...]
    uv = 0.5 * uv * (1.0 + jnp.tanh(math.sqrt(2.0 / math.pi) * (uv + 0.044715 * (uv * uv * uv))))
    u = uv[:, :width]
    v = _rms(uv[:, width:], vn_ref[...]).astype(BF16)
    row = lax.broadcasted_iota(jnp.int32, (c, c), 0)
    col = lax.broadcasted_iota(jnp.int32, (c, c), 1)
    causal = col <= row
    nc = tm // c
    cols = []
    for gi in range(SGU_GROUPS):
        w = jnp.where(causal, ws_ref[gi], 0.0).astype(BF16)
        vg = jnp.concatenate([v[n * c:(n + 1) * c, gi * gd:(gi + 1) * gd] for n in range(nc)], axis=1)
        sg = _dot(w, vg)
        bias = bs_ref[gi]
        cols.append(jnp.concatenate([sg[:, n * gd:(n + 1) * gd] + bias for n in range(nc)], axis=0))
    s = jnp.concatenate(cols, axis=1)
    y_ref[...] = (u * s).astype(y_ref.dtype)


def _sgu(x, gain, w_in, b_in, v_norm, w_s, b_s, tm=256):
    m, d = x.shape
    width = v_norm.shape[0]
    gd = width // SGU_GROUPS
    bs = jnp.broadcast_to(b_s[:, :, None], (SGU_GROUPS, SGU_CHUNK, gd)).astype(F32)
    fixed2 = lambda i: (0, 0)
    fixed3 = lambda i: (0, 0, 0)
    return pl.pallas_call(
        _sgu_body,
        grid=(m // tm,),
        in_specs=[
            pl.BlockSpec((tm, d), lambda i: (i, 0)),
            pl.BlockSpec((1, d), fixed2),
            pl.BlockSpec(w_in.shape, fixed2),
            pl.BlockSpec((1, 2 * width), fixed2),
            pl.BlockSpec((1, width), fixed2),
            pl.BlockSpec(w_s.shape, fixed3),
            pl.BlockSpec(bs.shape, fixed3),
        ],
        out_specs=pl.BlockSpec((tm, width), lambda i: (i, 0)),
        out_shape=jax.ShapeDtypeStruct((m, width), BF16),
        compiler_params=_params("parallel"),
        name="sgu",
    )(x, gain.reshape(1, d), w_in.astype(BF16), b_in.reshape(1, 2 * width), v_norm.reshape(1, width),
      w_s, bs)


def _gla_mixer(x, batch, seq, norm1, w_in, gw1, gw2, gb, head_norm):
    q, k, v, og, la = _gla_inproj(x, norm1, w_in, gw1, gw2, gb)
    return _gla_core(q, k, v, og, la, head_norm, batch, seq)


def _diff_mixer(x, batch, seq, norm1, w_in, lq1, lk1, lq2, lk2, head_norm, layer_idx):
    qt, k, vt = _diff_inproj(x, norm1, w_in)
    return _diff_attn(qt, k, vt, lq1, lk1, lq2, lk2, head_norm, batch, seq, layer_idx)


def kernel(x, l0_norm1, l0_w_in, l0_gate_w1, l0_gate_w2, l0_gate_b, l0_head_norm, l0_w_out, l0_norm2, l0_mlp_w1, l0_mlp_w2, l1_norm1, l1_w_in, l1_lambda_q1, l1_lambda_k1, l1_lambda_q2, l1_lambda_k2, l1_head_norm, l1_w_out, l1_norm2, l1_mlp_w1, l1_mlp_w2, l2_norm1, l2_w_in, l2_b_in, l2_v_norm, l2_w_s, l2_b_s, l2_w_out, l2_norm2, l2_mlp_w1, l2_mlp_w2, l3_norm1, l3_w_in, l3_gate_w1, l3_gate_w2, l3_gate_b, l3_head_norm, l3_w_out, l3_norm2, l3_mlp_w1, l3_mlp_w2, final_norm):
    batch, seq, d = x.shape
    h = x.reshape(batch * seq, d)
    o = _gla_mixer(h, batch, seq, l0_norm1, l0_w_in, l0_gate_w1, l0_gate_w2, l0_gate_b, l0_head_norm)
    h = _mlp(h, o, l0_w_out, l0_norm2, l0_mlp_w1, l0_mlp_w2)
    o = _diff_mixer(h, batch, seq, l1_norm1, l1_w_in, l1_lambda_q1, l1_lambda_k1, l1_lambda_q2, l1_lambda_k2,
                    l1_head_norm, 1)
    h = _mlp(h, o, l1_w_out, l1_norm2, l1_mlp_w1, l1_mlp_w2)
    o = _sgu(h, l2_norm1, l2_w_in, l2_b_in, l2_v_norm, l2_w_s, l2_b_s)
    h = _mlp(h, o, l2_w_out, l2_norm2, l2_mlp_w1, l2_mlp_w2)
    o = _gla_mixer(h, batch, seq, l3_norm1, l3_w_in, l3_gate_w1, l3_gate_w2, l3_gate_b, l3_head_norm)
    h = _mlp(h, o, l3_w_out, l3_norm2, l3_mlp_w1, l3_mlp_w2, final_gain=final_norm)
    return h.reshape(batch, seq, d)
```

```python
import functools
import math

import jax
import jax.numpy as jnp
from jax import lax
from jax.experimental import pallas as pl
from jax.experimental.pallas import tpu as pltpu

F32 = jnp.float32
BF16 = jnp.bfloat16

EPS = 1e-6
NEG_INF = -1e30

GLA_HEADS = 4
GLA_RANK = 16
GLA_GATE_NORM = 16.0
GLA_CHUNK = 64

DIFF_HEAD_DIM = 64
DIFF_HEADS = 8

SGU_CHUNK = 128
SGU_GROUPS = 8

LANES = 128
LOG2E = 1.4426950408889634
VMEM_LIMIT = 52 * 1024 * 1024


def _params(*sem):
    return pltpu.CompilerParams(dimension_semantics=sem, vmem_limit_bytes=VMEM_LIMIT)


def _dot(a, b):
    return jnp.dot(a, b, preferred_element_type=F32)


def _dot_nt(a, b):
    return lax.dot_general(a, b, (((1,), (1,)), ((), ())), preferred_element_type=F32)


def _dot_tn(a, b):
    return lax.dot_general(a, b, (((0,), (0,)), ((), ())), preferred_element_type=F32)


def _rms(x, g):
    return x * lax.rsqrt(jnp.mean(x * x, axis=-1, keepdims=True) + EPS) * g


def _gla_inproj_body(x_ref, g_ref, w_ref, gw1_ref, gw2_ref, gb_ref, q_ref, k_ref, v_ref, og_ref, la_ref):
    dk = q_ref.shape[1]
    dv = v_ref.shape[1]
    h = _rms(x_ref[...], g_ref[...]).astype(BF16)
    q_ref[...] = _dot(h, w_ref[:, 0:dk])
    k_ref[...] = _dot(h, w_ref[:, dk:2 * dk])
    v_ref[...] = _dot(h, w_ref[:, 2 * dk:2 * dk + dv]).astype(BF16)
    og_ref[...] = _dot(h, w_ref[:, 2 * dk + dv:2 * dk + 2 * dv])
    t = _dot(h, gw1_ref[...]).astype(BF16)
    z = _dot(t, gw2_ref[...]) + gb_ref[...]
    la_ref[...] = (jnp.minimum(z, 0.0) - jnp.log(1.0 + jnp.exp(-jnp.abs(z)))) / GLA_GATE_NORM


def _gla_inproj(x, gain, w_in, gw1, gw2, gb, tm=512):
    m, d = x.shape
    dk = gw2.shape[1]
    dv = (w_in.shape[1] - 2 * dk) // 2
    gw1p = jnp.zeros((d, LANES), BF16).at[:, :GLA_RANK].set(gw1.astype(BF16))
    gw2p = jnp.zeros((LANES, dk), BF16).at[:GLA_RANK, :].set(gw2.astype(BF16))
    row = lambda i: (i, 0)
    fixed = lambda i: (0, 0)
    return pl.pallas_call(
        _gla_inproj_body,
        grid=(m // tm,),
        in_specs=[
            pl.BlockSpec((tm, d), row),
            pl.BlockSpec((1, d), fixed),
            pl.BlockSpec(w_in.shape, fixed),
            pl.BlockSpec(gw1p.shape, fixed),
            pl.BlockSpec(gw2p.shape, fixed),
            pl.BlockSpec((1, dk), fixed),
        ],
        out_specs=[
            pl.BlockSpec((tm, dk), row),
            pl.BlockSpec((tm, dk), row),
            pl.BlockSpec((tm, dv), row),
            pl.BlockSpec((tm, dv), row),
            pl.BlockSpec((tm, dk), row),
        ],
        out_shape=[
            jax.ShapeDtypeStruct((m, dk), F32),
            jax.ShapeDtypeStruct((m, dk), F32),
            jax.ShapeDtypeStruct((m, dv), BF16),
            jax.ShapeDtypeStruct((m, dv), F32),
            jax.ShapeDtypeStruct((m, dk), F32),
        ],
        compiler_params=_params("parallel"),
        name="gla_inproj",
    )(x, gain.reshape(1, d), w_in.astype(BF16), gw1p, gw2p, gb.reshape(1, dk))


def _gla_body(tri_ref, q_ref, k_ref, v_ref, og_ref, la_ref, hn_ref, o_ref, st_ref):
    c = GLA_CHUNK
    tc, hk = q_ref.shape
    n_chunks = tc // c

    @pl.when(pl.program_id(2) == 0)
    def _():
        st_ref[...] = jnp.zeros_like(st_ref)

    tri = tri_ref[...]
    tg = tri.shape[0]
    la = la_ref[...]
    la_hi = la.astype(BF16)
    la_lo = (la - la_hi.astype(F32)).astype(BF16)
    b = jnp.concatenate([_dot(tri, la_hi[r:r + tg]) + _dot(tri, la_lo[r:r + tg]) for r in range(0, tc, tg)],
                        axis=0)
    b_lasts = [b[(n + 1) * c - 1:(n + 1) * c, :] for n in range(n_chunks)]
    b_end = jnp.concatenate([jnp.broadcast_to(bl, (c, hk)) for bl in b_lasts], axis=0)

    q = q_ref[...] * (hk ** -0.5)
    k = k_ref[...]
    q_dec = (q * jnp.exp(b)).astype(BF16)
    k_inv = (k * jnp.exp(-b)).astype(BF16)
    k_end = (k * jnp.exp(b_end - b)).astype(BF16)

    row = lax.broadcasted_iota(jnp.int32, (c, c), 0)
    col = lax.broadcasted_iota(jnp.int32, (c, c), 1)
    causal = col <= row
    chunk = lambda a, n: a[n * c:(n + 1) * c]

    atts = [jnp.where(causal, _dot_nt(chunk(q_dec, n), chunk(k_inv, n)), 0.0).astype(BF16)
            for n in range(n_chunks)]
    kvs = [_dot_tn(v_ref[n * c:(n + 1) * c, :], chunk(k_end, n)) for n in range(n_chunks)]
    intra = [_dot(atts[n], v_ref[n * c:(n + 1) * c, :]) for n in range(n_chunks)]

    st = st_ref[...]
    outs = []
    for n in range(n_chunks):
        outs.append(intra[n] + _dot_nt(chunk(q_dec, n), st.astype(BF16)))
        st = jnp.exp(b_lasts[n]) * st + kvs[n]
    st_ref[...] = st

    o = jnp.concatenate(outs, axis=0)
    o = o * lax.rsqrt(jnp.mean(o * o, axis=-1, keepdims=True) + EPS) * hn_ref[...]
    g = og_ref[...]
    o_ref[...] = (o * (g / (1.0 + jnp.exp(-g)))).astype(o_ref.dtype)


GLA_CUMSUM_ROWS = 256


def _gla_core(q, k, v, og, la, head_norm, batch, seq, tc=2048):
    tc = min(tc, seq)
    m, dk = q.shape
    dv = v.shape[1]
    hk = dk // GLA_HEADS
    hv = dv // GLA_HEADS
    nt = seq // tc
    idx = jnp.arange(GLA_CUMSUM_ROWS)
    tri = ((idx[:, None] // GLA_CHUNK == idx[None, :] // GLA_CHUNK)
           & (idx[None, :] <= idx[:, None])).astype(BF16)
    blk = lambda b, h, t: (b * nt + t, h)
    return pl.pallas_call(
        _gla_body,
        grid=(batch, GLA_HEADS, nt),
        in_specs=[
            pl.BlockSpec(tri.shape, lambda b, h, t: (0, 0)),
            pl.BlockSpec((tc, hk), blk),
            pl.BlockSpec((tc, hk), blk),
            pl.BlockSpec((tc, hv), blk),
            pl.BlockSpec((tc, hv), blk),
            pl.BlockSpec((tc, hk), blk),
            pl.BlockSpec((1, hv), lambda b, h, t: (0, 0)),
        ],
        out_specs=pl.BlockSpec((tc, hv), blk),
        out_shape=jax.ShapeDtypeStruct((m, dv), BF16),
        scratch_shapes=[pltpu.VMEM((hv, hk), F32)],
        compiler_params=_params("parallel", "parallel", "arbitrary"),
        name="gla_core",
    )(tri, q, k, v, og, la, head_norm.reshape(1, hv))


def _mlp_body(final, x_ref, o_ref, wo_ref, g_ref, w1_ref, w2_ref, gf_ref, y_ref, x1_scr, h_scr, acc_scr):
    f = pl.program_id(1)

    @pl.when(f == 0)
    def _():
        x1 = x_ref[...] + _dot(o_ref[...], wo_ref[...])
        x1_scr[...] = x1
        h_scr[...] = _rms(x1, g_ref[...]).astype(BF16)
        acc_scr[...] = jnp.zeros_like(acc_scr)

    a = jnp.maximum(_dot(h_scr[...], w1_ref[...]), 0.0)
    acc_scr[...] += _dot((a * a).astype(BF16), w2_ref[...])

    @pl.when(f == pl.num_programs(1) - 1)
    def _():
        y = x1_scr[...] + acc_scr[...]
        if final:
            y = _rms(y, gf_ref[...])
        y_ref[...] = y


def _mlp(x, o, w_out, gain, w1, w2, final_gain=None, tm=1024, tf=512):
    m, d = x.shape
    kdim = o.shape[1]
    dff = w1.shape[1]
    final = final_gain is not None
    gf = (final_gain if final else gain).reshape(1, d)
    return pl.pallas_call(
        functools.partial(_mlp_body, final),
        grid=(m // tm, dff // tf),
        in_specs=[
            pl.BlockSpec((tm, d), lambda i, f: (i, 0)),
            pl.BlockSpec((tm, kdim), lambda i, f: (i, 0)),
            pl.BlockSpec((kdim, d), lambda i, f: (0, 0)),
            pl.BlockSpec((1, d), lambda i, f: (0, 0)),
            pl.BlockSpec((d, tf), lambda i, f: (0, f)),
            pl.BlockSpec((tf, d), lambda i, f: (f, 0)),
            pl.BlockSpec((1, d), lambda i, f: (0, 0)),
        ],
        out_specs=pl.BlockSpec((tm, d), lambda i, f: (i, 0)),
        out_shape=jax.ShapeDtypeStruct((m, d), F32),
        scratch_shapes=[pltpu.VMEM((tm, d), F32), pltpu.VMEM((tm, d), BF16), pltpu.VMEM((tm, d), F32)],
        compiler_params=_params("parallel", "arbitrary"),
        name="mlp",
    )(x, o, w_out.astype(BF16), gain.reshape(1, d), w1.astype(BF16), w2.astype(BF16), gf)


Q_BLOCK = 256
KEY_BLOCK = 512


def _diff_inproj_body(x_ref, g_ref, wqt_ref, wk_ref, wvt_ref, qt_ref, k_ref, vt_ref):
    tq = qt_ref.shape[2]
    tk = vt_ref.shape[2]
    h = _rms(x_ref[...], g_ref[...]).astype(BF16)
    k_ref[...] = _dot(h, wk_ref[...]).astype(BF16)
    qt = _dot_nt(wqt_ref[...], h) * (LOG2E * DIFF_HEAD_DIM ** -0.5)
    vt = _dot_nt(wvt_ref[...], h)
    for j in range(qt_ref.shape[0]):
        qt_ref[j] = qt[:, j * tq:(j + 1) * tq].astype(BF16)
    for j in range(vt_ref.shape[0]):
        vt_ref[j] = vt[:, j * tk:(j + 1) * tk].astype(BF16)


def _diff_inproj(x, gain, w_in, tm=512):
    m, d = x.shape
    dq = 2 * DIFF_HEADS * DIFF_HEAD_DIM
    wqt = w_in[:, :dq].T.astype(BF16)
    wk = w_in[:, dq:2 * dq].astype(BF16)
    wvt = w_in[:, 2 * dq:].T.astype(BF16)
    dv = wvt.shape[0]
    tq, tk = Q_BLOCK, KEY_BLOCK
    fixed = lambda i: (0, 0)
    return pl.pallas_call(
        _diff_inproj_body,
        grid=(m // tm,),
        in_specs=[
            pl.BlockSpec((tm, d), lambda i: (i, 0)),
            pl.BlockSpec((1, d), fixed),
            pl.BlockSpec(wqt.shape, fixed),
            pl.BlockSpec(wk.shape, fixed),
            pl.BlockSpec(wvt.shape, fixed),
        ],
        out_specs=[
            pl.BlockSpec((tm // tq, dq, tq), lambda i: (i, 0, 0)),
            pl.BlockSpec((tm, dq), lambda i: (i, 0)),
            pl.BlockSpec((tm // tk, dv, tk), lambda i: (i, 0, 0)),
        ],
        out_shape=[
            jax.ShapeDtypeStruct((m // tq, dq, tq), BF16),
            jax.ShapeDtypeStruct((m, dq), BF16),
            jax.ShapeDtypeStruct((m // tk, dv, tk), BF16),
        ],
        compiler_params=_params("parallel"),
        name="diff_inproj",
    )(x, gain.reshape(1, d), wqt, wk, wvt)


ATT_HEADS_PER_STEP = 4
KEY_SPLIT = 32
ONES_ROWS = 16
FEAT_ROWS = 8


def _bf16_pieces(c):
    c1 = c.astype(BF16).astype(F32)
    c2 = (c - c1).astype(BF16).astype(F32)
    c3 = (c - c1 - c2).astype(BF16).astype(F32)
    return c1, c2, c3


def _diff_attn_body(lam_init, slopes_ref, qt_ref, k_ref, vt_ref, lq1_ref, lk1_ref, lq2_ref, lk2_ref, hn_ref,
                    o_ref, m_scr, acc_scr, s_scr, mx_scr):
    tq, tk = Q_BLOCK, KEY_BLOCK
    hd = 2 * DIFF_HEAD_DIM
    nh = ATT_HEADS_PER_STEP
    qi = pl.program_id(2)
    hg = pl.program_id(1)
    n_full = qi // (tk // tq)
    odd = qi % (tk // tq)

    def mask(t):
        key = lax.broadcasted_iota(jnp.int32, (tk, 2 * tq), 0)
        qry = lax.broadcasted_iota(jnp.int32, (tk, 2 * tq), 1) % tq
        return jnp.where((key - qry) <= odd * tq, t, NEG_INF)

    kj = lax.broadcasted_iota(jnp.int32, (tk, LANES), 0)
    kl = lax.broadcasted_iota(jnp.int32, (tk, LANES), 1)
    kfeat = jnp.where(kl < 3, kj // KEY_SPLIT, jnp.where(kl < 6, kj % KEY_SPLIT, 0)).astype(F32).astype(BF16)
    ones = jnp.ones((ONES_ROWS, tk), BF16)
    half = lax.broadcasted_iota(jnp.int32, (hd, tq), 0) < DIFF_HEAD_DIM
    frow = lax.broadcasted_iota(jnp.int32, (FEAT_ROWS, 2 * tq), 0)

    heads = []
    for j in range(nh):
        slope = slopes_ref[hg * nh + j] * LOG2E
        qt = qt_ref[0, j * hd:(j + 1) * hd, :]
        zero = jnp.zeros_like(qt)
        q2t = jnp.concatenate([jnp.where(half, qt, zero), jnp.where(half, zero, qt)], axis=1)
        c1, c2, c3 = _bf16_pieces(jnp.full(frow.shape, slope, F32))
        s = float(KEY_SPLIT)
        qfeat = jnp.where(frow == 0, s * c1, jnp.where(frow == 1, s * c2, jnp.where(frow == 2, s * c3,
                jnp.where(frow == 3, c1, jnp.where(frow == 4, c2, jnp.where(frow == 5, c3, 0.0))))))
        qfeat = jnp.concatenate([qfeat, jnp.zeros((hd - FEAT_ROWS, 2 * tq), F32)], axis=0).astype(BF16)
        heads.append((slope, jnp.concatenate([q2t, qfeat], axis=0)))

    m_scr[...] = jnp.full_like(m_scr, NEG_INF)
    acc_scr[...] = jnp.zeros_like(acc_scr)

    def scores(j, kb):
        k_blk = k_ref[pl.ds(pl.multiple_of(kb * tk, tk), tk), j * hd:(j + 1) * hd]
        return _dot(jnp.concatenate([k_blk, kfeat], axis=1), heads[j][1])

    def stage(j, t):
        s_scr[j] = t
        mx_scr[j] = jnp.max(t, axis=0, keepdims=True)

    def consume(j, kb):
        shift = heads[j][0] * (kb * tk - qi * tq).astype(F32)
        m_old = m_scr[j]
        m_new = jnp.maximum(m_old, mx_scr[j] + shift)
        alpha = jnp.exp2(m_old - m_new)
        p = jnp.exp2(s_scr[j] - (m_new - shift)).astype(BF16)
        vt_aug = jnp.concatenate([vt_ref[kb, j * hd:(j + 1) * hd, :], ones], axis=0)
        acc_scr[j] = alpha * acc_scr[j] + _dot(vt_aug, p)
        m_scr[j] = m_new

    def pipelined(kb, masked):
        for j in range(nh):
            t_next = scores(j, kb + 1)
            if masked:
                t_next = mask(t_next)
            consume(j, kb)
            stage(j, t_next)

    @pl.when(n_full == 0)
    def _():
        for j in range(nh):
            stage(j, mask(scores(j, 0)))

    @pl.when(n_full > 0)
    def _():
        for j in range(nh):
            stage(j, scores(j, 0))

    def full_tile(kb, carry):
        pipelined(kb, False)
        return carry

    lax.fori_loop(0, n_full - 1, full_tile, 0)

    @pl.when(n_full > 0)
    def _():
        pipelined(n_full - 1, True)

    for j in range(nh):
        consume(j, n_full)

    lam = (jnp.exp(jnp.sum(lq1_ref[...] * lk1_ref[...], keepdims=True))
           - jnp.exp(jnp.sum(lq2_ref[...] * lk2_ref[...], keepdims=True)) + lam_init)
    for j in range(nh):
        acc = acc_scr[j]
        o = acc[:hd] / acc[hd:hd + 1]
        ot = o[:, :tq] - lam * o[:, tq:]
        ot = ot * lax.rsqrt(jnp.mean(ot * ot, axis=0, keepdims=True) + EPS)
        ot = ot * hn_ref[...] * (1.0 - lam_init)
        o_ref[:, j * hd:(j + 1) * hd] = ot.T.astype(o_ref.dtype)


def _diff_attn(qt, k, vt, lq1, lk1, lq2, lk2, head_norm, batch, seq, layer_idx):
    tq, tk = Q_BLOCK, KEY_BLOCK
    nh = ATT_HEADS_PER_STEP
    m, dq = k.shape
    hd = 2 * DIFF_HEAD_DIM
    nq = seq // tq
    lam_init = 0.8 - 0.6 * math.exp(-0.3 * layer_idx)
    slopes = 2.0 ** (-8.0 * jnp.arange(1, DIFF_HEADS + 1, dtype=F32) / DIFF_HEADS)
    vec = lambda a: a.reshape(1, DIFF_HEAD_DIM).astype(F32)
    lam_spec = pl.BlockSpec((1, DIFF_HEAD_DIM), lambda b, h, i: (0, 0))
    return pl.pallas_call(
        functools.partial(_diff_attn_body, lam_init),
        grid=(batch, DIFF_HEADS // nh, nq),
        in_specs=[
            pl.BlockSpec(memory_space=pltpu.SMEM),
            pl.BlockSpec((1, nh * hd, tq), lambda b, h, i: (b * nq + i, h, 0)),
            pl.BlockSpec((seq, nh * hd), lambda b, h, i: (b, h)),
            pl.BlockSpec((seq // tk, nh * hd, tk), lambda b, h, i: (b, h, 0)),
            lam_spec, lam_spec, lam_spec, lam_spec,
            pl.BlockSpec((hd, 1), lambda b, h, i: (0, 0)),
        ],
        out_specs=pl.BlockSpec((tq, nh * hd), lambda b, h, i: (b * nq + i, h)),
        out_shape=jax.ShapeDtypeStruct((m, dq), BF16),
        scratch_shapes=[
            pltpu.VMEM((nh, 1, 2 * tq), F32),
            pltpu.VMEM((nh, hd + ONES_ROWS, 2 * tq), F32),
            pltpu.VMEM((nh, tk, 2 * tq), F32),
            pltpu.VMEM((nh, 1, 2 * tq), F32),
        ],
        compiler_params=_params("parallel", "parallel", "arbitrary"),
        name="diff_attn",
    )(slopes, qt, k, vt, vec(lq1), vec(lk1), vec(lq2), vec(lk2), head_norm.reshape(hd, 1).astype(F32))


def _sgu_body(x_ref, g_ref, win_ref, bin_ref, vn_ref, ws_ref, bs_ref, y_ref):
    tm = x_ref.shape[0]
    width = vn_ref.shape[1]
    gd = width // SGU_GROUPS
    c = SGU_CHUNK
    h = _rms(x_ref[...], g_ref[...]).astype(BF16)
    uv = _dot(h, win_ref[...]) + bin_ref[...]
    uv = 0.5 * uv * (1.0 + jnp.tanh(math.sqrt(2.0 / math.pi) * (uv + 0.044715 * (uv * uv * uv))))
    u = uv[:, :width]
    v = _rms(uv[:, width:], vn_ref[...]).astype(BF16)
    row = lax.broadcasted_iota(jnp.int32, (c, c), 0)
    col = lax.broadcasted_iota(jnp.int32, (c, c), 1)
    causal = col <= row
    nc = tm // c
    cols = []
    for gi in range(SGU_GROUPS):
        w = jnp.where(causal, ws_ref[gi], 0.0).astype(BF16)
        vg = jnp.concatenate([v[n * c:(n + 1) * c, gi * gd:(gi + 1) * gd] for n in range(nc)], axis=1)
        sg = _dot(w, vg)
        bias = bs_ref[gi]
        cols.append(jnp.concatenate([sg[:, n * gd:(n + 1) * gd] + bias for n in range(nc)], axis=0))
    s = jnp.concatenate(cols, axis=1)
    y_ref[...] = (u * s).astype(y_ref.dtype)


def _sgu(x, gain, w_in, b_in, v_norm, w_s, b_s, tm=256):
    m, d = x.shape
    width = v_norm.shape[0]
    gd = width // SGU_GROUPS
    bs = jnp.broadcast_to(b_s[:, :, None], (SGU_GROUPS, SGU_CHUNK, gd)).astype(F32)
    fixed2 = lambda i: (0, 0)
    fixed3 = lambda i: (0, 0, 0)
    return pl.pallas_call(
        _sgu_body,
        grid=(m // tm,),
        in_specs=[
            pl.BlockSpec((tm, d), lambda i: (i, 0)),
            pl.BlockSpec((1, d), fixed2),
            pl.BlockSpec(w_in.shape, fixed2),
            pl.BlockSpec((1, 2 * width), fixed2),
            pl.BlockSpec((1, width), fixed2),
            pl.BlockSpec(w_s.shape, fixed3),
            pl.BlockSpec(bs.shape, fixed3),
        ],
        out_specs=pl.BlockSpec((tm, width), lambda i: (i, 0)),
        out_shape=jax.ShapeDtypeStruct((m, width), BF16),
        compiler_params=_params("parallel"),
        name="sgu",
    )(x, gain.reshape(1, d), w_in.astype(BF16), b_in.reshape(1, 2 * width), v_norm.reshape(1, width),
      w_s, bs)


def _gla_mixer(x, batch, seq, norm1, w_in, gw1, gw2, gb, head_norm):
    q, k, v, og, la = _gla_inproj(x, norm1, w_in, gw1, gw2, gb)
    return _gla_core(q, k, v, og, la, head_norm, batch, seq)


def _diff_mixer(x, batch, seq, norm1, w_in, lq1, lk1, lq2, lk2, head_norm, layer_idx):
    qt, k, vt = _diff_inproj(x, norm1, w_in)
    return _diff_attn(qt, k, vt, lq1, lk1, lq2, lk2, head_norm, batch, seq, layer_idx)


def kernel(x, l0_norm1, l0_w_in, l0_gate_w1, l0_gate_w2, l0_gate_b, l0_head_norm, l0_w_out, l0_norm2, l0_mlp_w1, l0_mlp_w2, l1_norm1, l1_w_in, l1_lambda_q1, l1_lambda_k1, l1_lambda_q2, l1_lambda_k2, l1_head_norm, l1_w_out, l1_norm2, l1_mlp_w1, l1_mlp_w2, l2_norm1, l2_w_in, l2_b_in, l2_v_norm, l2_w_s, l2_b_s, l2_w_out, l2_norm2, l2_mlp_w1, l2_mlp_w2, l3_norm1, l3_w_in, l3_gate_w1, l3_gate_w2, l3_gate_b, l3_head_norm, l3_w_out, l3_norm2, l3_mlp_w1, l3_mlp_w2, final_norm):
    batch, seq, d = x.shape
    h = x.reshape(batch * seq, d)
    o = _gla_mixer(h, batch, seq, l0_norm1, l0_w_in, l0_gate_w1, l0_gate_w2, l0_gate_b, l0_head_norm)
    h = _mlp(h, o, l0_w_out, l0_norm2, l0_mlp_w1, l0_mlp_w2)
    o = _diff_mixer(h, batch, seq, l1_norm1, l1_w_in, l1_lambda_q1, l1_lambda_k1, l1_lambda_q2, l1_lambda_k2,
                    l1_head_norm, 1)
    h = _mlp(h, o, l1_w_out, l1_norm2, l1_mlp_w1, l1_mlp_w2)
    o = _sgu(h, l2_norm1, l2_w_in, l2_b_in, l2_v_norm, l2_w_s, l2_b_s)
    h = _mlp(h, o, l2_w_out, l2_norm2, l2_mlp_w1, l2_mlp_w2)
    o = _gla_mixer(h, batch, seq, l3_norm1, l3_w_in, l3_gate_w1, l3_gate_w2, l3_gate_b, l3_head_norm)
    h = _mlp(h, o, l3_w_out, l3_norm2, l3_mlp_w1, l3_mlp_w2, final_gain=final_norm)
    return h.reshape(batch, seq, d)
```

```python
import functools
import math

import jax
import jax.numpy as jnp
from jax import lax
from jax.experimental import pallas as pl
from jax.experimental.pallas import tpu as pltpu

F32 = jnp.float32
BF16 = jnp.bfloat16

EPS = 1e-6
NEG_INF = -1e30

GLA_HEADS = 4
GLA_RANK = 16
GLA_GATE_NORM = 16.0
GLA_CHUNK = 64

DIFF_HEAD_DIM = 64
DIFF_HEADS = 8

SGU_CHUNK = 128
SGU_GROUPS = 8

LANES = 128
LOG2E = 1.4426950408889634
VMEM_LIMIT = 52 * 1024 * 1024


def _params(*sem):
    return pltpu.CompilerParams(dimension_semantics=sem, vmem_limit_bytes=VMEM_LIMIT)


def _dot(a, b):
    return jnp.dot(a, b, preferred_element_type=F32)


def _dot_nt(a, b):
    return lax.dot_general(a, b, (((1,), (1,)), ((), ())), preferred_element_type=F32)


def _dot_tn(a, b):
    return lax.dot_general(a, b, (((0,), (0,)), ((), ())), preferred_element_type=F32)


def _rms(x, g):
    return x * lax.rsqrt(jnp.mean(x * x, axis=-1, keepdims=True) + EPS) * g


def _gla_inproj_body(x_ref, g_ref, w_ref, gw1_ref, gw2_ref, gb_ref, q_ref, k_ref, v_ref, og_ref, la_ref):
    dk = q_ref.shape[1]
    dv = v_ref.shape[1]
    h = _rms(x_ref[...], g_ref[...]).astype(BF16)
    q_ref[...] = _dot(h, w_ref[:, 0:dk])
    k_ref[...] = _dot(h, w_ref[:, dk:2 * dk])
    v_ref[...] = _dot(h, w_ref[:, 2 * dk:2 * dk + dv]).astype(BF16)
    og_ref[...] = _dot(h, w_ref[:, 2 * dk + dv:2 * dk + 2 * dv])
    t = _dot(h, gw1_ref[...]).astype(BF16)
    z = _dot(t, gw2_ref[...]) + gb_ref[...]
    la_ref[...] = (jnp.minimum(z, 0.0) - jnp.log(1.0 + jnp.exp(-jnp.abs(z)))) / GLA_GATE_NORM


def _gla_inproj(x, gain, w_in, gw1, gw2, gb, tm=512):
    m, d = x.shape
    dk = gw2.shape[1]
    dv = (w_in.shape[1] - 2 * dk) // 2
    gw1p = jnp.zeros((d, LANES), BF16).at[:, :GLA_RANK].set(gw1.astype(BF16))
    gw2p = jnp.zeros((LANES, dk), BF16).at[:GLA_RANK, :].set(gw2.astype(BF16))
    row = lambda i: (i, 0)
    fixed = lambda i: (0, 0)
    return pl.pallas_call(
        _gla_inproj_body,
        grid=(m // tm,),
        in_specs=[
            pl.BlockSpec((tm, d), row),
            pl.BlockSpec((1, d), fixed),
            pl.BlockSpec(w_in.shape, fixed),
            pl.BlockSpec(gw1p.shape, fixed),
            pl.BlockSpec(gw2p.shape, fixed),
            pl.BlockSpec((1, dk), fixed),
        ],
        out_specs=[
            pl.BlockSpec((tm, dk), row),
            pl.BlockSpec((tm, dk), row),
            pl.BlockSpec((tm, dv), row),
            pl.BlockSpec((tm, dv), row),
            pl.BlockSpec((tm, dk), row),
        ],
        out_shape=[
            jax.ShapeDtypeStruct((m, dk), F32),
            jax.ShapeDtypeStruct((m, dk), F32),
            jax.ShapeDtypeStruct((m, dv), BF16),
            jax.ShapeDtypeStruct((m, dv), F32),
            jax.ShapeDtypeStruct((m, dk), F32),
        ],
        compiler_params=_params("parallel"),
        name="gla_inproj",
    )(x, gain.reshape(1, d), w_in.astype(BF16), gw1p, gw2p, gb.reshape(1, dk))


def _gla_body(tri_ref, q_ref, k_ref, v_ref, og_ref, la_ref, hn_ref, o_ref, st_ref):
    c = GLA_CHUNK
    tc, hk = q_ref.shape
    n_chunks = tc // c

    @pl.when(pl.program_id(2) == 0)
    def _():
        st_ref[...] = jnp.zeros_like(st_ref)

    tri = tri_ref[...]
    tg = tri.shape[0]
    la = la_ref[...]
    la_hi = la.astype(BF16)
    la_lo = (la - la_hi.astype(F32)).astype(BF16)
    b = jnp.concatenate([_dot(tri, la_hi[r:r + tg]) + _dot(tri, la_lo[r:r + tg]) for r in range(0, tc, tg)],
                        axis=0)
    b_lasts = [b[(n + 1) * c - 1:(n + 1) * c, :] for n in range(n_chunks)]
    b_end = jnp.concatenate([jnp.broadcast_to(bl, (c, hk)) for bl in b_lasts], axis=0)

    q = q_ref[...] * (hk ** -0.5)
    k = k_ref[...]
    q_dec = (q * jnp.exp(b)).astype(BF16)
    k_inv = (k * jnp.exp(-b)).astype(BF16)
    k_end = (k * jnp.exp(b_end - b)).astype(BF16)

    row = lax.broadcasted_iota(jnp.int32, (c, c), 0)
    col = lax.broadcasted_iota(jnp.int32, (c, c), 1)
    causal = col <= row
    chunk = lambda a, n: a[n * c:(n + 1) * c]

    atts = [jnp.where(causal, _dot_nt(chunk(q_dec, n), chunk(k_inv, n)), 0.0).astype(BF16)
            for n in range(n_chunks)]
    kvs = [_dot_tn(v_ref[n * c:(n + 1) * c, :], chunk(k_end, n)) for n in range(n_chunks)]
    intra = [_dot(atts[n], v_ref[n * c:(n + 1) * c, :]) for n in range(n_chunks)]

    st = st_ref[...]
    outs = []
    for n in range(n_chunks):
        outs.append(intra[n] + _dot_nt(chunk(q_dec, n), st.astype(BF16)))
        st = jnp.exp(b_lasts[n]) * st + kvs[n]
    st_ref[...] = st

    o = jnp.concatenate(outs, axis=0)
    o = o * lax.rsqrt(jnp.mean(o * o, axis=-1, keepdims=True) + EPS) * hn_ref[...]
    g = og_ref[...]
    o_ref[...] = (o * (g / (1.0 + jnp.exp(-g)))).astype(o_ref.dtype)


GLA_CUMSUM_ROWS = 256


def _gla_core(q, k, v, og, la, head_norm, batch, seq, tc=2048):
    tc = min(tc, seq)
    m, dk = q.shape
    dv = v.shape[1]
    hk = dk // GLA_HEADS
    hv = dv // GLA_HEADS
    nt = seq // tc
    idx = jnp.arange(GLA_CUMSUM_ROWS)
    tri = ((idx[:, None] // GLA_CHUNK == idx[None, :] // GLA_CHUNK)
           & (idx[None, :] <= idx[:, None])).astype(BF16)
    blk = lambda b, h, t: (b * nt + t, h)
    return pl.pallas_call(
        _gla_body,
        grid=(batch, GLA_HEADS, nt),
        in_specs=[
            pl.BlockSpec(tri.shape, lambda b, h, t: (0, 0)),
            pl.BlockSpec((tc, hk), blk),
            pl.BlockSpec((tc, hk), blk),
            pl.BlockSpec((tc, hv), blk),
            pl.BlockSpec((tc, hv), blk),
            pl.BlockSpec((tc, hk), blk),
            pl.BlockSpec((1, hv), lambda b, h, t: (0, 0)),
        ],
        out_specs=pl.BlockSpec((tc, hv), blk),
        out_shape=jax.ShapeDtypeStruct((m, dv), BF16),
        scratch_shapes=[pltpu.VMEM((hv, hk), F32)],
        compiler_params=_params("parallel", "parallel", "arbitrary"),
        name="gla_core",
    )(tri, q, k, v, og, la, head_norm.reshape(1, hv))


MLP_INIT_GROUPS = 4


def _mlp_body(final, x_ref, o_ref, wo_ref, g_ref, w1_ref, w2_ref, gf_ref, y_ref, x1_scr, h_scr, acc_scr):
    f = pl.program_id(1)

    @pl.when(f == 0)
    def _():
        acc_scr[...] = jnp.zeros_like(acc_scr)
        rows = x_ref.shape[0] // MLP_INIT_GROUPS
        for r in range(0, x_ref.shape[0], rows):
            x1 = x_ref[r:r + rows] + _dot(o_ref[r:r + rows], wo_ref[...])
            x1_scr[r:r + rows] = x1
            h_scr[r:r + rows] = _rms(x1, g_ref[...]).astype(BF16)

    a = jnp.maximum(_dot(h_scr[...], w1_ref[...]), 0.0)
    acc_scr[...] += _dot((a * a).astype(BF16), w2_ref[...])

    @pl.when(f == pl.num_programs(1) - 1)
    def _():
        y = x1_scr[...] + acc_scr[...]
        if final:
            y = _rms(y, gf_ref[...])
        y_ref[...] = y


def _mlp(x, o, w_out, gain, w1, w2, final_gain=None, tm=1024, tf=1024):
    m, d = x.shape
    kdim = o.shape[1]
    dff = w1.shape[1]
    final = final_gain is not None
    gf = (final_gain if final else gain).reshape(1, d)
    return pl.pallas_call(
        functools.partial(_mlp_body, final),
        grid=(m // tm, dff // tf),
        in_specs=[
            pl.BlockSpec((tm, d), lambda i, f: (i, 0)),
            pl.BlockSpec((tm, kdim), lambda i, f: (i, 0)),
            pl.BlockSpec((kdim, d), lambda i, f: (0, 0)),
            pl.BlockSpec((1, d), lambda i, f: (0, 0)),
            pl.BlockSpec((d, tf), lambda i, f: (0, f)),
            pl.BlockSpec((tf, d), lambda i, f: (f, 0)),
            pl.BlockSpec((1, d), lambda i, f: (0, 0)),
        ],
        out_specs=pl.BlockSpec((tm, d), lambda i, f: (i, 0)),
        out_shape=jax.ShapeDtypeStruct((m, d), F32),
        scratch_shapes=[pltpu.VMEM((tm, d), F32), pltpu.VMEM((tm, d), BF16), pltpu.VMEM((tm, d), F32)],
        compiler_params=_params("parallel", "arbitrary"),
        name="mlp",
    )(x, o, w_out.astype(BF16), gain.reshape(1, d), w1.astype(BF16), w2.astype(BF16), gf)


Q_BLOCK = 256
KEY_BLOCK = 512


def _diff_inproj_body(x_ref, g_ref, wqt_ref, wk_ref, wvt_ref, qt_ref, k_ref, vt_ref):
    tq = qt_ref.shape[2]
    tk = vt_ref.shape[2]
    h = _rms(x_ref[...], g_ref[...]).astype(BF16)
    k_ref[...] = _dot(h, wk_ref[...]).astype(BF16)
    qt = _dot_nt(wqt_ref[...], h) * (LOG2E * DIFF_HEAD_DIM ** -0.5)
    vt = _dot_nt(wvt_ref[...], h)
    for j in range(qt_ref.shape[0]):
        qt_ref[j] = qt[:, j * tq:(j + 1) * tq].astype(BF16)
    for j in range(vt_ref.shape[0]):
        vt_ref[j] = vt[:, j * tk:(j + 1) * tk].astype(BF16)


def _diff_inproj(x, gain, w_in, tm=512):
    m, d = x.shape
    dq = 2 * DIFF_HEADS * DIFF_HEAD_DIM
    wqt = w_in[:, :dq].T.astype(BF16)
    wk = w_in[:, dq:2 * dq].astype(BF16)
    wvt = w_in[:, 2 * dq:].T.astype(BF16)
    dv = wvt.shape[0]
    tq, tk = Q_BLOCK, KEY_BLOCK
    fixed = lambda i: (0, 0)
    return pl.pallas_call(
        _diff_inproj_body,
        grid=(m // tm,),
        in_specs=[
            pl.BlockSpec((tm, d), lambda i: (i, 0)),
            pl.BlockSpec((1, d), fixed),
            pl.BlockSpec(wqt.shape, fixed),
            pl.BlockSpec(wk.shape, fixed),
            pl.BlockSpec(wvt.shape, fixed),
        ],
        out_specs=[
            pl.BlockSpec((tm // tq, dq, tq), lambda i: (i, 0, 0)),
            pl.BlockSpec((tm, dq), lambda i: (i, 0)),
            pl.BlockSpec((tm // tk, dv, tk), lambda i: (i, 0, 0)),
        ],
        out_shape=[
            jax.ShapeDtypeStruct((m // tq, dq, tq), BF16),
            jax.ShapeDtypeStruct((m, dq), BF16),
            jax.ShapeDtypeStruct((m // tk, dv, tk), BF16),
        ],
        compiler_params=_params("parallel"),
        name="diff_inproj",
    )(x, gain.reshape(1, d), wqt, wk, wvt)


ATT_HEADS_PER_STEP = 4
KEY_SPLIT = 32
ONES_ROWS = 16
FEAT_ROWS = 8


def _bf16_pieces(c):
    c1 = c.astype(BF16).astype(F32)
    c2 = (c - c1).astype(BF16).astype(F32)
    c3 = (c - c1 - c2).astype(BF16).astype(F32)
    return c1, c2, c3


def _diff_attn_body(lam_init, slopes_ref, qt_ref, k_ref, vt_ref, lq1_ref, lk1_ref, lq2_ref, lk2_ref, hn_ref,
                    o_ref, m_scr, acc_scr, s_scr, mx_scr):
    tq, tk = Q_BLOCK, KEY_BLOCK
    hd = 2 * DIFF_HEAD_DIM
    nh = ATT_HEADS_PER_STEP
    qi = pl.program_id(2)
    hg = pl.program_id(1)
    n_full = qi // (tk // tq)
    odd = qi % (tk // tq)

    def mask(t):
        key = lax.broadcasted_iota(jnp.int32, (tk, 2 * tq), 0)
        qry = lax.broadcasted_iota(jnp.int32, (tk, 2 * tq), 1) % tq
        return jnp.where((key - qry) <= odd * tq, t, NEG_INF)

    kj = lax.broadcasted_iota(jnp.int32, (tk, LANES), 0)
    kl = lax.broadcasted_iota(jnp.int32, (tk, LANES), 1)
    kfeat = jnp.where(kl < 3, kj // KEY_SPLIT, jnp.where(kl < 6, kj % KEY_SPLIT, 0)).astype(F32).astype(BF16)
    ones = jnp.ones((ONES_ROWS, tk), BF16)
    half = lax.broadcasted_iota(jnp.int32, (hd, tq), 0) < DIFF_HEAD_DIM
    frow = lax.broadcasted_iota(jnp.int32, (FEAT_ROWS, 2 * tq), 0)

    heads = []
    for j in range(nh):
        slope = slopes_ref[hg * nh + j] * LOG2E
        qt = qt_ref[0, j * hd:(j + 1) * hd, :]
        zero = jnp.zeros_like(qt)
        q2t = jnp.concatenate([jnp.where(half, qt, zero), jnp.where(half, zero, qt)], axis=1)
        c1, c2, c3 = _bf16_pieces(jnp.full(frow.shape, slope, F32))
        s = float(KEY_SPLIT)
        qfeat = jnp.where(frow == 0, s * c1, jnp.where(frow == 1, s * c2, jnp.where(frow == 2, s * c3,
                jnp.where(frow == 3, c1, jnp.where(frow == 4, c2, jnp.where(frow == 5, c3, 0.0))))))
        qfeat = jnp.concatenate([qfeat, jnp.zeros((hd - FEAT_ROWS, 2 * tq), F32)], axis=0).astype(BF16)
        heads.append((slope, jnp.concatenate([q2t, qfeat], axis=0)))

    m_scr[...] = jnp.full_like(m_scr, NEG_INF)
    acc_scr[...] = jnp.zeros_like(acc_scr)

    def scores(j, kb):
        k_blk = k_ref[pl.ds(pl.multiple_of(kb * tk, tk), tk), j * hd:(j + 1) * hd]
        return _dot(jnp.concatenate([k_blk, kfeat], axis=1), heads[j][1])

    def stage(j, t):
        s_scr[j] = t
        mx_scr[j] = jnp.max(t, axis=0, keepdims=True)

    def consume(j, kb):
        shift = heads[j][0] * (kb * tk - qi * tq).astype(F32)
        m_old = m_scr[j]
        m_new = jnp.maximum(m_old, mx_scr[j] + shift)
        alpha = jnp.exp2(m_old - m_new)
        p = jnp.exp2(s_scr[j] - (m_new - shift)).astype(BF16)
        vt_aug = jnp.concatenate([vt_ref[kb, j * hd:(j + 1) * hd, :], ones], axis=0)
        acc_scr[j] = alpha * acc_scr[j] + _dot(vt_aug, p)
        m_scr[j] = m_new

    def pipelined(kb, masked):
        for j in range(nh):
            t_next = scores(j, kb + 1)
            if masked:
                t_next = mask(t_next)
            consume(j, kb)
            stage(j, t_next)

    @pl.when(n_full == 0)
    def _():
        for j in range(nh):
            stage(j, mask(scores(j, 0)))

    @pl.when(n_full > 0)
    def _():
        for j in range(nh):
            stage(j, scores(j, 0))

    def full_tile(kb, carry):
        pipelined(kb, False)
        return carry

    lax.fori_loop(0, n_full - 1, full_tile, 0)

    @pl.when(n_full > 0)
    def _():
        pipelined(n_full - 1, True)

    for j in range(nh):
        consume(j, n_full)

    lam = (jnp.exp(jnp.sum(lq1_ref[...] * lk1_ref[...], keepdims=True))
           - jnp.exp(jnp.sum(lq2_ref[...] * lk2_ref[...], keepdims=True)) + lam_init)
    for j in range(nh):
        acc = acc_scr[j]
        o = acc[:hd] / acc[hd:hd + 1]
        ot = o[:, :tq] - lam * o[:, tq:]
        ot = ot * lax.rsqrt(jnp.mean(ot * ot, axis=0, keepdims=True) + EPS)
        ot = ot * hn_ref[...] * (1.0 - lam_init)
        o_ref[:, j * hd:(j + 1) * hd] = ot.T.astype(o_ref.dtype)


def _diff_attn(qt, k, vt, lq1, lk1, lq2, lk2, head_norm, batch, seq, layer_idx):
    tq, tk = Q_BLOCK, KEY_BLOCK
    nh = ATT_HEADS_PER_STEP
    m, dq = k.shape
    hd = 2 * DIFF_HEAD_DIM
    nq = seq // tq
    lam_init = 0.8 - 0.6 * math.exp(-0.3 * layer_idx)
    slopes = 2.0 ** (-8.0 * jnp.arange(1, DIFF_HEADS + 1, dtype=F32) / DIFF_HEADS)
    vec = lambda a: a.reshape(1, DIFF_HEAD_DIM).astype(F32)
    lam_spec = pl.BlockSpec((1, DIFF_HEAD_DIM), lambda b, h, i: (0, 0))
    return pl.pallas_call(
        functools.partial(_diff_attn_body, lam_init),
        grid=(batch, DIFF_HEADS // nh, nq),
        in_specs=[
            pl.BlockSpec(memory_space=pltpu.SMEM),
            pl.BlockSpec((1, nh * hd, tq), lambda b, h, i: (b * nq + i, h, 0)),
            pl.BlockSpec((seq, nh * hd), lambda b, h, i: (b, h)),
            pl.BlockSpec((seq // tk, nh * hd, tk), lambda b, h, i: (b, h, 0)),
            lam_spec, lam_spec, lam_spec, lam_spec,
            pl.BlockSpec((hd, 1), lambda b, h, i: (0, 0)),
        ],
        out_specs=pl.BlockSpec((tq, nh * hd), lambda b, h, i: (b * nq + i, h)),
        out_shape=jax.ShapeDtypeStruct((m, dq), BF16),
        scratch_shapes=[
            pltpu.VMEM((nh, 1, 2 * tq), F32),
            pltpu.VMEM((nh, hd + ONES_ROWS, 2 * tq), F32),
            pltpu.VMEM((nh, tk, 2 * tq), F32),
            pltpu.VMEM((nh, 1, 2 * tq), F32),
        ],
        compiler_params=_params("parallel", "parallel", "arbitrary"),
        name="diff_attn",
    )(slopes, qt, k, vt, vec(lq1), vec(lk1), vec(lq2), vec(lk2), head_norm.reshape(hd, 1).astype(F32))


def _sgu_body(x_ref, g_ref, win_ref, bin_ref, vn_ref, ws_ref, bs_ref, y_ref):
    tm = x_ref.shape[0]
    width = vn_ref.shape[1]
    gd = width // SGU_GROUPS
    c = SGU_CHUNK
    h = _rms(x_ref[...], g_ref[...]).astype(BF16)
    uv = _dot(h, win_ref[...]) + bin_ref[...]
    uv = 0.5 * uv * (1.0 + jnp.tanh(math.sqrt(2.0 / math.pi) * (uv + 0.044715 * (uv * uv * uv))))
    u = uv[:, :width]
    v = _rms(uv[:, width:], vn_ref[...]).astype(BF16)
    row = lax.broadcasted_iota(jnp.int32, (c, c), 0)
    col = lax.broadcasted_iota(jnp.int32, (c, c), 1)
    causal = col <= row
    nc = tm // c
    cols = []
    for gi in range(SGU_GROUPS):
        w = jnp.where(causal, ws_ref[gi], 0.0).astype(BF16)
        vg = jnp.concatenate([v[n * c:(n + 1) * c, gi * gd:(gi + 1) * gd] for n in range(nc)], axis=1)
        sg = _dot(w, vg)
        bias = bs_ref[gi]
        cols.append(jnp.concatenate([sg[:, n * gd:(n + 1) * gd] + bias for n in range(nc)], axis=0))
    s = jnp.concatenate(cols, axis=1)
    y_ref[...] = (u * s).astype(y_ref.dtype)


def _sgu(x, gain, w_in, b_in, v_norm, w_s, b_s, tm=512):
    m, d = x.shape
    width = v_norm.shape[0]
    gd = width // SGU_GROUPS
    bs = jnp.broadcast_to(b_s[:, :, None], (SGU_GROUPS, SGU_CHUNK, gd)).astype(F32)
    fixed2 = lambda i: (0, 0)
    fixed3 = lambda i: (0, 0, 0)
    return pl.pallas_call(
        _sgu_body,
        grid=(m // tm,),
        in_specs=[
            pl.BlockSpec((tm, d), lambda i: (i, 0)),
            pl.BlockSpec((1, d), fixed2),
            pl.BlockSpec(w_in.shape, fixed2),
            pl.BlockSpec((1, 2 * width), fixed2),
            pl.BlockSpec((1, width), fixed2),
            pl.BlockSpec(w_s.shape, fixed3),
            pl.BlockSpec(bs.shape, fixed3),
        ],
        out_specs=pl.BlockSpec((tm, width), lambda i: (i, 0)),
        out_shape=jax.ShapeDtypeStruct((m, width), BF16),
        compiler_params=_params("parallel"),
        name="sgu",
    )(x, gain.reshape(1, d), w_in.astype(BF16), b_in.reshape(1, 2 * width), v_norm.reshape(1, width),
      w_s, bs)


def _gla_mixer(x, batch, seq, norm1, w_in, gw1, gw2, gb, head_norm):
    q, k, v, og, la = _gla_inproj(x, norm1, w_in, gw1, gw2, gb)
    return _gla_core(q, k, v, og, la, head_norm, batch, seq)


def _diff_mixer(x, batch, seq, norm1, w_in, lq1, lk1, lq2, lk2, head_norm, layer_idx):
    qt, k, vt = _diff_inproj(x, norm1, w_in)
    return _diff_attn(qt, k, vt, lq1, lk1, lq2, lk2, head_norm, batch, seq, layer_idx)


def kernel(x, l0_norm1, l0_w_in, l0_gate_w1, l0_gate_w2, l0_gate_b, l0_head_norm, l0_w_out, l0_norm2, l0_mlp_w1, l0_mlp_w2, l1_norm1, l1_w_in, l1_lambda_q1, l1_lambda_k1, l1_lambda_q2, l1_lambda_k2, l1_head_norm, l1_w_out, l1_norm2, l1_mlp_w1, l1_mlp_w2, l2_norm1, l2_w_in, l2_b_in, l2_v_norm, l2_w_s, l2_b_s, l2_w_out, l2_norm2, l2_mlp_w1, l2_mlp_w2, l3_norm1, l3_w_in, l3_gate_w1, l3_gate_w2, l3_gate_b, l3_head_norm, l3_w_out, l3_norm2, l3_mlp_w1, l3_mlp_w2, final_norm):
    batch, seq, d = x.shape
    h = x.reshape(batch * seq, d)
    o = _gla_mixer(h, batch, seq, l0_norm1, l0_w_in, l0_gate_w1, l0_gate_w2, l0_gate_b, l0_head_norm)
    h = _mlp(h, o, l0_w_out, l0_norm2, l0_mlp_w1, l0_mlp_w2)
    o = _diff_mixer(h, batch, seq, l1_norm1, l1_w_in, l1_lambda_q1, l1_lambda_k1, l1_lambda_q2, l1_lambda_k2,
                    l1_head_norm, 1)
    h = _mlp(h, o, l1_w_out, l1_norm2, l1_mlp_w1, l1_mlp_w2)
    o = _sgu(h, l2_norm1, l2_w_in, l2_b_in, l2_v_norm, l2_w_s, l2_b_s)
    h = _mlp(h, o, l2_w_out, l2_norm2, l2_mlp_w1, l2_mlp_w2)
    o = _gla_mixer(h, batch, seq, l3_norm1, l3_w_in, l3_gate_w1, l3_gate_w2, l3_gate_b, l3_head_norm)
    h = _mlp(h, o, l3_w_out, l3_norm2, l3_mlp_w1, l3_mlp_w2, final_gain=final_norm)
    return h.reshape(batch, seq, d)
```

```python
import functools
import math

import jax
import jax.numpy as jnp
from jax import lax
from jax.experimental import pallas as pl
from jax.experimental.pallas import tpu as pltpu

F32 = jnp.float32
BF16 = jnp.bfloat16

EPS = 1e-6
NEG_INF = -1e30

GLA_HEADS = 4
GLA_RANK = 16
GLA_GATE_NORM = 16.0
GLA_CHUNK = 64

DIFF_HEAD_DIM = 64
DIFF_HEADS = 8

SGU_CHUNK = 128
SGU_GROUPS = 8

LANES = 128
LOG2E = 1.4426950408889634
VMEM_LIMIT = 52 * 1024 * 1024


def _params(*sem):
    return pltpu.CompilerParams(dimension_semantics=sem, vmem_limit_bytes=VMEM_LIMIT)


def _dot(a, b):
    return jnp.dot(a, b, preferred_element_type=F32)


def _dot_nt(a, b):
    return lax.dot_general(a, b, (((1,), (1,)), ((), ())), preferred_element_type=F32)


def _dot_tn(a, b):
    return lax.dot_general(a, b, (((0,), (0,)), ((), ())), preferred_element_type=F32)


def _rms(x, g):
    return x * lax.rsqrt(jnp.mean(x * x, axis=-1, keepdims=True) + EPS) * g


INPROJ_GROUPS = 2


def _gla_inproj_body(x_ref, g_ref, w_ref, gw1_ref, gw2_ref, gb_ref, q_ref, k_ref, v_ref, og_ref, la_ref):
    dk = q_ref.shape[1]
    dv = v_ref.shape[1]
    rows = x_ref.shape[0] // INPROJ_GROUPS
    for r in range(0, x_ref.shape[0], rows):
        sl = slice(r, r + rows)
        h = _rms(x_ref[sl], g_ref[...]).astype(BF16)
        t = _dot(h, gw1_ref[...]).astype(BF16)
        q_ref[sl] = _dot(h, w_ref[:, 0:dk].astype(BF16))
        z = _dot(t, gw2_ref[...]) + gb_ref[...]
        k_ref[sl] = _dot(h, w_ref[:, dk:2 * dk].astype(BF16))
        la_ref[sl] = (jnp.minimum(z, 0.0) - jnp.log(1.0 + jnp.exp(-jnp.abs(z)))) / GLA_GATE_NORM
        v_ref[sl] = _dot(h, w_ref[:, 2 * dk:2 * dk + dv].astype(BF16)).astype(BF16)
        og_ref[sl] = _dot(h, w_ref[:, 2 * dk + dv:2 * dk + 2 * dv].astype(BF16))


def _gla_inproj(x, gain, w_in, gw1, gw2, gb, tm=512):
    m, d = x.shape
    dk = gw2.shape[1]
    dv = (w_in.shape[1] - 2 * dk) // 2
    gw1p = jnp.zeros((d, LANES), BF16).at[:, :GLA_RANK].set(gw1.astype(BF16))
    gw2p = jnp.zeros((LANES, dk), BF16).at[:GLA_RANK, :].set(gw2.astype(BF16))
    row = lambda i: (i, 0)
    fixed = lambda i: (0, 0)
    return pl.pallas_call(
        _gla_inproj_body,
        grid=(m // tm,),
        in_specs=[
            pl.BlockSpec((tm, d), row),
            pl.BlockSpec((1, d), fixed),
            pl.BlockSpec(w_in.shape, fixed),
            pl.BlockSpec(gw1p.shape, fixed),
            pl.BlockSpec(gw2p.shape, fixed),
            pl.BlockSpec((1, dk), fixed),
        ],
        out_specs=[
            pl.BlockSpec((tm, dk), row),
            pl.BlockSpec((tm, dk), row),
            pl.BlockSpec((tm, dv), row),
            pl.BlockSpec((tm, dv), row),
            pl.BlockSpec((tm, dk), row),
        ],
        out_shape=[
            jax.ShapeDtypeStruct((m, dk), F32),
            jax.ShapeDtypeStruct((m, dk), F32),
            jax.ShapeDtypeStruct((m, dv), BF16),
            jax.ShapeDtypeStruct((m, dv), F32),
            jax.ShapeDtypeStruct((m, dk), F32),
        ],
        compiler_params=_params("parallel"),
        name="gla_inproj",
    )(x, gain.reshape(1, d), w_in, gw1p, gw2p, gb.reshape(1, dk))


def _gla_body(tri_ref, q_ref, k_ref, v_ref, og_ref, la_ref, hn_ref, o_ref, st_ref):
    c = GLA_CHUNK
    tc, hk = q_ref.shape
    n_chunks = tc // c

    @pl.when(pl.program_id(2) == 0)
    def _():
        st_ref[...] = jnp.zeros_like(st_ref)

    tri = tri_ref[...]
    tg = tri.shape[0]
    la = la_ref[...]
    la_hi = la.astype(BF16)
    la_lo = (la - la_hi.astype(F32)).astype(BF16)
    b = jnp.concatenate([_dot(tri, la_hi[r:r + tg]) + _dot(tri, la_lo[r:r + tg]) for r in range(0, tc, tg)],
                        axis=0)
    b_lasts = [b[(n + 1) * c - 1:(n + 1) * c, :] for n in range(n_chunks)]
    b_end = jnp.concatenate([jnp.broadcast_to(bl, (c, hk)) for bl in b_lasts], axis=0)

    q = q_ref[...] * (hk ** -0.5)
    k = k_ref[...]
    q_dec = (q * jnp.exp(b)).astype(BF16)
    k_inv = (k * jnp.exp(-b)).astype(BF16)
    k_end = (k * jnp.exp(b_end - b)).astype(BF16)

    row = lax.broadcasted_iota(jnp.int32, (c, c), 0)
    col = lax.broadcasted_iota(jnp.int32, (c, c), 1)
    causal = col <= row
    chunk = lambda a, n: a[n * c:(n + 1) * c]

    atts = [jnp.where(causal, _dot_nt(chunk(q_dec, n), chunk(k_inv, n)), 0.0).astype(BF16)
            for n in range(n_chunks)]
    kvs = [_dot_tn(v_ref[n * c:(n + 1) * c, :], chunk(k_end, n)) for n in range(n_chunks)]
    intra = [_dot(atts[n], v_ref[n * c:(n + 1) * c, :]) for n in range(n_chunks)]

    st = st_ref[...]
    outs = []
    for n in range(n_chunks):
        outs.append(intra[n] + _dot_nt(chunk(q_dec, n), st.astype(BF16)))
        st = jnp.exp(b_lasts[n]) * st + kvs[n]
    st_ref[...] = st

    o = jnp.concatenate(outs, axis=0)
    o = o * lax.rsqrt(jnp.mean(o * o, axis=-1, keepdims=True) + EPS) * hn_ref[...]
    g = og_ref[...]
    o_ref[...] = (o * (g / (1.0 + jnp.exp(-g)))).astype(o_ref.dtype)


GLA_CUMSUM_ROWS = 256


def _gla_core(q, k, v, og, la, head_norm, batch, seq, tc=2048):
    tc = min(tc, seq)
    m, dk = q.shape
    dv = v.shape[1]
    hk = dk // GLA_HEADS
    hv = dv // GLA_HEADS
    nt = seq // tc
    idx = jnp.arange(GLA_CUMSUM_ROWS)
    tri = ((idx[:, None] // GLA_CHUNK == idx[None, :] // GLA_CHUNK)
           & (idx[None, :] <= idx[:, None])).astype(BF16)
    blk = lambda b, h, t: (b * nt + t, h)
    return pl.pallas_call(
        _gla_body,
        grid=(batch, GLA_HEADS, nt),
        in_specs=[
            pl.BlockSpec(tri.shape, lambda b, h, t: (0, 0)),
            pl.BlockSpec((tc, hk), blk),
            pl.BlockSpec((tc, hk), blk),
            pl.BlockSpec((tc, hv), blk),
            pl.BlockSpec((tc, hv), blk),
            pl.BlockSpec((tc, hk), blk),
            pl.BlockSpec((1, hv), lambda b, h, t: (0, 0)),
        ],
        out_specs=pl.BlockSpec((tc, hv), blk),
        out_shape=jax.ShapeDtypeStruct((m, dv), BF16),
        scratch_shapes=[pltpu.VMEM((hv, hk), F32)],
        compiler_params=_params("parallel", "parallel", "arbitrary"),
        name="gla_core",
    )(tri, q, k, v, og, la, head_norm.reshape(1, hv))


MLP_INIT_GROUPS = 4


def _mlp_body(final, x_ref, o_ref, wo_ref, g_ref, w1_ref, w2_ref, gf_ref, y_ref, x1_scr, h_scr, acc_scr):
    f = pl.program_id(1)

    @pl.when(f == 0)
    def _():
        acc_scr[...] = jnp.zeros_like(acc_scr)
        rows = x_ref.shape[0] // MLP_INIT_GROUPS
        wo = wo_ref[...].astype(BF16)
        for r in range(0, x_ref.shape[0], rows):
            x1 = x_ref[r:r + rows] + _dot(o_ref[r:r + rows], wo)
            x1_scr[r:r + rows] = x1
            h_scr[r:r + rows] = _rms(x1, g_ref[...]).astype(BF16)

    a = jnp.maximum(_dot(h_scr[...], w1_ref[...]), 0.0)
    acc_scr[...] += _dot((a * a).astype(BF16), w2_ref[...])

    @pl.when(f == pl.num_programs(1) - 1)
    def _():
        y = x1_scr[...] + acc_scr[...]
        if final:
            y = _rms(y, gf_ref[...])
        y_ref[...] = y


def _mlp(x, o, w_out, gain, w1, w2, final_gain=None, tm=1024, tf=1024):
    m, d = x.shape
    kdim = o.shape[1]
    dff = w1.shape[1]
    final = final_gain is not None
    gf = (final_gain if final else gain).reshape(1, d)
    return pl.pallas_call(
        functools.partial(_mlp_body, final),
        grid=(m // tm, dff // tf),
        in_specs=[
            pl.BlockSpec((tm, d), lambda i, f: (i, 0)),
            pl.BlockSpec((tm, kdim), lambda i, f: (i, 0)),
            pl.BlockSpec((kdim, d), lambda i, f: (0, 0)),
            pl.BlockSpec((1, d), lambda i, f: (0, 0)),
            pl.BlockSpec((d, tf), lambda i, f: (0, f)),
            pl.BlockSpec((tf, d), lambda i, f: (f, 0)),
            pl.BlockSpec((1, d), lambda i, f: (0, 0)),
        ],
        out_specs=pl.BlockSpec((tm, d), lambda i, f: (i, 0)),
        out_shape=jax.ShapeDtypeStruct((m, d), F32),
        scratch_shapes=[pltpu.VMEM((tm, d), F32), pltpu.VMEM((tm, d), BF16), pltpu.VMEM((tm, d), F32)],
        compiler_params=_params("parallel", "arbitrary"),
        name="mlp",
    )(x, o, w_out, gain.reshape(1, d), w1.astype(BF16), w2.astype(BF16), gf)


Q_BLOCK = 256
KEY_BLOCK = 512


def _diff_inproj_body(x_ref, g_ref, wqt_ref, wk_ref, wvt_ref, qt_ref, k_ref, vt_ref):
    tq = qt_ref.shape[2]
    tk = vt_ref.shape[2]
    for r in range(0, x_ref.shape[0], tq):
        h = _rms(x_ref[r:r + tq], g_ref[...]).astype(BF16)
        k_ref[r:r + tq] = _dot(h, wk_ref[...]).astype(BF16)
        qt = _dot_nt(wqt_ref[...], h) * (LOG2E * DIFF_HEAD_DIM ** -0.5)
        qt_ref[r // tq] = qt.astype(BF16)
        vt_ref[r // tk, :, r % tk:r % tk + tq] = _dot_nt(wvt_ref[...], h).astype(BF16)


def _diff_inproj(x, gain, w_in, tm=512):
    m, d = x.shape
    dq = 2 * DIFF_HEADS * DIFF_HEAD_DIM
    wqt = w_in[:, :dq].T.astype(BF16)
    wk = w_in[:, dq:2 * dq].astype(BF16)
    wvt = w_in[:, 2 * dq:].T.astype(BF16)
    dv = wvt.shape[0]
    tq, tk = Q_BLOCK, KEY_BLOCK
    fixed = lambda i: (0, 0)
    return pl.pallas_call(
        _diff_inproj_body,
        grid=(m // tm,),
        in_specs=[
            pl.BlockSpec((tm, d), lambda i: (i, 0)),
            pl.BlockSpec((1, d), fixed),
            pl.BlockSpec(wqt.shape, fixed),
            pl.BlockSpec(wk.shape, fixed),
            pl.BlockSpec(wvt.shape, fixed),
        ],
        out_specs=[
            pl.BlockSpec((tm // tq, dq, tq), lambda i: (i, 0, 0)),
            pl.BlockSpec((tm, dq), lambda i: (i, 0)),
            pl.BlockSpec((tm // tk, dv, tk), lambda i: (i, 0, 0)),
        ],
        out_shape=[
            jax.ShapeDtypeStruct((m // tq, dq, tq), BF16),
            jax.ShapeDtypeStruct((m, dq), BF16),
            jax.ShapeDtypeStruct((m // tk, dv, tk), BF16),
        ],
        compiler_params=_params("parallel"),
        name="diff_inproj",
    )(x, gain.reshape(1, d), wqt, wk, wvt)


ATT_HEADS_PER_STEP = 4
KEY_SPLIT = 32
ONES_ROWS = 16
FEAT_ROWS = 8


def _bf16_pieces(c):
    c1 = c.astype(BF16).astype(F32)
    c2 = (c - c1).astype(BF16).astype(F32)
    c3 = (c - c1 - c2).astype(BF16).astype(F32)
    return c1, c2, c3


def _diff_attn_body(lam_init, slopes_ref, qt_ref, k_ref, vt_ref, lq1_ref, lk1_ref, lq2_ref, lk2_ref, hn_ref,
                    o_ref, m_scr, acc_scr, s_scr, mx_scr):
    tq, tk = Q_BLOCK, KEY_BLOCK
    hd = 2 * DIFF_HEAD_DIM
    nh = ATT_HEADS_PER_STEP
    qi = pl.program_id(2)
    hg = pl.program_id(1)
    n_full = qi // (tk // tq)
    odd = qi % (tk // tq)

    def mask(t):
        key = lax.broadcasted_iota(jnp.int32, (tk, 2 * tq), 0)
        qry = lax.broadcasted_iota(jnp.int32, (tk, 2 * tq), 1) % tq
        return jnp.where((key - qry) <= odd * tq, t, NEG_INF)

    kj = lax.broadcasted_iota(jnp.int32, (tk, LANES), 0)
    kl = lax.broadcasted_iota(jnp.int32, (tk, LANES), 1)
    kfeat = jnp.where(kl < 3, kj // KEY_SPLIT, jnp.where(kl < 6, kj % KEY_SPLIT, 0)).astype(F32).astype(BF16)
    ones = jnp.ones((ONES_ROWS, tk), BF16)
    half = lax.broadcasted_iota(jnp.int32, (hd, tq), 0) < DIFF_HEAD_DIM
    frow = lax.broadcasted_iota(jnp.int32, (FEAT_ROWS, 2 * tq), 0)

    heads = []
    for j in range(nh):
        slope = slopes_ref[hg * nh + j] * LOG2E
        qt = qt_ref[0, j * hd:(j + 1) * hd, :]
        zero = jnp.zeros_like(qt)
        q2t = jnp.concatenate([jnp.where(half, qt, zero), jnp.where(half, zero, qt)], axis=1)
        c1, c2, c3 = _bf16_pieces(jnp.full(frow.shape, slope, F32))
        s = float(KEY_SPLIT)
        qfeat = jnp.where(frow == 0, s * c1, jnp.where(frow == 1, s * c2, jnp.where(frow == 2, s * c3,
                jnp.where(frow == 3, c1, jnp.where(frow == 4, c2, jnp.where(frow == 5, c3, 0.0))))))
        qfeat = jnp.concatenate([qfeat, jnp.zeros((hd - FEAT_ROWS, 2 * tq), F32)], axis=0).astype(BF16)
        heads.append((slope, jnp.concatenate([q2t, qfeat], axis=0)))

    m_scr[...] = jnp.full_like(m_scr, NEG_INF)
    acc_scr[...] = jnp.zeros_like(acc_scr)

    def scores(j, kb):
        k_blk = k_ref[pl.ds(pl.multiple_of(kb * tk, tk), tk), j * hd:(j + 1) * hd]
        return _dot(jnp.concatenate([k_blk, kfeat], axis=1), heads[j][1])

    def stage(j, t):
        s_scr[j] = t
        mx_scr[j] = jnp.max(t, axis=0, keepdims=True)

    def consume(j, kb):
        shift = heads[j][0] * (kb * tk - qi * tq).astype(F32)
        m_old = m_scr[j]
        m_new = jnp.maximum(m_old, mx_scr[j] + shift)
        alpha = jnp.exp2(m_old - m_new)
        p = jnp.exp2(s_scr[j] - (m_new - shift)).astype(BF16)
        vt_aug = jnp.concatenate([vt_ref[kb, j * hd:(j + 1) * hd, :], ones], axis=0)
        acc_scr[j] = alpha * acc_scr[j] + _dot(vt_aug, p)
        m_scr[j] = m_new

    def pipelined(kb, masked):
        for j in range(nh):
            t_next = scores(j, kb + 1)
            if masked:
                t_next = mask(t_next)
            consume(j, kb)
            stage(j, t_next)

    @pl.when(n_full == 0)
    def _():
        for j in range(nh):
            stage(j, mask(scores(j, 0)))

    @pl.when(n_full > 0)
    def _():
        for j in range(nh):
            stage(j, scores(j, 0))

    def full_tile(kb, carry):
        pipelined(kb, False)
        return carry

    lax.fori_loop(0, n_full - 1, full_tile, 0)

    @pl.when(n_full > 0)
    def _():
        pipelined(n_full - 1, True)

    for j in range(nh):
        consume(j, n_full)

    lam = (jnp.exp(jnp.sum(lq1_ref[...] * lk1_ref[...], keepdims=True))
           - jnp.exp(jnp.sum(lq2_ref[...] * lk2_ref[...], keepdims=True)) + lam_init)
    for j in range(nh):
        acc = acc_scr[j]
        o = acc[:hd] / acc[hd:hd + 1]
        ot = o[:, :tq] - lam * o[:, tq:]
        ot = ot * lax.rsqrt(jnp.mean(ot * ot, axis=0, keepdims=True) + EPS)
        ot = ot * hn_ref[...] * (1.0 - lam_init)
        o_ref[:, j * hd:(j + 1) * hd] = ot.T.astype(o_ref.dtype)


def _diff_attn(qt, k, vt, lq1, lk1, lq2, lk2, head_norm, batch, seq, layer_idx):
    tq, tk = Q_BLOCK, KEY_BLOCK
    nh = ATT_HEADS_PER_STEP
    m, dq = k.shape
    hd = 2 * DIFF_HEAD_DIM
    nq = seq // tq
    lam_init = 0.8 - 0.6 * math.exp(-0.3 * layer_idx)
    slopes = 2.0 ** (-8.0 * jnp.arange(1, DIFF_HEADS + 1, dtype=F32) / DIFF_HEADS)
    vec = lambda a: a.reshape(1, DIFF_HEAD_DIM).astype(F32)
    lam_spec = pl.BlockSpec((1, DIFF_HEAD_DIM), lambda b, h, i: (0, 0))
    return pl.pallas_call(
        functools.partial(_diff_attn_body, lam_init),
        grid=(batch, DIFF_HEADS // nh, nq),
        in_specs=[
            pl.BlockSpec(memory_space=pltpu.SMEM),
            pl.BlockSpec((1, nh * hd, tq), lambda b, h, i: (b * nq + i, h, 0)),
            pl.BlockSpec((seq, nh * hd), lambda b, h, i: (b, h)),
            pl.BlockSpec((seq // tk, nh * hd, tk), lambda b, h, i: (b, h, 0)),
            lam_spec, lam_spec, lam_spec, lam_spec,
            pl.BlockSpec((hd, 1), lambda b, h, i: (0, 0)),
        ],
        out_specs=pl.BlockSpec((tq, nh * hd), lambda b, h, i: (b * nq + i, h)),
        out_shape=jax.ShapeDtypeStruct((m, dq), BF16),
        scratch_shapes=[
            pltpu.VMEM((nh, 1, 2 * tq), F32),
            pltpu.VMEM((nh, hd + ONES_ROWS, 2 * tq), F32),
            pltpu.VMEM((nh, tk, 2 * tq), F32),
            pltpu.VMEM((nh, 1, 2 * tq), F32),
        ],
        compiler_params=_params("parallel", "parallel", "arbitrary"),
        name="diff_attn",
    )(slopes, qt, k, vt, vec(lq1), vec(lk1), vec(lq2), vec(lk2), head_norm.reshape(hd, 1).astype(F32))


def _sgu_body(x_ref, g_ref, win_ref, bin_ref, vn_ref, ws_ref, bs_ref, y_ref):
    tm = x_ref.shape[0]
    width = vn_ref.shape[1]
    gd = width // SGU_GROUPS
    c = SGU_CHUNK
    h = _rms(x_ref[...], g_ref[...]).astype(BF16)
    uv = _dot(h, win_ref[...].astype(BF16)) + bin_ref[...]
    uv = 0.5 * uv * (1.0 + jnp.tanh(math.sqrt(2.0 / math.pi) * (uv + 0.044715 * (uv * uv * uv))))
    u = uv[:, :width]
    v = _rms(uv[:, width:], vn_ref[...]).astype(BF16)
    row = lax.broadcasted_iota(jnp.int32, (c, c), 0)
    col = lax.broadcasted_iota(jnp.int32, (c, c), 1)
    causal = col <= row
    nc = tm // c
    cols = []
    for gi in range(SGU_GROUPS):
        w = jnp.where(causal, ws_ref[gi], 0.0).astype(BF16)
        vg = jnp.concatenate([v[n * c:(n + 1) * c, gi * gd:(gi + 1) * gd] for n in range(nc)], axis=1)
        sg = _dot(w, vg)
        bias = bs_ref[gi]
        cols.append(jnp.concatenate([sg[:, n * gd:(n + 1) * gd] + bias for n in range(nc)], axis=0))
    s = jnp.concatenate(cols, axis=1)
    y_ref[...] = (u * s).astype(y_ref.dtype)


def _sgu(x, gain, w_in, b_in, v_norm, w_s, b_s, tm=512):
    m, d = x.shape
    width = v_norm.shape[0]
    gd = width // SGU_GROUPS
    bs = jnp.broadcast_to(b_s[:, :, None], (SGU_GROUPS, SGU_CHUNK, gd)).astype(F32)
    fixed2 = lambda i: (0, 0)
    fixed3 = lambda i: (0, 0, 0)
    return pl.pallas_call(
        _sgu_body,
        grid=(m // tm,),
        in_specs=[
            pl.BlockSpec((tm, d), lambda i: (i, 0)),
            pl.BlockSpec((1, d), fixed2),
            pl.BlockSpec(w_in.shape, fixed2),
            pl.BlockSpec((1, 2 * width), fixed2),
            pl.BlockSpec((1, width), fixed2),
            pl.BlockSpec(w_s.shape, fixed3),
            pl.BlockSpec(bs.shape, fixed3),
        ],
        out_specs=pl.BlockSpec((tm, width), lambda i: (i, 0)),
        out_shape=jax.ShapeDtypeStruct((m, width), BF16),
        compiler_params=_params("parallel"),
        name="sgu",
    )(x, gain.reshape(1, d), w_in, b_in.reshape(1, 2 * width), v_norm.reshape(1, width),
      w_s, bs)


def _gla_mixer(x, batch, seq, norm1, w_in, gw1, gw2, gb, head_norm):
    q, k, v, og, la = _gla_inproj(x, norm1, w_in, gw1, gw2, gb)
    return _gla_core(q, k, v, og, la, head_norm, batch, seq)


def _diff_mixer(x, batch, seq, norm1, w_in, lq1, lk1, lq2, lk2, head_norm, layer_idx):
    qt, k, vt = _diff_inproj(x, norm1, w_in)
    return _diff_attn(qt, k, vt, lq1, lk1, lq2, lk2, head_norm, batch, seq, layer_idx)


def kernel(x, l0_norm1, l0_w_in, l0_gate_w1, l0_gate_w2, l0_gate_b, l0_head_norm, l0_w_out, l0_norm2, l0_mlp_w1, l0_mlp_w2, l1_norm1, l1_w_in, l1_lambda_q1, l1_lambda_k1, l1_lambda_q2, l1_lambda_k2, l1_head_norm, l1_w_out, l1_norm2, l1_mlp_w1, l1_mlp_w2, l2_norm1, l2_w_in, l2_b_in, l2_v_norm, l2_w_s, l2_b_s, l2_w_out, l2_norm2, l2_mlp_w1, l2_mlp_w2, l3_norm1, l3_w_in, l3_gate_w1, l3_gate_w2, l3_gate_b, l3_head_norm, l3_w_out, l3_norm2, l3_mlp_w1, l3_mlp_w2, final_norm):
    batch, seq, d = x.shape
    h = x.reshape(batch * seq, d)
    o = _gla_mixer(h, batch, seq, l0_norm1, l0_w_in, l0_gate_w1, l0_gate_w2, l0_gate_b, l0_head_norm)
    h = _mlp(h, o, l0_w_out, l0_norm2, l0_mlp_w1, l0_mlp_w2)
    o = _diff_mixer(h, batch, seq, l1_norm1, l1_w_in, l1_lambda_q1, l1_lambda_k1, l1_lambda_q2, l1_lambda_k2,
                    l1_head_norm, 1)
    h = _mlp(h, o, l1_w_out, l1_norm2, l1_mlp_w1, l1_mlp_w2)
    o = _sgu(h, l2_norm1, l2_w_in, l2_b_in, l2_v_norm, l2_w_s, l2_b_s)
    h = _mlp(h, o, l2_w_out, l2_norm2, l2_mlp_w1, l2_mlp_w2)
    o = _gla_mixer(h, batch, seq, l3_norm1, l3_w_in, l3_gate_w1, l3_gate_w2, l3_gate_b, l3_head_norm)
    h = _mlp(h, o, l3_w_out, l3_norm2, l3_mlp_w1, l3_mlp_w2, final_gain=final_norm)
    return h.reshape(batch, seq, d)
```

```python
import functools
import math

import jax
import jax.numpy as jnp
from jax import lax
from jax.experimental import pallas as pl
from jax.experimental.pallas import tpu as pltpu

F32 = jnp.float32
BF16 = jnp.bfloat16

EPS = 1e-6
NEG_INF = -1e30

GLA_HEADS = 4
GLA_RANK = 16
GLA_GATE_NORM = 16.0
GLA_CHUNK = 64

DIFF_HEAD_DIM = 64
DIFF_HEADS = 8

SGU_CHUNK = 128
SGU_GROUPS = 8

LANES = 128
LOG2E = 1.4426950408889634
VMEM_LIMIT = 52 * 1024 * 1024


def _params(*sem):
    return pltpu.CompilerParams(dimension_semantics=sem, vmem_limit_bytes=VMEM_LIMIT)


def _dot(a, b):
    return jnp.dot(a, b, preferred_element_type=F32)


def _dot_nt(a, b):
    return lax.dot_general(a, b, (((1,), (1,)), ((), ())), preferred_element_type=F32)


def _dot_tn(a, b):
    return lax.dot_general(a, b, (((0,), (0,)), ((), ())), preferred_element_type=F32)


def _rms(x, g):
    return x * lax.rsqrt(jnp.mean(x * x, axis=-1, keepdims=True) + EPS) * g


BF16_ROWS = 16


def _cast_specs(weights, steps, step_of):
    in_specs, out_specs, out_shapes = [], [], []
    for w in weights:
        rows, cols = w.shape
        blk = max(BF16_ROWS, rows // steps)
        nblk = rows // blk
        index = lambda *ids, nblk=nblk: (step_of(*ids) * nblk // steps, 0)
        in_specs.append(pl.BlockSpec((blk, cols), index))
        out_specs.append(pl.BlockSpec((blk, cols), index))
        out_shapes.append(jax.ShapeDtypeStruct(w.shape, BF16))
    return in_specs, out_specs, out_shapes


def _cast_blocks(src_refs, dst_refs):
    for src, dst in zip(src_refs, dst_refs):
        dst[...] = src[...].astype(BF16)


INPROJ_GROUPS = 2


def _gla_inproj_body(n_cast, x_ref, g_ref, w_ref, gw1_ref, gw2_ref, gb_ref, *refs):
    q_ref, k_ref, v_ref, og_ref, la_ref = refs[n_cast:n_cast + 5]
    _cast_blocks(refs[:n_cast], refs[n_cast + 5:])
    dk = q_ref.shape[1]
    dv = v_ref.shape[1]
    rows = x_ref.shape[0] // INPROJ_GROUPS
    for r in range(0, x_ref.shape[0], rows):
        sl = slice(r, r + rows)
        h = _rms(x_ref[sl], g_ref[...]).astype(BF16)
        t = _dot(h, gw1_ref[...]).astype(BF16)
        q_ref[sl] = _dot(h, w_ref[:, 0:dk].astype(BF16))
        z = _dot(t, gw2_ref[...]) + gb_ref[...]
        k_ref[sl] = _dot(h, w_ref[:, dk:2 * dk].astype(BF16))
        la_ref[sl] = (jnp.minimum(z, 0.0) - jnp.log(1.0 + jnp.exp(-jnp.abs(z)))) / GLA_GATE_NORM
        v_ref[sl] = _dot(h, w_ref[:, 2 * dk:2 * dk + dv].astype(BF16)).astype(BF16)
        og_ref[sl] = _dot(h, w_ref[:, 2 * dk + dv:2 * dk + 2 * dv].astype(BF16))


def _gla_inproj(x, gain, w_in, gw1, gw2, gb, cast=(), tm=512):
    m, d = x.shape
    cast_in, cast_out, cast_shapes = _cast_specs(cast, m // tm, lambda i: i)
    dk = gw2.shape[1]
    dv = (w_in.shape[1] - 2 * dk) // 2
    gw1p = jnp.zeros((d, LANES), BF16).at[:, :GLA_RANK].set(gw1.astype(BF16))
    gw2p = jnp.zeros((LANES, dk), BF16).at[:GLA_RANK, :].set(gw2.astype(BF16))
    row = lambda i: (i, 0)
    fixed = lambda i: (0, 0)
    return pl.pallas_call(
        functools.partial(_gla_inproj_body, len(cast)),
        grid=(m // tm,),
        in_specs=[
            pl.BlockSpec((tm, d), row),
            pl.BlockSpec((1, d), fixed),
            pl.BlockSpec(w_in.shape, fixed),
            pl.BlockSpec(gw1p.shape, fixed),
            pl.BlockSpec(gw2p.shape, fixed),
            pl.BlockSpec((1, dk), fixed),
        ] + cast_in,
        out_specs=[
            pl.BlockSpec((tm, dk), row),
            pl.BlockSpec((tm, dk), row),
            pl.BlockSpec((tm, dv), row),
            pl.BlockSpec((tm, dv), row),
            pl.BlockSpec((tm, dk), row),
        ] + cast_out,
        out_shape=[
            jax.ShapeDtypeStruct((m, dk), F32),
            jax.ShapeDtypeStruct((m, dk), F32),
            jax.ShapeDtypeStruct((m, dv), BF16),
            jax.ShapeDtypeStruct((m, dv), F32),
            jax.ShapeDtypeStruct((m, dk), F32),
        ] + cast_shapes,
        compiler_params=_params("parallel"),
        name="gla_inproj",
    )(x, gain.reshape(1, d), w_in, gw1p, gw2p, gb.reshape(1, dk), *cast)


def _gla_body(tri_ref, q_ref, k_ref, v_ref, og_ref, la_ref, hn_ref, o_ref, st_ref):
    c = GLA_CHUNK
    tc, hk = q_ref.shape
    n_chunks = tc // c

    @pl.when(pl.program_id(2) == 0)
    def _():
        st_ref[...] = jnp.zeros_like(st_ref)

    tri = tri_ref[...]
    tg = tri.shape[0]
    la = la_ref[...]
    la_hi = la.astype(BF16)
    la_lo = (la - la_hi.astype(F32)).astype(BF16)
    b = jnp.concatenate([_dot(tri, la_hi[r:r + tg]) + _dot(tri, la_lo[r:r + tg]) for r in range(0, tc, tg)],
                        axis=0)
    b_lasts = [b[(n + 1) * c - 1:(n + 1) * c, :] for n in range(n_chunks)]
    b_end = jnp.concatenate([jnp.broadcast_to(bl, (c, hk)) for bl in b_lasts], axis=0)

    q = q_ref[...] * (hk ** -0.5)
    k = k_ref[...]
    q_dec = (q * jnp.exp(b)).astype(BF16)
    k_inv = (k * jnp.exp(-b)).astype(BF16)
    k_end = (k * jnp.exp(b_end - b)).astype(BF16)

    row = lax.broadcasted_iota(jnp.int32, (c, c), 0)
    col = lax.broadcasted_iota(jnp.int32, (c, c), 1)
    causal = col <= row
    chunk = lambda a, n: a[n * c:(n + 1) * c]

    atts = [jnp.where(causal, _dot_nt(chunk(q_dec, n), chunk(k_inv, n)), 0.0).astype(BF16)
            for n in range(n_chunks)]
    kvs = [_dot_tn(v_ref[n * c:(n + 1) * c, :], chunk(k_end, n)) for n in range(n_chunks)]
    intra = [_dot(atts[n], v_ref[n * c:(n + 1) * c, :]) for n in range(n_chunks)]

    st = st_ref[...]
    outs = []
    for n in range(n_chunks):
        outs.append(intra[n] + _dot_nt(chunk(q_dec, n), st.astype(BF16)))
        st = jnp.exp(b_lasts[n]) * st + kvs[n]
    st_ref[...] = st

    o = jnp.concatenate(outs, axis=0)
    o = o * lax.rsqrt(jnp.mean(o * o, axis=-1, keepdims=True) + EPS) * hn_ref[...]
    g = og_ref[...]
    o_ref[...] = (o * (g / (1.0 + jnp.exp(-g)))).astype(o_ref.dtype)


GLA_CUMSUM_ROWS = 256


def _gla_core(q, k, v, og, la, head_norm, batch, seq, tc=2048):
    tc = min(tc, seq)
    m, dk = q.shape
    dv = v.shape[1]
    hk = dk // GLA_HEADS
    hv = dv // GLA_HEADS
    nt = seq // tc
    idx = jnp.arange(GLA_CUMSUM_ROWS)
    tri = ((idx[:, None] // GLA_CHUNK == idx[None, :] // GLA_CHUNK)
           & (idx[None, :] <= idx[:, None])).astype(BF16)
    blk = lambda b, h, t: (b * nt + t, h)
    return pl.pallas_call(
        _gla_body,
        grid=(batch, GLA_HEADS, nt),
        in_specs=[
            pl.BlockSpec(tri.shape, lambda b, h, t: (0, 0)),
            pl.BlockSpec((tc, hk), blk),
            pl.BlockSpec((tc, hk), blk),
            pl.BlockSpec((tc, hv), blk),
            pl.BlockSpec((tc, hv), blk),
            pl.BlockSpec((tc, hk), blk),
            pl.BlockSpec((1, hv), lambda b, h, t: (0, 0)),
        ],
        out_specs=pl.BlockSpec((tc, hv), blk),
        out_shape=jax.ShapeDtypeStruct((m, dv), BF16),
        scratch_shapes=[pltpu.VMEM((hv, hk), F32)],
        compiler_params=_params("parallel", "parallel", "arbitrary"),
        name="gla_core",
    )(tri, q, k, v, og, la, head_norm.reshape(1, hv))


MLP_INIT_GROUPS = 4


def _mlp_body(final, x_ref, o_ref, wo_ref, g_ref, w1_ref, w2_ref, gf_ref, y_ref, x1_scr, h_scr, acc_scr):
    f = pl.program_id(1)

    @pl.when(f == 0)
    def _():
        acc_scr[...] = jnp.zeros_like(acc_scr)
        rows = x_ref.shape[0] // MLP_INIT_GROUPS
        wo = wo_ref[...].astype(BF16)
        for r in range(0, x_ref.shape[0], rows):
            x1 = x_ref[r:r + rows] + _dot(o_ref[r:r + rows], wo)
            x1_scr[r:r + rows] = x1
            h_scr[r:r + rows] = _rms(x1, g_ref[...]).astype(BF16)

    a = jnp.maximum(_dot(h_scr[...], w1_ref[...]), 0.0)
    acc_scr[...] += _dot((a * a).astype(BF16), w2_ref[...])

    @pl.when(f == pl.num_programs(1) - 1)
    def _():
        y = x1_scr[...] + acc_scr[...]
        if final:
            y = _rms(y, gf_ref[...])
        y_ref[...] = y


def _mlp(x, o, w_out, gain, w1, w2, final_gain=None, tm=1024, tf=1024):
    m, d = x.shape
    kdim = o.shape[1]
    dff = w1.shape[1]
    final = final_gain is not None
    gf = (final_gain if final else gain).reshape(1, d)
    return pl.pallas_call(
        functools.partial(_mlp_body, final),
        grid=(m // tm, dff // tf),
        in_specs=[
            pl.BlockSpec((tm, d), lambda i, f: (i, 0)),
            pl.BlockSpec((tm, kdim), lambda i, f: (i, 0)),
            pl.BlockSpec((kdim, d), lambda i, f: (0, 0)),
            pl.BlockSpec((1, d), lambda i, f: (0, 0)),
            pl.BlockSpec((d, tf), lambda i, f: (0, f)),
            pl.BlockSpec((tf, d), lambda i, f: (f, 0)),
            pl.BlockSpec((1, d), lambda i, f: (0, 0)),
        ],
        out_specs=pl.BlockSpec((tm, d), lambda i, f: (i, 0)),
        out_shape=jax.ShapeDtypeStruct((m, d), F32),
        scratch_shapes=[pltpu.VMEM((tm, d), F32), pltpu.VMEM((tm, d), BF16), pltpu.VMEM((tm, d), F32)],
        compiler_params=_params("parallel", "arbitrary"),
        name="mlp",
    )(x, o, w_out, gain.reshape(1, d), w1, w2, gf)


Q_BLOCK = 256
KEY_BLOCK = 512


def _diff_inproj_body(x_ref, g_ref, wqt_ref, wk_ref, wvt_ref, qt_ref, k_ref, vt_ref):
    tq = qt_ref.shape[2]
    tk = vt_ref.shape[2]
    for r in range(0, x_ref.shape[0], tq):
        h = _rms(x_ref[r:r + tq], g_ref[...]).astype(BF16)
        k_ref[r:r + tq] = _dot(h, wk_ref[...]).astype(BF16)
        qt = _dot_nt(wqt_ref[...], h) * (LOG2E * DIFF_HEAD_DIM ** -0.5)
        qt_ref[r // tq] = qt.astype(BF16)
        vt_ref[r // tk, :, r % tk:r % tk + tq] = _dot_nt(wvt_ref[...], h).astype(BF16)


def _diff_inproj(x, gain, w_in, tm=512):
    m, d = x.shape
    dq = 2 * DIFF_HEADS * DIFF_HEAD_DIM
    wqt = w_in[:, :dq].T.astype(BF16)
    wk = w_in[:, dq:2 * dq].astype(BF16)
    wvt = w_in[:, 2 * dq:].T.astype(BF16)
    dv = wvt.shape[0]
    tq, tk = Q_BLOCK, KEY_BLOCK
    fixed = lambda i: (0, 0)
    return pl.pallas_call(
        _diff_inproj_body,
        grid=(m // tm,),
        in_specs=[
            pl.BlockSpec((tm, d), lambda i: (i, 0)),
            pl.BlockSpec((1, d), fixed),
            pl.BlockSpec(wqt.shape, fixed),
            pl.BlockSpec(wk.shape, fixed),
            pl.BlockSpec(wvt.shape, fixed),
        ],
        out_specs=[
            pl.BlockSpec((tm // tq, dq, tq), lambda i: (i, 0, 0)),
            pl.BlockSpec((tm, dq), lambda i: (i, 0)),
            pl.BlockSpec((tm // tk, dv, tk), lambda i: (i, 0, 0)),
        ],
        out_shape=[
            jax.ShapeDtypeStruct((m // tq, dq, tq), BF16),
            jax.ShapeDtypeStruct((m, dq), BF16),
            jax.ShapeDtypeStruct((m // tk, dv, tk), BF16),
        ],
        compiler_params=_params("parallel"),
        name="diff_inproj",
    )(x, gain.reshape(1, d), wqt, wk, wvt)


ATT_HEADS_PER_STEP = 4
KEY_SPLIT = 32
ONES_ROWS = 16
FEAT_ROWS = 8


def _bf16_pieces(c):
    c1 = c.astype(BF16).astype(F32)
    c2 = (c - c1).astype(BF16).astype(F32)
    c3 = (c - c1 - c2).astype(BF16).astype(F32)
    return c1, c2, c3


def _diff_attn_body(lam_init, n_cast, slopes_ref, qt_ref, k_ref, vt_ref, lq1_ref, lk1_ref, lq2_ref, lk2_ref,
                    hn_ref, *refs):
    o_ref = refs[n_cast]
    m_scr, acc_scr, s_scr, mx_scr = refs[2 * n_cast + 1:]
    tq, tk = Q_BLOCK, KEY_BLOCK
    hd = 2 * DIFF_HEAD_DIM
    nh = ATT_HEADS_PER_STEP
    qi = pl.program_id(2)
    hg = pl.program_id(1)
    n_full = qi // (tk // tq)
    odd = qi % (tk // tq)

    def mask(t):
        key = lax.broadcasted_iota(jnp.int32, (tk, 2 * tq), 0)
        qry = lax.broadcasted_iota(jnp.int32, (tk, 2 * tq), 1) % tq
        return jnp.where((key - qry) <= odd * tq, t, NEG_INF)

    kj = lax.broadcasted_iota(jnp.int32, (tk, LANES), 0)
    kl = lax.broadcasted_iota(jnp.int32, (tk, LANES), 1)
    kfeat = jnp.where(kl < 3, kj // KEY_SPLIT, jnp.where(kl < 6, kj % KEY_SPLIT, 0)).astype(F32).astype(BF16)
    ones = jnp.ones((ONES_ROWS, tk), BF16)
    half = lax.broadcasted_iota(jnp.int32, (hd, tq), 0) < DIFF_HEAD_DIM
    frow = lax.broadcasted_iota(jnp.int32, (FEAT_ROWS, 2 * tq), 0)

    heads = []
    for j in range(nh):
        slope = slopes_ref[hg * nh + j] * LOG2E
        qt = qt_ref[0, j * hd:(j + 1) * hd, :]
        zero = jnp.zeros_like(qt)
        q2t = jnp.concatenate([jnp.where(half, qt, zero), jnp.where(half, zero, qt)], axis=1)
        c1, c2, c3 = _bf16_pieces(jnp.full(frow.shape, slope, F32))
        s = float(KEY_SPLIT)
        qfeat = jnp.where(frow == 0, s * c1, jnp.where(frow == 1, s * c2, jnp.where(frow == 2, s * c3,
                jnp.where(frow == 3, c1, jnp.where(frow == 4, c2, jnp.where(frow == 5, c3, 0.0))))))
        qfeat = jnp.concatenate([qfeat, jnp.zeros((hd - FEAT_ROWS, 2 * tq), F32)], axis=0).astype(BF16)
        heads.append((slope, jnp.concatenate([q2t, qfeat], axis=0)))

    m_scr[...] = jnp.full_like(m_scr, NEG_INF)
    acc_scr[...] = jnp.zeros_like(acc_scr)

    def scores(j, kb):
        k_blk = k_ref[pl.ds(pl.multiple_of(kb * tk, tk), tk), j * hd:(j + 1) * hd]
        return _dot(jnp.concatenate([k_blk, kfeat], axis=1), heads[j][1])

    def stage(j, t):
        s_scr[j] = t
        mx_scr[j] = jnp.max(t, axis=0, keepdims=True)

    def consume(j, kb):
        shift = heads[j][0] * (kb * tk - qi * tq).astype(F32)
        m_old = m_scr[j]
        m_new = jnp.maximum(m_old, mx_scr[j] + shift)
        alpha = jnp.exp2(m_old - m_new)
        p = jnp.exp2(s_scr[j] - (m_new - shift)).astype(BF16)
        vt_aug = jnp.concatenate([vt_ref[kb, j * hd:(j + 1) * hd, :], ones], axis=0)
        acc_scr[j] = alpha * acc_scr[j] + _dot(vt_aug, p)
        m_scr[j] = m_new

    def pipelined(kb, masked):
        for j in range(nh):
            t_next = scores(j, kb + 1)
            if masked:
                t_next = mask(t_next)
            consume(j, kb)
            stage(j, t_next)

    @pl.when(n_full == 0)
    def _():
        for j in range(nh):
            stage(j, mask(scores(j, 0)))

    @pl.when(n_full > 0)
    def _():
        for j in range(nh):
            stage(j, scores(j, 0))

    def full_tile(kb, carry):
        pipelined(kb, False)
        return carry

    lax.fori_loop(0, n_full - 1, full_tile, 0)

    @pl.when(n_full > 0)
    def _():
        pipelined(n_full - 1, True)

    _cast_blocks(refs[:n_cast], refs[n_cast + 1:2 * n_cast + 1])
    for j in range(nh):
        consume(j, n_full)

    lam = (jnp.exp(jnp.sum(lq1_ref[...] * lk1_ref[...], keepdims=True))
           - jnp.exp(jnp.sum(lq2_ref[...] * lk2_ref[...], keepdims=True)) + lam_init)
    for j in range(nh):
        acc = acc_scr[j]
        o = acc[:hd] / acc[hd:hd + 1]
        ot = o[:, :tq] - lam * o[:, tq:]
        ot = ot * lax.rsqrt(jnp.mean(ot * ot, axis=0, keepdims=True) + EPS)
        ot = ot * hn_ref[...] * (1.0 - lam_init)
        o_ref[:, j * hd:(j + 1) * hd] = ot.T.astype(o_ref.dtype)


def _diff_attn(qt, k, vt, lq1, lk1, lq2, lk2, head_norm, batch, seq, layer_idx, cast=()):
    tq, tk = Q_BLOCK, KEY_BLOCK
    nh = ATT_HEADS_PER_STEP
    m, dq = k.shape
    hd = 2 * DIFF_HEAD_DIM
    nq = seq // tq
    ng = DIFF_HEADS // nh
    cast_in, cast_out, cast_shapes = _cast_specs(cast, batch * ng * nq, lambda b, h, i: (b * ng + h) * nq + i)
    lam_init = 0.8 - 0.6 * math.exp(-0.3 * layer_idx)
    slopes = 2.0 ** (-8.0 * jnp.arange(1, DIFF_HEADS + 1, dtype=F32) / DIFF_HEADS)
    vec = lambda a: a.reshape(1, DIFF_HEAD_DIM).astype(F32)
    lam_spec = pl.BlockSpec((1, DIFF_HEAD_DIM), lambda b, h, i: (0, 0))
    return pl.pallas_call(
        functools.partial(_diff_attn_body, lam_init, len(cast)),
        grid=(batch, ng, nq),
        in_specs=[
            pl.BlockSpec(memory_space=pltpu.SMEM),
            pl.BlockSpec((1, nh * hd, tq), lambda b, h, i: (b * nq + i, h, 0)),
            pl.BlockSpec((seq, nh * hd), lambda b, h, i: (b, h)),
            pl.BlockSpec((seq // tk, nh * hd, tk), lambda b, h, i: (b, h, 0)),
            lam_spec, lam_spec, lam_spec, lam_spec,
            pl.BlockSpec((hd, 1), lambda b, h, i: (0, 0)),
        ] + cast_in,
        out_specs=[pl.BlockSpec((tq, nh * hd), lambda b, h, i: (b * nq + i, h))] + cast_out,
        out_shape=[jax.ShapeDtypeStruct((m, dq), BF16)] + cast_shapes,
        scratch_shapes=[
            pltpu.VMEM((nh, 1, 2 * tq), F32),
            pltpu.VMEM((nh, hd + ONES_ROWS, 2 * tq), F32),
            pltpu.VMEM((nh, tk, 2 * tq), F32),
            pltpu.VMEM((nh, 1, 2 * tq), F32),
        ],
        compiler_params=_params("parallel", "parallel", "arbitrary"),
        name="diff_attn",
    )(slopes, qt, k, vt, vec(lq1), vec(lk1), vec(lq2), vec(lk2), head_norm.reshape(hd, 1).astype(F32), *cast)


def _sgu_body(x_ref, g_ref, win_ref, bin_ref, vn_ref, ws_ref, bs_ref, y_ref):
    tm = x_ref.shape[0]
    width = vn_ref.shape[1]
    gd = width // SGU_GROUPS
    c = SGU_CHUNK
    h = _rms(x_ref[...], g_ref[...]).astype(BF16)
    uv = _dot(h, win_ref[...].astype(BF16)) + bin_ref[...]
    uv = 0.5 * uv * (1.0 + jnp.tanh(math.sqrt(2.0 / math.pi) * (uv + 0.044715 * (uv * uv * uv))))
    u = uv[:, :width]
    v = _rms(uv[:, width:], vn_ref[...]).astype(BF16)
    row = lax.broadcasted_iota(jnp.int32, (c, c), 0)
    col = lax.broadcasted_iota(jnp.int32, (c, c), 1)
    causal = col <= row
    nc = tm // c
    cols = []
    for gi in range(SGU_GROUPS):
        w = jnp.where(causal, ws_ref[gi], 0.0).astype(BF16)
        vg = jnp.concatenate([v[n * c:(n + 1) * c, gi * gd:(gi + 1) * gd] for n in range(nc)], axis=1)
        sg = _dot(w, vg)
        bias = bs_ref[gi]
        cols.append(jnp.concatenate([sg[:, n * gd:(n + 1) * gd] + bias for n in range(nc)], axis=0))
    s = jnp.concatenate(cols, axis=1)
    y_ref[...] = (u * s).astype(y_ref.dtype)


def _sgu(x, gain, w_in, b_in, v_norm, w_s, b_s, tm=512):
    m, d = x.shape
    width = v_norm.shape[0]
    gd = width // SGU_GROUPS
    bs = jnp.broadcast_to(b_s[:, :, None], (SGU_GROUPS, SGU_CHUNK, gd)).astype(F32)
    fixed2 = lambda i: (0, 0)
    fixed3 = lambda i: (0, 0, 0)
    return pl.pallas_call(
        _sgu_body,
        grid=(m // tm,),
        in_specs=[
            pl.BlockSpec((tm, d), lambda i: (i, 0)),
            pl.BlockSpec((1, d), fixed2),
            pl.BlockSpec(w_in.shape, fixed2),
            pl.BlockSpec((1, 2 * width), fixed2),
            pl.BlockSpec((1, width), fixed2),
            pl.BlockSpec(w_s.shape, fixed3),
            pl.BlockSpec(bs.shape, fixed3),
        ],
        out_specs=pl.BlockSpec((tm, width), lambda i: (i, 0)),
        out_shape=jax.ShapeDtypeStruct((m, width), BF16),
        compiler_params=_params("parallel"),
        name="sgu",
    )(x, gain.reshape(1, d), w_in, b_in.reshape(1, 2 * width), v_norm.reshape(1, width),
      w_s, bs)


def _gla_mixer(x, batch, seq, norm1, w_in, gw1, gw2, gb, head_norm, cast=()):
    q, k, v, og, la, *cast_out = _gla_inproj(x, norm1, w_in, gw1, gw2, gb, cast)
    return _gla_core(q, k, v, og, la, head_norm, batch, seq), cast_out


def _diff_mixer(x, batch, seq, norm1, w_in, lq1, lk1, lq2, lk2, head_norm, layer_idx, cast=()):
    qt, k, vt = _diff_inproj(x, norm1, w_in)
    o, *cast_out = _diff_attn(qt, k, vt, lq1, lk1, lq2, lk2, head_norm, batch, seq, layer_idx, cast)
    return o, cast_out


def kernel(x, l0_norm1, l0_w_in, l0_gate_w1, l0_gate_w2, l0_gate_b, l0_head_norm, l0_w_out, l0_norm2, l0_mlp_w1, l0_mlp_w2, l1_norm1, l1_w_in, l1_lambda_q1, l1_lambda_k1, l1_lambda_q2, l1_lambda_k2, l1_head_norm, l1_w_out, l1_norm2, l1_mlp_w1, l1_mlp_w2, l2_norm1, l2_w_in, l2_b_in, l2_v_norm, l2_w_s, l2_b_s, l2_w_out, l2_norm2, l2_mlp_w1, l2_mlp_w2, l3_norm1, l3_w_in, l3_gate_w1, l3_gate_w2, l3_gate_b, l3_head_norm, l3_w_out, l3_norm2, l3_mlp_w1, l3_mlp_w2, final_norm):
    batch, seq, d = x.shape
    h = x.reshape(batch * seq, d)
    o, (w1, w2) = _gla_mixer(h, batch, seq, l0_norm1, l0_w_in, l0_gate_w1, l0_gate_w2, l0_gate_b, l0_head_norm,
                             cast=(l0_mlp_w1, l0_mlp_w2))
    h = _mlp(h, o, l0_w_out, l0_norm2, w1, w2)
    later = (l1_mlp_w1, l1_mlp_w2, l2_mlp_w1, l2_mlp_w2, l3_mlp_w1, l3_mlp_w2)
    o, later = _diff_mixer(h, batch, seq, l1_norm1, l1_w_in, l1_lambda_q1, l1_lambda_k1, l1_lambda_q2, l1_lambda_k2,
                           l1_head_norm, 1, cast=later)
    h = _mlp(h, o, l1_w_out, l1_norm2, later[0], later[1])
    o = _sgu(h, l2_norm1, l2_w_in, l2_b_in, l2_v_norm, l2_w_s, l2_b_s)
    h = _mlp(h, o, l2_w_out, l2_norm2, later[2], later[3])
    o, _ = _gla_mixer(h, batch, seq, l3_norm1, l3_w_in, l3_gate_w1, l3_gate_w2, l3_gate_b, l3_head_norm)
    h = _mlp(h, o, l3_w_out, l3_norm2, later[4], later[5], final_gain=final_norm)
    return h.reshape(batch, seq, d)
```

```python
import functools
import math

import jax
import jax.numpy as jnp
from jax import lax
from jax.experimental import pallas as pl
from jax.experimental.pallas import tpu as pltpu

F32 = jnp.float32
BF16 = jnp.bfloat16

EPS = 1e-6
NEG_INF = -1e30

GLA_HEADS = 4
GLA_RANK = 16
GLA_GATE_NORM = 16.0
GLA_CHUNK = 64

DIFF_HEAD_DIM = 64
DIFF_HEADS = 8

SGU_CHUNK = 128
SGU_GROUPS = 8

LANES = 128
LOG2E = 1.4426950408889634
VMEM_LIMIT = 52 * 1024 * 1024


def _params(*sem):
    return pltpu.CompilerParams(dimension_semantics=sem, vmem_limit_bytes=VMEM_LIMIT)


def _dot(a, b):
    return jnp.dot(a, b, preferred_element_type=F32)


def _dot_nt(a, b):
    return lax.dot_general(a, b, (((1,), (1,)), ((), ())), preferred_element_type=F32)


def _dot_tn(a, b):
    return lax.dot_general(a, b, (((0,), (0,)), ((), ())), preferred_element_type=F32)


def _rms(x, g):
    return x * lax.rsqrt(jnp.mean(x * x, axis=-1, keepdims=True) + EPS) * g


BF16_ROWS = 16


def _cast_specs(weights, steps, step_of):
    in_specs, out_specs, out_shapes = [], [], []
    for w in weights:
        rows, cols = w.shape
        blk = max(BF16_ROWS, rows // steps)
        nblk = rows // blk
        index = lambda *ids, nblk=nblk: (step_of(*ids) * nblk // steps, 0)
        in_specs.append(pl.BlockSpec((blk, cols), index))
        out_specs.append(pl.BlockSpec((blk, cols), index))
        out_shapes.append(jax.ShapeDtypeStruct(w.shape, BF16))
    return in_specs, out_specs, out_shapes


def _cast_blocks(src_refs, dst_refs):
    for src, dst in zip(src_refs, dst_refs):
        dst[...] = src[...].astype(BF16)


INPROJ_GROUPS = 2


def _gla_inproj_body(n_cast, x_ref, g_ref, w_ref, gw1_ref, gw2_ref, gb_ref, *refs):
    q_ref, k_ref, v_ref, og_ref, la_ref = refs[n_cast:n_cast + 5]
    _cast_blocks(refs[:n_cast], refs[n_cast + 5:])
    dk = q_ref.shape[1]
    dv = v_ref.shape[1]
    rows = x_ref.shape[0] // INPROJ_GROUPS
    for r in range(0, x_ref.shape[0], rows):
        sl = slice(r, r + rows)
        h = _rms(x_ref[sl], g_ref[...]).astype(BF16)
        t = _dot(h, gw1_ref[...]).astype(BF16)
        q_ref[sl] = _dot(h, w_ref[:, 0:dk].astype(BF16))
        z = _dot(t, gw2_ref[...]) + gb_ref[...]
        k_ref[sl] = _dot(h, w_ref[:, dk:2 * dk].astype(BF16))
        la_ref[sl] = (jnp.minimum(z, 0.0) - jnp.log(1.0 + jnp.exp(-jnp.abs(z)))) / GLA_GATE_NORM
        v_ref[sl] = _dot(h, w_ref[:, 2 * dk:2 * dk + dv].astype(BF16)).astype(BF16)
        og_ref[sl] = _dot(h, w_ref[:, 2 * dk + dv:2 * dk + 2 * dv].astype(BF16))


def _gla_inproj(x, gain, w_in, gw1, gw2, gb, cast=(), tm=512):
    m, d = x.shape
    cast_in, cast_out, cast_shapes = _cast_specs(cast, m // tm, lambda i: i)
    dk = gw2.shape[1]
    dv = (w_in.shape[1] - 2 * dk) // 2
    gw1p = jnp.zeros((d, LANES), BF16).at[:, :GLA_RANK].set(gw1.astype(BF16))
    gw2p = jnp.zeros((LANES, dk), BF16).at[:GLA_RANK, :].set(gw2.astype(BF16))
    row = lambda i: (i, 0)
    fixed = lambda i: (0, 0)
    return pl.pallas_call(
        functools.partial(_gla_inproj_body, len(cast)),
        grid=(m // tm,),
        in_specs=[
            pl.BlockSpec((tm, d), row),
            pl.BlockSpec((1, d), fixed),
            pl.BlockSpec(w_in.shape, fixed),
            pl.BlockSpec(gw1p.shape, fixed),
            pl.BlockSpec(gw2p.shape, fixed),
            pl.BlockSpec((1, dk), fixed),
        ] + cast_in,
        out_specs=[
            pl.BlockSpec((tm, dk), row),
            pl.BlockSpec((tm, dk), row),
            pl.BlockSpec((tm, dv), row),
            pl.BlockSpec((tm, dv), row),
            pl.BlockSpec((tm, dk), row),
        ] + cast_out,
        out_shape=[
            jax.ShapeDtypeStruct((m, dk), F32),
            jax.ShapeDtypeStruct((m, dk), F32),
            jax.ShapeDtypeStruct((m, dv), BF16),
            jax.ShapeDtypeStruct((m, dv), F32),
            jax.ShapeDtypeStruct((m, dk), F32),
        ] + cast_shapes,
        compiler_params=_params("parallel"),
        name="gla_inproj",
    )(x, gain.reshape(1, d), w_in, gw1p, gw2p, gb.reshape(1, dk), *cast)


def _gla_body(tri_ref, q_ref, k_ref, v_ref, og_ref, la_ref, hn_ref, o_ref, st_ref):
    c = GLA_CHUNK
    tc, hk = q_ref.shape
    n_chunks = tc // c

    @pl.when(pl.program_id(2) == 0)
    def _():
        st_ref[...] = jnp.zeros_like(st_ref)

    tri = tri_ref[...]
    tg = tri.shape[0]
    la = la_ref[...]
    la_hi = la.astype(BF16)
    la_lo = (la - la_hi.astype(F32)).astype(BF16)
    b = jnp.concatenate([_dot(tri, la_hi[r:r + tg]) + _dot(tri, la_lo[r:r + tg]) for r in range(0, tc, tg)],
                        axis=0)
    b_lasts = [b[(n + 1) * c - 1:(n + 1) * c, :] for n in range(n_chunks)]
    b_end = jnp.concatenate([jnp.broadcast_to(bl, (c, hk)) for bl in b_lasts], axis=0)

    q = q_ref[...] * (hk ** -0.5)
    k = k_ref[...]
    q_dec = (q * jnp.exp(b)).astype(BF16)
    k_inv = (k * jnp.exp(-b)).astype(BF16)
    k_end = (k * jnp.exp(b_end - b)).astype(BF16)

    row = lax.broadcasted_iota(jnp.int32, (c, c), 0)
    col = lax.broadcasted_iota(jnp.int32, (c, c), 1)
    causal = col <= row
    chunk = lambda a, n: a[n * c:(n + 1) * c]

    atts = [jnp.where(causal, _dot_nt(chunk(q_dec, n), chunk(k_inv, n)), 0.0).astype(BF16)
            for n in range(n_chunks)]
    kvs = [_dot_tn(v_ref[n * c:(n + 1) * c, :], chunk(k_end, n)) for n in range(n_chunks)]
    intra = [_dot(atts[n], v_ref[n * c:(n + 1) * c, :]) for n in range(n_chunks)]

    st = st_ref[...]
    outs = []
    for n in range(n_chunks):
        outs.append(intra[n] + _dot_nt(chunk(q_dec, n), st.astype(BF16)))
        st = jnp.exp(b_lasts[n]) * st + kvs[n]
    st_ref[...] = st

    o = jnp.concatenate(outs, axis=0)
    o = o * lax.rsqrt(jnp.mean(o * o, axis=-1, keepdims=True) + EPS) * hn_ref[...]
    g = og_ref[...]
    o_ref[...] = (o * (g / (1.0 + jnp.exp(-g)))).astype(o_ref.dtype)


GLA_CUMSUM_ROWS = 256


def _gla_core(q, k, v, og, la, head_norm, batch, seq, tc=2048):
    tc = min(tc, seq)
    m, dk = q.shape
    dv = v.shape[1]
    hk = dk // GLA_HEADS
    hv = dv // GLA_HEADS
    nt = seq // tc
    idx = jnp.arange(GLA_CUMSUM_ROWS)
    tri = ((idx[:, None] // GLA_CHUNK == idx[None, :] // GLA_CHUNK)
           & (idx[None, :] <= idx[:, None])).astype(BF16)
    blk = lambda b, h, t: (b * nt + t, h)
    return pl.pallas_call(
        _gla_body,
        grid=(batch, GLA_HEADS, nt),
        in_specs=[
            pl.BlockSpec(tri.shape, lambda b, h, t: (0, 0)),
            pl.BlockSpec((tc, hk), blk),
            pl.BlockSpec((tc, hk), blk),
            pl.BlockSpec((tc, hv), blk),
            pl.BlockSpec((tc, hv), blk),
            pl.BlockSpec((tc, hk), blk),
            pl.BlockSpec((1, hv), lambda b, h, t: (0, 0)),
        ],
        out_specs=pl.BlockSpec((tc, hv), blk),
        out_shape=jax.ShapeDtypeStruct((m, dv), BF16),
        scratch_shapes=[pltpu.VMEM((hv, hk), F32)],
        compiler_params=_params("parallel", "parallel", "arbitrary"),
        name="gla_core",
    )(tri, q, k, v, og, la, head_norm.reshape(1, hv))


MLP_INIT_GROUPS = 4


def _mlp_body(final, x_ref, o_ref, wo_ref, g_ref, w1_ref, w2_ref, gf_ref, y_ref, x1_scr, h_scr, acc_scr):
    f = pl.program_id(1)

    @pl.when(f == 0)
    def _():
        acc_scr[...] = jnp.zeros_like(acc_scr)
        rows = x_ref.shape[0] // MLP_INIT_GROUPS
        wo = wo_ref[...].astype(BF16)
        for r in range(0, x_ref.shape[0], rows):
            x1 = x_ref[r:r + rows] + _dot(o_ref[r:r + rows], wo)
            x1_scr[r:r + rows] = x1
            h_scr[r:r + rows] = _rms(x1, g_ref[...]).astype(BF16)

    a = jnp.maximum(_dot(h_scr[...], w1_ref[...]), 0.0)
    acc_scr[...] += _dot((a * a).astype(BF16), w2_ref[...])

    @pl.when(f == pl.num_programs(1) - 1)
    def _():
        y = x1_scr[...] + acc_scr[...]
        if final:
            y = _rms(y, gf_ref[...])
        y_ref[...] = y


def _mlp(x, o, w_out, gain, w1, w2, final_gain=None, tm=1024, tf=1024):
    m, d = x.shape
    kdim = o.shape[1]
    dff = w1.shape[1]
    final = final_gain is not None
    gf = (final_gain if final else gain).reshape(1, d)
    return pl.pallas_call(
        functools.partial(_mlp_body, final),
        grid=(m // tm, dff // tf),
        in_specs=[
            pl.BlockSpec((tm, d), lambda i, f: (i, 0)),
            pl.BlockSpec((tm, kdim), lambda i, f: (i, 0)),
            pl.BlockSpec((kdim, d), lambda i, f: (0, 0)),
            pl.BlockSpec((1, d), lambda i, f: (0, 0)),
            pl.BlockSpec((d, tf), lambda i, f: (0, f)),
            pl.BlockSpec((tf, d), lambda i, f: (f, 0)),
            pl.BlockSpec((1, d), lambda i, f: (0, 0)),
        ],
        out_specs=pl.BlockSpec((tm, d), lambda i, f: (i, 0)),
        out_shape=jax.ShapeDtypeStruct((m, d), F32),
        scratch_shapes=[pltpu.VMEM((tm, d), F32), pltpu.VMEM((tm, d), BF16), pltpu.VMEM((tm, d), F32)],
        compiler_params=_params("parallel", "arbitrary"),
        name="mlp",
    )(x, o, w_out, gain.reshape(1, d), w1, w2, gf)


Q_BLOCK = 256
KEY_BLOCK = 512


def _diff_inproj_body(x_ref, g_ref, wqt_ref, wk_ref, wvt_ref, qt_ref, k_ref, vt_ref):
    tq = qt_ref.shape[2]
    tk = vt_ref.shape[2]
    for r in range(0, x_ref.shape[0], tq):
        h = _rms(x_ref[r:r + tq], g_ref[...]).astype(BF16)
        k_ref[r:r + tq] = _dot(h, wk_ref[...]).astype(BF16)
        qt = _dot_nt(wqt_ref[...], h) * (LOG2E * DIFF_HEAD_DIM ** -0.5)
        qt_ref[r // tq] = qt.astype(BF16)
        vt_ref[r // tk, :, r % tk:r % tk + tq] = _dot_nt(wvt_ref[...], h).astype(BF16)


def _diff_inproj(x, gain, w_in, tm=512):
    m, d = x.shape
    dq = 2 * DIFF_HEADS * DIFF_HEAD_DIM
    wqt = w_in[:, :dq].T.astype(BF16)
    wk = w_in[:, dq:2 * dq].astype(BF16)
    wvt = w_in[:, 2 * dq:].T.astype(BF16)
    dv = wvt.shape[0]
    tq, tk = Q_BLOCK, KEY_BLOCK
    fixed = lambda i: (0, 0)
    return pl.pallas_call(
        _diff_inproj_body,
        grid=(m // tm,),
        in_specs=[
            pl.BlockSpec((tm, d), lambda i: (i, 0)),
            pl.BlockSpec((1, d), fixed),
            pl.BlockSpec(wqt.shape, fixed),
            pl.BlockSpec(wk.shape, fixed),
            pl.BlockSpec(wvt.shape, fixed),
        ],
        out_specs=[
            pl.BlockSpec((tm // tq, dq, tq), lambda i: (i, 0, 0)),
            pl.BlockSpec((tm, dq), lambda i: (i, 0)),
            pl.BlockSpec((tm // tk, dv, tk), lambda i: (i, 0, 0)),
        ],
        out_shape=[
            jax.ShapeDtypeStruct((m // tq, dq, tq), BF16),
            jax.ShapeDtypeStruct((m, dq), BF16),
            jax.ShapeDtypeStruct((m // tk, dv, tk), BF16),
        ],
        compiler_params=_params("parallel"),
        name="diff_inproj",
    )(x, gain.reshape(1, d), wqt, wk, wvt)


ATT_HEADS_PER_STEP = 4
KEY_SPLIT = 32
ONES_ROWS = 16
FEAT_ROWS = 8


def _bf16_pieces(c):
    c1 = c.astype(BF16).astype(F32)
    c2 = (c - c1).astype(BF16).astype(F32)
    c3 = (c - c1 - c2).astype(BF16).astype(F32)
    return c1, c2, c3


def _diff_attn_body(lam_init, n_cast, slopes_ref, qt_ref, k_ref, vt_ref, lq1_ref, lk1_ref, lq2_ref, lk2_ref,
                    hn_ref, *refs):
    o_ref = refs[n_cast]
    m_scr, acc_scr, s_scr, mx_scr = refs[2 * n_cast + 1:]
    tq, tk = Q_BLOCK, KEY_BLOCK
    hd = 2 * DIFF_HEAD_DIM
    nh = ATT_HEADS_PER_STEP
    qi = pl.program_id(2)
    hg = pl.program_id(1)
    n_full = qi // (tk // tq)
    odd = qi % (tk // tq)

    def mask(t):
        key = lax.broadcasted_iota(jnp.int32, (tk, 2 * tq), 0)
        qry = lax.broadcasted_iota(jnp.int32, (tk, 2 * tq), 1) % tq
        return jnp.where((key - qry) <= odd * tq, t, NEG_INF)

    kj = lax.broadcasted_iota(jnp.int32, (tk, LANES), 0)
    kl = lax.broadcasted_iota(jnp.int32, (tk, LANES), 1)
    kfeat = jnp.where(kl < 3, kj // KEY_SPLIT, jnp.where(kl < 6, kj % KEY_SPLIT, 0)).astype(F32).astype(BF16)
    ones = jnp.ones((ONES_ROWS, tk), BF16)
    half = lax.broadcasted_iota(jnp.int32, (hd, tq), 0) < DIFF_HEAD_DIM
    frow = lax.broadcasted_iota(jnp.int32, (FEAT_ROWS, 2 * tq), 0)

    heads = []
    for j in range(nh):
        slope = slopes_ref[hg * nh + j] * LOG2E
        qt = qt_ref[0, j * hd:(j + 1) * hd, :]
        zero = jnp.zeros_like(qt)
        q2t = jnp.concatenate([jnp.where(half, qt, zero), jnp.where(half, zero, qt)], axis=1)
        c1, c2, c3 = _bf16_pieces(jnp.full(frow.shape, slope, F32))
        s = float(KEY_SPLIT)
        qfeat = jnp.where(frow == 0, s * c1, jnp.where(frow == 1, s * c2, jnp.where(frow == 2, s * c3,
                jnp.where(frow == 3, c1, jnp.where(frow == 4, c2, jnp.where(frow == 5, c3, 0.0))))))
        qfeat = jnp.concatenate([qfeat, jnp.zeros((hd - FEAT_ROWS, 2 * tq), F32)], axis=0).astype(BF16)
        heads.append((slope, jnp.concatenate([q2t, qfeat], axis=0)))

    m_scr[...] = jnp.full_like(m_scr, NEG_INF)
    acc_scr[...] = jnp.zeros_like(acc_scr)

    def scores(j, kb):
        k_blk = k_ref[pl.ds(pl.multiple_of(kb * tk, tk), tk), j * hd:(j + 1) * hd]
        return _dot(jnp.concatenate([k_blk, kfeat], axis=1), heads[j][1])

    def stage(j, t):
        s_scr[j] = t
        mx_scr[j] = jnp.max(t, axis=0, keepdims=True)

    def consume(j, kb):
        shift = heads[j][0] * (kb * tk - qi * tq).astype(F32)
        m_old = m_scr[j]
        m_new = jnp.maximum(m_old, mx_scr[j] + shift)
        alpha = jnp.exp2(m_old - m_new)
        p = jnp.exp2(s_scr[j] - (m_new - shift)).astype(BF16)
        vt_aug = jnp.concatenate([vt_ref[kb, j * hd:(j + 1) * hd, :], ones], axis=0)
        acc_scr[j] = alpha * acc_scr[j] + _dot(vt_aug, p)
        m_scr[j] = m_new

    def pipelined(kb, masked):
        for j in range(nh):
            t_next = scores(j, kb + 1)
            if masked:
                t_next = mask(t_next)
            consume(j, kb)
            stage(j, t_next)

    @pl.when(n_full == 0)
    def _():
        for j in range(nh):
            stage(j, mask(scores(j, 0)))

    @pl.when(n_full > 0)
    def _():
        for j in range(nh):
            stage(j, scores(j, 0))

    n_plain = jnp.maximum(n_full - 1, 0)

    def two_tiles(i, carry):
        pipelined(2 * i, False)
        pipelined(2 * i + 1, False)
        return carry

    lax.fori_loop(0, n_plain // 2, two_tiles, 0)

    @pl.when(n_plain % 2 == 1)
    def _():
        pipelined(n_plain - 1, False)

    @pl.when(n_full > 0)
    def _():
        pipelined(n_full - 1, True)

    _cast_blocks(refs[:n_cast], refs[n_cast + 1:2 * n_cast + 1])
    for j in range(nh):
        consume(j, n_full)

    lam = (jnp.exp(jnp.sum(lq1_ref[...] * lk1_ref[...], keepdims=True))
           - jnp.exp(jnp.sum(lq2_ref[...] * lk2_ref[...], keepdims=True)) + lam_init)
    for j in range(nh):
        acc = acc_scr[j]
        o = acc[:hd] / acc[hd:hd + 1]
        ot = o[:, :tq] - lam * o[:, tq:]
        ot = ot * lax.rsqrt(jnp.mean(ot * ot, axis=0, keepdims=True) + EPS)
        ot = ot * hn_ref[...] * (1.0 - lam_init)
        o_ref[:, j * hd:(j + 1) * hd] = ot.T.astype(o_ref.dtype)


def _diff_attn(qt, k, vt, lq1, lk1, lq2, lk2, head_norm, batch, seq, layer_idx, cast=()):
    tq, tk = Q_BLOCK, KEY_BLOCK
    nh = ATT_HEADS_PER_STEP
    m, dq = k.shape
    hd = 2 * DIFF_HEAD_DIM
    nq = seq // tq
    ng = DIFF_HEADS // nh
    cast_in, cast_out, cast_shapes = _cast_specs(cast, batch * ng * nq, lambda b, h, i: (b * ng + h) * nq + i)
    lam_init = 0.8 - 0.6 * math.exp(-0.3 * layer_idx)
    slopes = 2.0 ** (-8.0 * jnp.arange(1, DIFF_HEADS + 1, dtype=F32) / DIFF_HEADS)
    vec = lambda a: a.reshape(1, DIFF_HEAD_DIM).astype(F32)
    lam_spec = pl.BlockSpec((1, DIFF_HEAD_DIM), lambda b, h, i: (0, 0))
    return pl.pallas_call(
        functools.partial(_diff_attn_body, lam_init, len(cast)),
        grid=(batch, ng, nq),
        in_specs=[
            pl.BlockSpec(memory_space=pltpu.SMEM),
            pl.BlockSpec((1, nh * hd, tq), lambda b, h, i: (b * nq + i, h, 0)),
            pl.BlockSpec((seq, nh * hd), lambda b, h, i: (b, h)),
            pl.BlockSpec((seq // tk, nh * hd, tk), lambda b, h, i: (b, h, 0)),
            lam_spec, lam_spec, lam_spec, lam_spec,
            pl.BlockSpec((hd, 1), lambda b, h, i: (0, 0)),
        ] + cast_in,
        out_specs=[pl.BlockSpec((tq, nh * hd), lambda b, h, i: (b * nq + i, h))] + cast_out,
        out_shape=[jax.ShapeDtypeStruct((m, dq), BF16)] + cast_shapes,
        scratch_shapes=[
            pltpu.VMEM((nh, 1, 2 * tq), F32),
            pltpu.VMEM((nh, hd + ONES_ROWS, 2 * tq), F32),
            pltpu.VMEM((nh, tk, 2 * tq), F32),
            pltpu.VMEM((nh, 1, 2 * tq), F32),
        ],
        compiler_params=_params("parallel", "parallel", "arbitrary"),
        name="diff_attn",
    )(slopes, qt, k, vt, vec(lq1), vec(lk1), vec(lq2), vec(lk2), head_norm.reshape(hd, 1).astype(F32), *cast)


SGU_ROWS = 256


def _sgu_body(x_ref, g_ref, win_ref, bin_ref, vn_ref, ws_ref, bs_ref, y_ref):
    tm = x_ref.shape[0]
    width = vn_ref.shape[1]
    gd = width // SGU_GROUPS
    c = SGU_CHUNK
    nc = SGU_ROWS // c
    row = lax.broadcasted_iota(jnp.int32, (c, c), 0)
    col = lax.broadcasted_iota(jnp.int32, (c, c), 1)
    ws = [jnp.where(col <= row, ws_ref[gi], 0.0).astype(BF16) for gi in range(SGU_GROUPS)]
    w_in = win_ref[...].astype(BF16)
    uvs = [_dot(_rms(x_ref[r:r + SGU_ROWS], g_ref[...]).astype(BF16), w_in) + bin_ref[...]
           for r in range(0, tm, SGU_ROWS)]
    for i, uv in enumerate(uvs):
        uv = 0.5 * uv * (1.0 + jnp.tanh(math.sqrt(2.0 / math.pi) * (uv + 0.044715 * (uv * uv * uv))))
        u = uv[:, :width]
        v = _rms(uv[:, width:], vn_ref[...]).astype(BF16)
        cols = []
        for gi in range(SGU_GROUPS):
            vg = jnp.concatenate([v[n * c:(n + 1) * c, gi * gd:(gi + 1) * gd] for n in range(nc)], axis=1)
            sg = _dot(ws[gi], vg)
            bias = bs_ref[gi]
            cols.append(jnp.concatenate([sg[:, n * gd:(n + 1) * gd] + bias for n in range(nc)], axis=0))
        s = jnp.concatenate(cols, axis=1)
        y_ref[i * SGU_ROWS:(i + 1) * SGU_ROWS] = (u * s).astype(y_ref.dtype)


def _sgu(x, gain, w_in, b_in, v_norm, w_s, b_s, tm=1024):
    m, d = x.shape
    width = v_norm.shape[0]
    gd = width // SGU_GROUPS
    bs = jnp.broadcast_to(b_s[:, :, None], (SGU_GROUPS, SGU_CHUNK, gd)).astype(F32)
    fixed2 = lambda i: (0, 0)
    fixed3 = lambda i: (0, 0, 0)
    return pl.pallas_call(
        _sgu_body,
        grid=(m // tm,),
        in_specs=[
            pl.BlockSpec((tm, d), lambda i: (i, 0)),
            pl.BlockSpec((1, d), fixed2),
            pl.BlockSpec(w_in.shape, fixed2),
            pl.BlockSpec((1, 2 * width), fixed2),
            pl.BlockSpec((1, width), fixed2),
            pl.BlockSpec(w_s.shape, fixed3),
            pl.BlockSpec(bs.shape, fixed3),
        ],
        out_specs=pl.BlockSpec((tm, width), lambda i: (i, 0)),
        out_shape=jax.ShapeDtypeStruct((m, width), BF16),
        compiler_params=_params("parallel"),
        name="sgu",
    )(x, gain.reshape(1, d), w_in, b_in.reshape(1, 2 * width), v_norm.reshape(1, width),
      w_s, bs)


def _gla_mixer(x, batch, seq, norm1, w_in, gw1, gw2, gb, head_norm, cast=()):
    q, k, v, og, la, *cast_out = _gla_inproj(x, norm1, w_in, gw1, gw2, gb, cast)
    return _gla_core(q, k, v, og, la, head_norm, batch, seq), cast_out


def _diff_mixer(x, batch, seq, norm1, w_in, lq1, lk1, lq2, lk2, head_norm, layer_idx, cast=()):
    qt, k, vt = _diff_inproj(x, norm1, w_in)
    o, *cast_out = _diff_attn(qt, k, vt, lq1, lk1, lq2, lk2, head_norm, batch, seq, layer_idx, cast)
    return o, cast_out


def kernel(x, l0_norm1, l0_w_in, l0_gate_w1, l0_gate_w2, l0_gate_b, l0_head_norm, l0_w_out, l0_norm2, l0_mlp_w1, l0_mlp_w2, l1_norm1, l1_w_in, l1_lambda_q1, l1_lambda_k1, l1_lambda_q2, l1_lambda_k2, l1_head_norm, l1_w_out, l1_norm2, l1_mlp_w1, l1_mlp_w2, l2_norm1, l2_w_in, l2_b_in, l2_v_norm, l2_w_s, l2_b_s, l2_w_out, l2_norm2, l2_mlp_w1, l2_mlp_w2, l3_norm1, l3_w_in, l3_gate_w1, l3_gate_w2, l3_gate_b, l3_head_norm, l3_w_out, l3_norm2, l3_mlp_w1, l3_mlp_w2, final_norm):
    batch, seq, d = x.shape
    h = x.reshape(batch * seq, d)
    o, (w1, w2) = _gla_mixer(h, batch, seq, l0_norm1, l0_w_in, l0_gate_w1, l0_gate_w2, l0_gate_b, l0_head_norm,
                             cast=(l0_mlp_w1, l0_mlp_w2))
    h = _mlp(h, o, l0_w_out, l0_norm2, w1, w2)
    later = (l1_mlp_w1, l1_mlp_w2, l2_mlp_w1, l2_mlp_w2, l3_mlp_w1, l3_mlp_w2)
    o, later = _diff_mixer(h, batch, seq, l1_norm1, l1_w_in, l1_lambda_q1, l1_lambda_k1, l1_lambda_q2, l1_lambda_k2,
                           l1_head_norm, 1, cast=later)
    h = _mlp(h, o, l1_w_out, l1_norm2, later[0], later[1])
    o = _sgu(h, l2_norm1, l2_w_in, l2_b_in, l2_v_norm, l2_w_s, l2_b_s)
    h = _mlp(h, o, l2_w_out, l2_norm2, later[2], later[3])
    o, _ = _gla_mixer(h, batch, seq, l3_norm1, l3_w_in, l3_gate_w1, l3_gate_w2, l3_gate_b, l3_head_norm)
    h = _mlp(h, o, l3_w_out, l3_norm2, later[4], later[5], final_gain=final_norm)
    return h.reshape(batch, seq, d)
```

```python
import functools
import math

import jax
import jax.numpy as jnp
from jax import lax
from jax.experimental import pallas as pl
from jax.experimental.pallas import tpu as pltpu

F32 = jnp.float32
BF16 = jnp.bfloat16

EPS = 1e-6
NEG_INF = -1e30

GLA_HEADS = 4
GLA_RANK = 16
GLA_GATE_NORM = 16.0
GLA_CHUNK = 64

DIFF_HEAD_DIM = 64
DIFF_HEADS = 8

SGU_CHUNK = 128
SGU_GROUPS = 8

LANES = 128
LOG2E = 1.4426950408889634
VMEM_LIMIT = 52 * 1024 * 1024


def _params(*sem):
    return pltpu.CompilerParams(dimension_semantics=sem, vmem_limit_bytes=VMEM_LIMIT)


def _dot(a, b):
    return jnp.dot(a, b, preferred_element_type=F32)


def _dot_nt(a, b):
    return lax.dot_general(a, b, (((1,), (1,)), ((), ())), preferred_element_type=F32)


def _dot_tn(a, b):
    return lax.dot_general(a, b, (((0,), (0,)), ((), ())), preferred_element_type=F32)


def _rms(x, g):
    return x * lax.rsqrt(jnp.mean(x * x, axis=-1, keepdims=True) + EPS) * g


BF16_ROWS = 16


def _cast_specs(weights, steps, step_of):
    in_specs, out_specs, out_shapes = [], [], []
    for w in weights:
        rows, cols = w.shape
        blk = max(BF16_ROWS, rows // steps)
        nblk = rows // blk
        index = lambda *ids, nblk=nblk: (step_of(*ids) * nblk // steps, 0)
        in_specs.append(pl.BlockSpec((blk, cols), index))
        out_specs.append(pl.BlockSpec((blk, cols), index))
        out_shapes.append(jax.ShapeDtypeStruct(w.shape, BF16))
    return in_specs, out_specs, out_shapes


def _cast_blocks(src_refs, dst_refs):
    for src, dst in zip(src_refs, dst_refs):
        dst[...] = src[...].astype(BF16)


INPROJ_GROUPS = 2


def _gla_inproj_body(n_cast, x_ref, g_ref, w_ref, gw1_ref, gw2_ref, gb_ref, *refs):
    q_ref, k_ref, v_ref, og_ref, la_ref = refs[n_cast:n_cast + 5]
    _cast_blocks(refs[:n_cast], refs[n_cast + 5:])
    dk = q_ref.shape[1]
    dv = v_ref.shape[1]
    rows = x_ref.shape[0] // INPROJ_GROUPS
    for r in range(0, x_ref.shape[0], rows):
        sl = slice(r, r + rows)
        h = _rms(x_ref[sl], g_ref[...]).astype(BF16)
        t = _dot(h, gw1_ref[...]).astype(BF16)
        q_ref[sl] = _dot(h, w_ref[:, 0:dk].astype(BF16))
        z = _dot(t, gw2_ref[...]) + gb_ref[...]
        k_ref[sl] = _dot(h, w_ref[:, dk:2 * dk].astype(BF16))
        la_ref[sl] = (jnp.minimum(z, 0.0) - jnp.log(1.0 + jnp.exp(-jnp.abs(z)))) / GLA_GATE_NORM
        v_ref[sl] = _dot(h, w_ref[:, 2 * dk:2 * dk + dv].astype(BF16)).astype(BF16)
        og_ref[sl] = _dot(h, w_ref[:, 2 * dk + dv:2 * dk + 2 * dv].astype(BF16))


def _gla_inproj(x, gain, w_in, gw1, gw2, gb, cast=(), tm=512):
    m, d = x.shape
    cast_in, cast_out, cast_shapes = _cast_specs(cast, m // tm, lambda i: i)
    dk = gw2.shape[1]
    dv = (w_in.shape[1] - 2 * dk) // 2
    gw1p = jnp.zeros((d, LANES), BF16).at[:, :GLA_RANK].set(gw1.astype(BF16))
    gw2p = jnp.zeros((LANES, dk), BF16).at[:GLA_RANK, :].set(gw2.astype(BF16))
    row = lambda i: (i, 0)
    fixed = lambda i: (0, 0)
    return pl.pallas_call(
        functools.partial(_gla_inproj_body, len(cast)),
        grid=(m // tm,),
        in_specs=[
            pl.BlockSpec((tm, d), row),
            pl.BlockSpec((1, d), fixed),
            pl.BlockSpec(w_in.shape, fixed),
            pl.BlockSpec(gw1p.shape, fixed),
            pl.BlockSpec(gw2p.shape, fixed),
            pl.BlockSpec((1, dk), fixed),
        ] + cast_in,
        out_specs=[
            pl.BlockSpec((tm, dk), row),
            pl.BlockSpec((tm, dk), row),
            pl.BlockSpec((tm, dv), row),
            pl.BlockSpec((tm, dv), row),
            pl.BlockSpec((tm, dk), row),
        ] + cast_out,
        out_shape=[
            jax.ShapeDtypeStruct((m, dk), F32),
            jax.ShapeDtypeStruct((m, dk), F32),
            jax.ShapeDtypeStruct((m, dv), BF16),
            jax.ShapeDtypeStruct((m, dv), F32),
            jax.ShapeDtypeStruct((m, dk), F32),
        ] + cast_shapes,
        compiler_params=_params("parallel"),
        name="gla_inproj",
    )(x, gain.reshape(1, d), w_in, gw1p, gw2p, gb.reshape(1, dk), *cast)


def _gla_body(tri_ref, q_ref, k_ref, v_ref, og_ref, la_ref, hn_ref, o_ref, st_ref):
    c = GLA_CHUNK
    tc, hk = q_ref.shape
    n_chunks = tc // c

    @pl.when(pl.program_id(2) == 0)
    def _():
        st_ref[...] = jnp.zeros_like(st_ref)

    tri = tri_ref[...]
    tg = tri.shape[0]
    la = la_ref[...]
    la_hi = la.astype(BF16)
    la_lo = (la - la_hi.astype(F32)).astype(BF16)
    b = jnp.concatenate([_dot(tri, la_hi[r:r + tg]) + _dot(tri, la_lo[r:r + tg]) for r in range(0, tc, tg)],
                        axis=0)
    b_lasts = [b[(n + 1) * c - 1:(n + 1) * c, :] for n in range(n_chunks)]
    b_end = jnp.concatenate([jnp.broadcast_to(bl, (c, hk)) for bl in b_lasts], axis=0)

    q = q_ref[...] * (hk ** -0.5)
    k = k_ref[...]
    q_dec = (q * jnp.exp(b)).astype(BF16)
    k_inv = (k * jnp.exp(-b)).astype(BF16)
    k_end = (k * jnp.exp(b_end - b)).astype(BF16)

    row = lax.broadcasted_iota(jnp.int32, (c, c), 0)
    col = lax.broadcasted_iota(jnp.int32, (c, c), 1)
    causal = col <= row
    chunk = lambda a, n: a[n * c:(n + 1) * c]

    atts = [jnp.where(causal, _dot_nt(chunk(q_dec, n), chunk(k_inv, n)), 0.0).astype(BF16)
            for n in range(n_chunks)]
    kvs = [_dot_tn(v_ref[n * c:(n + 1) * c, :], chunk(k_end, n)) for n in range(n_chunks)]
    intra = [_dot(atts[n], v_ref[n * c:(n + 1) * c, :]) for n in range(n_chunks)]

    st = st_ref[...]
    outs = []
    for n in range(n_chunks):
        outs.append(intra[n] + _dot_nt(chunk(q_dec, n), st.astype(BF16)))
        st = jnp.exp(b_lasts[n]) * st + kvs[n]
    st_ref[...] = st

    o = jnp.concatenate(outs, axis=0)
    o = o * lax.rsqrt(jnp.mean(o * o, axis=-1, keepdims=True) + EPS) * hn_ref[...]
    g = og_ref[...]
    o_ref[...] = (o * (g / (1.0 + jnp.exp(-g)))).astype(o_ref.dtype)


GLA_CUMSUM_ROWS = 256


def _gla_core(q, k, v, og, la, head_norm, batch, seq, tc=2048):
    tc = min(tc, seq)
    m, dk = q.shape
    dv = v.shape[1]
    hk = dk // GLA_HEADS
    hv = dv // GLA_HEADS
    nt = seq // tc
    idx = jnp.arange(GLA_CUMSUM_ROWS)
    tri = ((idx[:, None] // GLA_CHUNK == idx[None, :] // GLA_CHUNK)
           & (idx[None, :] <= idx[:, None])).astype(BF16)
    blk = lambda b, h, t: (b * nt + t, h)
    return pl.pallas_call(
        _gla_body,
        grid=(batch, GLA_HEADS, nt),
        in_specs=[
            pl.BlockSpec(tri.shape, lambda b, h, t: (0, 0)),
            pl.BlockSpec((tc, hk), blk),
            pl.BlockSpec((tc, hk), blk),
            pl.BlockSpec((tc, hv), blk),
            pl.BlockSpec((tc, hv), blk),
            pl.BlockSpec((tc, hk), blk),
            pl.BlockSpec((1, hv), lambda b, h, t: (0, 0)),
        ],
        out_specs=pl.BlockSpec((tc, hv), blk),
        out_shape=jax.ShapeDtypeStruct((m, dv), BF16),
        scratch_shapes=[pltpu.VMEM((hv, hk), F32)],
        compiler_params=_params("parallel", "parallel", "arbitrary"),
        name="gla_core",
    )(tri, q, k, v, og, la, head_norm.reshape(1, hv))


MLP_INIT_GROUPS = 4


def _mlp_body(final, x_ref, o_ref, wo_ref, g_ref, w1_ref, w2_ref, gf_ref, y_ref, x1_scr, h_scr, acc_scr):
    f = pl.program_id(1)

    @pl.when(f == 0)
    def _():
        acc_scr[...] = jnp.zeros_like(acc_scr)
        rows = x_ref.shape[0] // MLP_INIT_GROUPS
        wo = wo_ref[...].astype(BF16)
        for r in range(0, x_ref.shape[0], rows):
            x1 = x_ref[r:r + rows] + _dot(o_ref[r:r + rows], wo)
            x1_scr[r:r + rows] = x1
            h_scr[r:r + rows] = _rms(x1, g_ref[...]).astype(BF16)

    a = jnp.maximum(_dot(h_scr[...], w1_ref[...]), 0.0)
    acc_scr[...] += _dot((a * a).astype(BF16), w2_ref[...])

    @pl.when(f == pl.num_programs(1) - 1)
    def _():
        y = x1_scr[...] + acc_scr[...]
        if final:
            y = _rms(y, gf_ref[...])
        y_ref[...] = y


def _mlp(x, o, w_out, gain, w1, w2, final_gain=None, tm=1024, tf=1024):
    m, d = x.shape
    kdim = o.shape[1]
    dff = w1.shape[1]
    final = final_gain is not None
    gf = (final_gain if final else gain).reshape(1, d)
    return pl.pallas_call(
        functools.partial(_mlp_body, final),
        grid=(m // tm, dff // tf),
        in_specs=[
            pl.BlockSpec((tm, d), lambda i, f: (i, 0)),
            pl.BlockSpec((tm, kdim), lambda i, f: (i, 0)),
            pl.BlockSpec((kdim, d), lambda i, f: (0, 0)),
            pl.BlockSpec((1, d), lambda i, f: (0, 0)),
            pl.BlockSpec((d, tf), lambda i, f: (0, f)),
            pl.BlockSpec((tf, d), lambda i, f: (f, 0)),
            pl.BlockSpec((1, d), lambda i, f: (0, 0)),
        ],
        out_specs=pl.BlockSpec((tm, d), lambda i, f: (i, 0)),
        out_shape=jax.ShapeDtypeStruct((m, d), F32),
        scratch_shapes=[pltpu.VMEM((tm, d), F32), pltpu.VMEM((tm, d), BF16), pltpu.VMEM((tm, d), F32)],
        compiler_params=_params("parallel", "arbitrary"),
        name="mlp",
    )(x, o, w_out, gain.reshape(1, d), w1, w2, gf)


Q_BLOCK = 256
KEY_BLOCK = 512


def _diff_inproj_body(x_ref, g_ref, wqt_ref, wk_ref, wvt_ref, qt_ref, k_ref, vt_ref):
    tq = qt_ref.shape[2]
    tk = vt_ref.shape[2]
    for r in range(0, x_ref.shape[0], tq):
        h = _rms(x_ref[r:r + tq], g_ref[...]).astype(BF16)
        k_ref[r:r + tq] = _dot(h, wk_ref[...]).astype(BF16)
        qt = _dot_nt(wqt_ref[...], h) * (LOG2E * DIFF_HEAD_DIM ** -0.5)
        qt_ref[r // tq] = qt.astype(BF16)
        vt_ref[r // tk, :, r % tk:r % tk + tq] = _dot_nt(wvt_ref[...], h).astype(BF16)


def _diff_inproj(x, gain, w_in, tm=512):
    m, d = x.shape
    dq = 2 * DIFF_HEADS * DIFF_HEAD_DIM
    wqt = w_in[:, :dq].T.astype(BF16)
    wk = w_in[:, dq:2 * dq].astype(BF16)
    wvt = w_in[:, 2 * dq:].T.astype(BF16)
    dv = wvt.shape[0]
    tq, tk = Q_BLOCK, KEY_BLOCK
    fixed = lambda i: (0, 0)
    return pl.pallas_call(
        _diff_inproj_body,
        grid=(m // tm,),
        in_specs=[
            pl.BlockSpec((tm, d), lambda i: (i, 0)),
            pl.BlockSpec((1, d), fixed),
            pl.BlockSpec(wqt.shape, fixed),
            pl.BlockSpec(wk.shape, fixed),
            pl.BlockSpec(wvt.shape, fixed),
        ],
        out_specs=[
            pl.BlockSpec((tm // tq, dq, tq), lambda i: (i, 0, 0)),
            pl.BlockSpec((tm, dq), lambda i: (i, 0)),
            pl.BlockSpec((tm // tk, dv, tk), lambda i: (i, 0, 0)),
        ],
        out_shape=[
            jax.ShapeDtypeStruct((m // tq, dq, tq), BF16),
            jax.ShapeDtypeStruct((m, dq), BF16),
            jax.ShapeDtypeStruct((m // tk, dv, tk), BF16),
        ],
        compiler_params=_params("parallel"),
        name="diff_inproj",
    )(x, gain.reshape(1, d), wqt, wk, wvt)


ATT_HEADS_PER_STEP = 4
KEY_SPLIT = 32
ONES_ROWS = 16
FEAT_ROWS = 8


def _bf16_pieces(c):
    c1 = c.astype(BF16).astype(F32)
    c2 = (c - c1).astype(BF16).astype(F32)
    c3 = (c - c1 - c2).astype(BF16).astype(F32)
    return c1, c2, c3


def _diff_attn_body(lam_init, n_cast, slopes_ref, qt_ref, k_ref, vt_ref, lq1_ref, lk1_ref, lq2_ref, lk2_ref,
                    hn_ref, *refs):
    o_ref = refs[n_cast]
    m_scr, acc_scr, s_scr, mx_scr = refs[2 * n_cast + 1:]
    tq, tk = Q_BLOCK, KEY_BLOCK
    hd = 2 * DIFF_HEAD_DIM
    nh = ATT_HEADS_PER_STEP
    hg = pl.program_id(1)
    n_full = pl.program_id(2)

    def tri_mask(t):
        key = lax.broadcasted_iota(jnp.int32, (tq, 2 * tq), 0)
        qry = lax.broadcasted_iota(jnp.int32, (tq, 2 * tq), 1) % tq
        return jnp.where(key <= qry, t, NEG_INF)

    kj = lax.broadcasted_iota(jnp.int32, (tk, LANES), 0)
    kl = lax.broadcasted_iota(jnp.int32, (tk, LANES), 1)
    kfeat = jnp.where(kl < 3, kj // KEY_SPLIT, jnp.where(kl < 6, kj % KEY_SPLIT, 0)).astype(F32).astype(BF16)
    ones = jnp.ones((ONES_ROWS, tk), BF16)
    half = lax.broadcasted_iota(jnp.int32, (hd, tq), 0) < DIFF_HEAD_DIM
    frow = lax.broadcasted_iota(jnp.int32, (FEAT_ROWS, 2 * tq), 0)

    def make_heads(slot):
        heads = []
        for j in range(nh):
            slope = slopes_ref[hg * nh + j] * LOG2E
            qt = qt_ref[slot, j * hd:(j + 1) * hd, :]
            zero = jnp.zeros_like(qt)
            q2t = jnp.concatenate([jnp.where(half, qt, zero), jnp.where(half, zero, qt)], axis=1)
            c1, c2, c3 = _bf16_pieces(jnp.full(frow.shape, slope, F32))
            s = float(KEY_SPLIT)
            qfeat = jnp.where(frow == 0, s * c1, jnp.where(frow == 1, s * c2, jnp.where(frow == 2, s * c3,
                    jnp.where(frow == 3, c1, jnp.where(frow == 4, c2, jnp.where(frow == 5, c3, 0.0))))))
            qfeat = jnp.concatenate([qfeat, jnp.zeros((hd - FEAT_ROWS, 2 * tq), F32)], axis=0).astype(BF16)
            heads.append((slope, jnp.concatenate([q2t, qfeat], axis=0)))
        return heads

    def scores(heads, j, kb, rows=tk):
        k_blk = k_ref[pl.ds(pl.multiple_of(kb * tk, tk), rows), j * hd:(j + 1) * hd]
        return _dot(jnp.concatenate([k_blk, kfeat[:rows]], axis=1), heads[j][1])

    def stage(j, t):
        s_scr[j, :t.shape[0]] = t
        mx_scr[j] = jnp.max(t, axis=0, keepdims=True)

    def consume(heads, slot, j, kb, rows=tk):
        q_tile = 2 * n_full + slot
        shift = heads[j][0] * (kb * tk - q_tile * tq).astype(F32)
        m_old = m_scr[j]
        m_new = jnp.maximum(m_old, mx_scr[j] + shift)
        alpha = jnp.exp2(m_old - m_new)
        p = jnp.exp2(s_scr[j, :rows] - (m_new - shift)).astype(BF16)
        vt_aug = jnp.concatenate([vt_ref[kb, j * hd:(j + 1) * hd, :rows], ones[:, :rows]], axis=0)
        acc_scr[j] = alpha * acc_scr[j] + _dot(vt_aug, p)
        m_scr[j] = m_new

    def pipelined(heads, slot, kb, next_tile):
        for j in range(nh):
            t_next = next_tile(j)
            consume(heads, slot, j, kb)
            stage(j, t_next)

    def plain_stages(heads, slot):
        n_plain = jnp.maximum(n_full - 1, 0)

        def two_tiles(i, carry):
            for kb in (2 * i, 2 * i + 1):
                pipelined(heads, slot, kb, lambda j, kb=kb: scores(heads, j, kb + 1))
            return carry

        lax.fori_loop(0, n_plain // 2, two_tiles, 0)

        @pl.when(n_plain % 2 == 1)
        def _():
            pipelined(heads, slot, n_plain - 1, lambda j: scores(heads, j, n_plain))

    def reset():
        m_scr[...] = jnp.full_like(m_scr, NEG_INF)
        acc_scr[...] = jnp.zeros_like(acc_scr)

    def finish(slot):
        lam = (jnp.exp(jnp.sum(lq1_ref[...] * lk1_ref[...], keepdims=True))
               - jnp.exp(jnp.sum(lq2_ref[...] * lk2_ref[...], keepdims=True)) + lam_init)
        for j in range(nh):
            acc = acc_scr[j]
            o = acc[:hd] / acc[hd:hd + 1]
            ot = o[:, :tq] - lam * o[:, tq:]
            ot = ot * lax.rsqrt(jnp.mean(ot * ot, axis=0, keepdims=True) + EPS)
            ot = ot * hn_ref[...] * (1.0 - lam_init)
            o_ref[slot * tq:(slot + 1) * tq, j * hd:(j + 1) * hd] = ot.T.astype(o_ref.dtype)

    even, odd = make_heads(0), make_heads(1)
    last_even = lambda j: tri_mask(scores(even, j, n_full, rows=tq))

    def last_odd(j):
        t = scores(odd, j, n_full)
        return jnp.concatenate([t[:tq], tri_mask(t[tq:])], axis=0)

    reset()

    @pl.when(n_full == 0)
    def _():
        for j in range(nh):
            stage(j, last_even(j))

    @pl.when(n_full > 0)
    def _():
        for j in range(nh):
            stage(j, scores(even, j, 0))

    plain_stages(even, 0)

    @pl.when(n_full > 0)
    def _():
        pipelined(even, 0, n_full - 1, last_even)

    @pl.when(n_full == 0)
    def _():
        for j in range(nh):
            t_next = last_odd(j)
            consume(even, 0, j, n_full, rows=tq)
            stage(j, t_next)

    @pl.when(n_full > 0)
    def _():
        for j in range(nh):
            t_next = scores(odd, j, 0)
            consume(even, 0, j, n_full, rows=tq)
            stage(j, t_next)

    finish(0)

    reset()
    plain_stages(odd, 1)

    @pl.when(n_full > 0)
    def _():
        pipelined(odd, 1, n_full - 1, last_odd)

    _cast_blocks(refs[:n_cast], refs[n_cast + 1:2 * n_cast + 1])
    for j in range(nh):
        consume(odd, 1, j, n_full)
    finish(1)


def _diff_attn(qt, k, vt, lq1, lk1, lq2, lk2, head_norm, batch, seq, layer_idx, cast=()):
    tq, tk = Q_BLOCK, KEY_BLOCK
    nh = ATT_HEADS_PER_STEP
    m, dq = k.shape
    hd = 2 * DIFF_HEAD_DIM
    assert tk == 2 * tq
    nq = seq // tk
    ng = DIFF_HEADS // nh
    cast_in, cast_out, cast_shapes = _cast_specs(cast, batch * ng * nq, lambda b, h, i: (b * ng + h) * nq + i)
    lam_init = 0.8 - 0.6 * math.exp(-0.3 * layer_idx)
    slopes = 2.0 ** (-8.0 * jnp.arange(1, DIFF_HEADS + 1, dtype=F32) / DIFF_HEADS)
    vec = lambda a: a.reshape(1, DIFF_HEAD_DIM).astype(F32)
    lam_spec = pl.BlockSpec((1, DIFF_HEAD_DIM), lambda b, h, i: (0, 0))
    return pl.pallas_call(
        functools.partial(_diff_attn_body, lam_init, len(cast)),
        grid=(batch, ng, nq),
        in_specs=[
            pl.BlockSpec(memory_space=pltpu.SMEM),
            pl.BlockSpec((2, nh * hd, tq), lambda b, h, i: (b * nq + i, h, 0)),
            pl.BlockSpec((seq, nh * hd), lambda b, h, i: (b, h)),
            pl.BlockSpec((seq // tk, nh * hd, tk), lambda b, h, i: (b, h, 0)),
            lam_spec, lam_spec, lam_spec, lam_spec,
            pl.BlockSpec((hd, 1), lambda b, h, i: (0, 0)),
        ] + cast_in,
        out_specs=[pl.BlockSpec((tk, nh * hd), lambda b, h, i: (b * nq + i, h))] + cast_out,
        out_shape=[jax.ShapeDtypeStruct((m, dq), BF16)] + cast_shapes,
        scratch_shapes=[
            pltpu.VMEM((nh, 1, 2 * tq), F32),
            pltpu.VMEM((nh, hd + ONES_ROWS, 2 * tq), F32),
            pltpu.VMEM((nh, tk, 2 * tq), F32),
            pltpu.VMEM((nh, 1, 2 * tq), F32),
        ],
        compiler_params=_params("parallel", "parallel", "arbitrary"),
        name="diff_attn",
    )(slopes, qt, k, vt, vec(lq1), vec(lk1), vec(lq2), vec(lk2), head_norm.reshape(hd, 1).astype(F32), *cast)


SGU_ROWS = 256


def _sgu_body(x_ref, g_ref, win_ref, bin_ref, vn_ref, ws_ref, bs_ref, y_ref):
    tm = x_ref.shape[0]
    width = vn_ref.shape[1]
    gd = width // SGU_GROUPS
    c = SGU_CHUNK
    nc = SGU_ROWS // c
    row = lax.broadcasted_iota(jnp.int32, (c, c), 0)
    col = lax.broadcasted_iota(jnp.int32, (c, c), 1)
    ws = [jnp.where(col <= row, ws_ref[gi], 0.0).astype(BF16) for gi in range(SGU_GROUPS)]
    w_in = win_ref[...].astype(BF16)
    uvs = [_dot(_rms(x_ref[r:r + SGU_ROWS], g_ref[...]).astype(BF16), w_in) + bin_ref[...]
           for r in range(0, tm, SGU_ROWS)]
    for i, uv in enumerate(uvs):
        uv = 0.5 * uv * (1.0 + jnp.tanh(math.sqrt(2.0 / math.pi) * (uv + 0.044715 * (uv * uv * uv))))
        u = uv[:, :width]
        v = _rms(uv[:, width:], vn_ref[...]).astype(BF16)
        cols = []
        for gi in range(SGU_GROUPS):
            vg = jnp.concatenate([v[n * c:(n + 1) * c, gi * gd:(gi + 1) * gd] for n in range(nc)], axis=1)
            sg = _dot(ws[gi], vg)
            bias = bs_ref[gi]
            cols.append(jnp.concatenate([sg[:, n * gd:(n + 1) * gd] + bias for n in range(nc)], axis=0))
        s = jnp.concatenate(cols, axis=1)
        y_ref[i * SGU_ROWS:(i + 1) * SGU_ROWS] = (u * s).astype(y_ref.dtype)


def _sgu(x, gain, w_in, b_in, v_norm, w_s, b_s, tm=1024):
    m, d = x.shape
    width = v_norm.shape[0]
    gd = width // SGU_GROUPS
    bs = jnp.broadcast_to(b_s[:, :, None], (SGU_GROUPS, SGU_CHUNK, gd)).astype(F32)
    fixed2 = lambda i: (0, 0)
    fixed3 = lambda i: (0, 0, 0)
    return pl.pallas_call(
        _sgu_body,
        grid=(m // tm,),
        in_specs=[
            pl.BlockSpec((tm, d), lambda i: (i, 0)),
            pl.BlockSpec((1, d), fixed2),
            pl.BlockSpec(w_in.shape, fixed2),
            pl.BlockSpec((1, 2 * width), fixed2),
            pl.BlockSpec((1, width), fixed2),
            pl.BlockSpec(w_s.shape, fixed3),
            pl.BlockSpec(bs.shape, fixed3),
        ],
        out_specs=pl.BlockSpec((tm, width), lambda i: (i, 0)),
        out_shape=jax.ShapeDtypeStruct((m, width), BF16),
        compiler_params=_params("parallel"),
        name="sgu",
    )(x, gain.reshape(1, d), w_in, b_in.reshape(1, 2 * width), v_norm.reshape(1, width),
      w_s, bs)


def _gla_mixer(x, batch, seq, norm1, w_in, gw1, gw2, gb, head_norm, cast=()):
    q, k, v, og, la, *cast_out = _gla_inproj(x, norm1, w_in, gw1, gw2, gb, cast)
    return _gla_core(q, k, v, og, la, head_norm, batch, seq), cast_out


def _diff_mixer(x, batch, seq, norm1, w_in, lq1, lk1, lq2, lk2, head_norm, layer_idx, cast=()):
    qt, k, vt = _diff_inproj(x, norm1, w_in)
    o, *cast_out = _diff_attn(qt, k, vt, lq1, lk1, lq2, lk2, head_norm, batch, seq, layer_idx, cast)
    return o, cast_out


def kernel(x, l0_norm1, l0_w_in, l0_gate_w1, l0_gate_w2, l0_gate_b, l0_head_norm, l0_w_out, l0_norm2, l0_mlp_w1, l0_mlp_w2, l1_norm1, l1_w_in, l1_lambda_q1, l1_lambda_k1, l1_lambda_q2, l1_lambda_k2, l1_head_norm, l1_w_out, l1_norm2, l1_mlp_w1, l1_mlp_w2, l2_norm1, l2_w_in, l2_b_in, l2_v_norm, l2_w_s, l2_b_s, l2_w_out, l2_norm2, l2_mlp_w1, l2_mlp_w2, l3_norm1, l3_w_in, l3_gate_w1, l3_gate_w2, l3_gate_b, l3_head_norm, l3_w_out, l3_norm2, l3_mlp_w1, l3_mlp_w2, final_norm):
    batch, seq, d = x.shape
    h = x.reshape(batch * seq, d)
    o, (w1, w2) = _gla_mixer(h, batch, seq, l0_norm1, l0_w_in, l0_gate_w1, l0_gate_w2, l0_gate_b, l0_head_norm,
                             cast=(l0_mlp_w1, l0_mlp_w2))
    h = _mlp(h, o, l0_w_out, l0_norm2, w1, w2)
    later = (l1_mlp_w1, l1_mlp_w2, l2_mlp_w1, l2_mlp_w2, l3_mlp_w1, l3_mlp_w2)
    o, later = _diff_mixer(h, batch, seq, l1_norm1, l1_w_in, l1_lambda_q1, l1_lambda_k1, l1_lambda_q2, l1_lambda_k2,
                           l1_head_norm, 1, cast=later)
    h = _mlp(h, o, l1_w_out, l1_norm2, later[0], later[1])
    o = _sgu(h, l2_norm1, l2_w_in, l2_b_in, l2_v_norm, l2_w_s, l2_b_s)
    h = _mlp(h, o, l2_w_out, l2_norm2, later[2], later[3])
    o, _ = _gla_mixer(h, batch, seq, l3_norm1, l3_w_in, l3_gate_w1, l3_gate_w2, l3_gate_b, l3_head_norm)
    h = _mlp(h, o, l3_w_out, l3_norm2, later[4], later[5], final_gain=final_norm)
    return h.reshape(batch, seq, d)
```

```python
import functools
import math

import jax
import jax.numpy as jnp
from jax import lax
from jax.experimental import pallas as pl
from jax.experimental.pallas import tpu as pltpu

F32 = jnp.float32
BF16 = jnp.bfloat16

EPS = 1e-6
NEG_INF = -1e30

GLA_HEADS = 4
GLA_RANK = 16
GLA_GATE_NORM = 16.0
GLA_CHUNK = 64

DIFF_HEAD_DIM = 64
DIFF_HEADS = 8

SGU_CHUNK = 128
SGU_GROUPS = 8

LANES = 128
LOG2E = 1.4426950408889634
VMEM_LIMIT = 52 * 1024 * 1024


def _params(*sem):
    return pltpu.CompilerParams(dimension_semantics=sem, vmem_limit_bytes=VMEM_LIMIT)


def _dot(a, b):
    return jnp.dot(a, b, preferred_element_type=F32)


def _dot_nt(a, b):
    return lax.dot_general(a, b, (((1,), (1,)), ((), ())), preferred_element_type=F32)


def _dot_tn(a, b):
    return lax.dot_general(a, b, (((0,), (0,)), ((), ())), preferred_element_type=F32)


def _rms(x, g):
    return x * lax.rsqrt(jnp.mean(x * x, axis=-1, keepdims=True) + EPS) * g


BF16_ROWS = 16


def _cast_specs(weights, steps, step_of):
    in_specs, out_specs, out_shapes = [], [], []
    for w in weights:
        rows, cols = w.shape
        blk = max(BF16_ROWS, rows // steps)
        nblk = rows // blk
        index = lambda *ids, nblk=nblk: (step_of(*ids) * nblk // steps, 0)
        in_specs.append(pl.BlockSpec((blk, cols), index))
        out_specs.append(pl.BlockSpec((blk, cols), index))
        out_shapes.append(jax.ShapeDtypeStruct(w.shape, BF16))
    return in_specs, out_specs, out_shapes


def _cast_blocks(src_refs, dst_refs):
    for src, dst in zip(src_refs, dst_refs):
        dst[...] = src[...].astype(BF16)


INPROJ_GROUPS = 2


def _gla_inproj_body(n_cast, x_ref, g_ref, w_ref, gw1_ref, gw2_ref, gb_ref, *refs):
    q_ref, k_ref, v_ref, og_ref, la_ref = refs[n_cast:n_cast + 5]
    _cast_blocks(refs[:n_cast], refs[n_cast + 5:])
    dk = q_ref.shape[1]
    dv = v_ref.shape[1]
    rows = x_ref.shape[0] // INPROJ_GROUPS
    for r in range(0, x_ref.shape[0], rows):
        sl = slice(r, r + rows)
        h = _rms(x_ref[sl], g_ref[...]).astype(BF16)
        t = _dot(h, gw1_ref[...]).astype(BF16)
        q_ref[sl] = _dot(h, w_ref[:, 0:dk].astype(BF16))
        z = _dot(t, gw2_ref[...]) + gb_ref[...]
        k_ref[sl] = _dot(h, w_ref[:, dk:2 * dk].astype(BF16))
        la_ref[sl] = (jnp.minimum(z, 0.0) - jnp.log(1.0 + jnp.exp(-jnp.abs(z)))) / GLA_GATE_NORM
        v_ref[sl] = _dot(h, w_ref[:, 2 * dk:2 * dk + dv].astype(BF16)).astype(BF16)
        og_ref[sl] = _dot(h, w_ref[:, 2 * dk + dv:2 * dk + 2 * dv].astype(BF16))


def _gla_inproj(x, gain, w_in, gw1, gw2, gb, cast=(), tm=512):
    m, d = x.shape
    cast_in, cast_out, cast_shapes = _cast_specs(cast, m // tm, lambda i: i)
    dk = gw2.shape[1]
    dv = (w_in.shape[1] - 2 * dk) // 2
    gw1p = jnp.zeros((d, LANES), BF16).at[:, :GLA_RANK].set(gw1.astype(BF16))
    gw2p = jnp.zeros((LANES, dk), BF16).at[:GLA_RANK, :].set(gw2.astype(BF16))
    row = lambda i: (i, 0)
    fixed = lambda i: (0, 0)
    return pl.pallas_call(
        functools.partial(_gla_inproj_body, len(cast)),
        grid=(m // tm,),
        in_specs=[
            pl.BlockSpec((tm, d), row),
            pl.BlockSpec((1, d), fixed),
            pl.BlockSpec(w_in.shape, fixed),
            pl.BlockSpec(gw1p.shape, fixed),
            pl.BlockSpec(gw2p.shape, fixed),
            pl.BlockSpec((1, dk), fixed),
        ] + cast_in,
        out_specs=[
            pl.BlockSpec((tm, dk), row),
            pl.BlockSpec((tm, dk), row),
            pl.BlockSpec((tm, dv), row),
            pl.BlockSpec((tm, dv), row),
            pl.BlockSpec((tm, dk), row),
        ] + cast_out,
        out_shape=[
            jax.ShapeDtypeStruct((m, dk), F32),
            jax.ShapeDtypeStruct((m, dk), F32),
            jax.ShapeDtypeStruct((m, dv), BF16),
            jax.ShapeDtypeStruct((m, dv), F32),
            jax.ShapeDtypeStruct((m, dk), F32),
        ] + cast_shapes,
        compiler_params=_params("parallel"),
        name="gla_inproj",
    )(x, gain.reshape(1, d), w_in, gw1p, gw2p, gb.reshape(1, dk), *cast)


def _gla_body(tri_ref, q_ref, k_ref, v_ref, og_ref, la_ref, hn_ref, o_ref, st_ref):
    c = GLA_CHUNK
    tc, hk = q_ref.shape
    n_chunks = tc // c

    @pl.when(pl.program_id(2) == 0)
    def _():
        st_ref[...] = jnp.zeros_like(st_ref)

    tri = tri_ref[...]
    tg = tri.shape[0]
    la = la_ref[...]
    la_hi = la.astype(BF16)
    la_lo = (la - la_hi.astype(F32)).astype(BF16)
    b = jnp.concatenate([_dot(tri, la_hi[r:r + tg]) + _dot(tri, la_lo[r:r + tg]) for r in range(0, tc, tg)],
                        axis=0)
    b_lasts = [b[(n + 1) * c - 1:(n + 1) * c, :] for n in range(n_chunks)]
    b_end = jnp.concatenate([jnp.broadcast_to(bl, (c, hk)) for bl in b_lasts], axis=0)

    q = q_ref[...] * (hk ** -0.5)
    k = k_ref[...]
    q_dec = (q * jnp.exp(b)).astype(BF16)
    k_inv = (k * jnp.exp(-b)).astype(BF16)
    k_end = (k * jnp.exp(b_end - b)).astype(BF16)

    row = lax.broadcasted_iota(jnp.int32, (c, c), 0)
    col = lax.broadcasted_iota(jnp.int32, (c, c), 1)
    causal = col <= row
    chunk = lambda a, n: a[n * c:(n + 1) * c]

    atts = [jnp.where(causal, _dot_nt(chunk(q_dec, n), chunk(k_inv, n)), 0.0).astype(BF16)
            for n in range(n_chunks)]
    kvs = [_dot_tn(v_ref[n * c:(n + 1) * c, :], chunk(k_end, n)) for n in range(n_chunks)]
    intra = [_dot(atts[n], v_ref[n * c:(n + 1) * c, :]) for n in range(n_chunks)]

    st = st_ref[...]
    outs = []
    for n in range(n_chunks):
        outs.append(intra[n] + _dot_nt(chunk(q_dec, n), st.astype(BF16)))
        st = jnp.exp(b_lasts[n]) * st + kvs[n]
    st_ref[...] = st

    o = jnp.concatenate(outs, axis=0)
    o = o * lax.rsqrt(jnp.mean(o * o, axis=-1, keepdims=True) + EPS) * hn_ref[...]
    g = og_ref[...]
    o_ref[...] = (o * (g / (1.0 + jnp.exp(-g)))).astype(o_ref.dtype)


GLA_CUMSUM_ROWS = 256


def _gla_core(q, k, v, og, la, head_norm, batch, seq, tc=2048):
    tc = min(tc, seq)
    m, dk = q.shape
    dv = v.shape[1]
    hk = dk // GLA_HEADS
    hv = dv // GLA_HEADS
    nt = seq // tc
    idx = jnp.arange(GLA_CUMSUM_ROWS)
    tri = ((idx[:, None] // GLA_CHUNK == idx[None, :] // GLA_CHUNK)
           & (idx[None, :] <= idx[:, None])).astype(BF16)
    blk = lambda b, h, t: (b * nt + t, h)
    return pl.pallas_call(
        _gla_body,
        grid=(batch, GLA_HEADS, nt),
        in_specs=[
            pl.BlockSpec(tri.shape, lambda b, h, t: (0, 0)),
            pl.BlockSpec((tc, hk), blk),
            pl.BlockSpec((tc, hk), blk),
            pl.BlockSpec((tc, hv), blk),
            pl.BlockSpec((tc, hv), blk),
            pl.BlockSpec((tc, hk), blk),
            pl.BlockSpec((1, hv), lambda b, h, t: (0, 0)),
        ],
        out_specs=pl.BlockSpec((tc, hv), blk),
        out_shape=jax.ShapeDtypeStruct((m, dv), BF16),
        scratch_shapes=[pltpu.VMEM((hv, hk), F32)],
        compiler_params=_params("parallel", "parallel", "arbitrary"),
        name="gla_core",
    )(tri, q, k, v, og, la, head_norm.reshape(1, hv))


MLP_INIT_GROUPS = 4
MLP_FF_CHUNK = 512


def _mlp_body(final, x_ref, o_ref, wo_ref, g_ref, w1_ref, w2_ref, gf_ref, y_ref, h_scr):
    tm = x_ref.shape[0]
    rows = tm // MLP_INIT_GROUPS
    wo = wo_ref[...].astype(BF16)
    for r in range(0, tm, rows):
        x1 = x_ref[r:r + rows] + _dot(o_ref[r:r + rows], wo)
        y_ref[r:r + rows] = x1
        h_scr[r:r + rows] = _rms(x1, g_ref[...]).astype(BF16)
    for c in range(0, w1_ref.shape[1], MLP_FF_CHUNK):
        a = jnp.maximum(_dot(h_scr[...], w1_ref[:, c:c + MLP_FF_CHUNK]), 0.0)
        y_ref[...] += _dot((a * a).astype(BF16), w2_ref[c:c + MLP_FF_CHUNK, :])
    if final:
        y_ref[...] = _rms(y_ref[...], gf_ref[...])


def _mlp(x, o, w_out, gain, w1, w2, final_gain=None, tm=1024):
    m, d = x.shape
    kdim = o.shape[1]
    final = final_gain is not None
    gf = (final_gain if final else gain).reshape(1, d)
    resident = lambda a: pl.BlockSpec(a.shape, lambda i: (0, 0), pipeline_mode=pl.Buffered(1))
    return pl.pallas_call(
        functools.partial(_mlp_body, final),
        grid=(m // tm,),
        in_specs=[
            pl.BlockSpec((tm, d), lambda i: (i, 0)),
            pl.BlockSpec((tm, kdim), lambda i: (i, 0)),
            resident(w_out),
            pl.BlockSpec((1, d), lambda i: (0, 0)),
            resident(w1),
            resident(w2),
            pl.BlockSpec((1, d), lambda i: (0, 0)),
        ],
        out_specs=pl.BlockSpec((tm, d), lambda i: (i, 0)),
        out_shape=jax.ShapeDtypeStruct((m, d), F32),
        scratch_shapes=[pltpu.VMEM((tm, d), BF16)],
        compiler_params=_params("parallel"),
        name="mlp",
    )(x, o, w_out, gain.reshape(1, d), w1, w2, gf)


Q_BLOCK = 256
KEY_BLOCK = 512


def _diff_inproj_body(x_ref, g_ref, wqt_ref, wk_ref, wvt_ref, qt_ref, k_ref, vt_ref):
    tq = qt_ref.shape[2]
    tk = vt_ref.shape[2]
    for r in range(0, x_ref.shape[0], tq):
        h = _rms(x_ref[r:r + tq], g_ref[...]).astype(BF16)
        k_ref[r:r + tq] = _dot(h, wk_ref[...]).astype(BF16)
        qt = _dot_nt(wqt_ref[...], h) * (LOG2E * DIFF_HEAD_DIM ** -0.5)
        qt_ref[r // tq] = qt.astype(BF16)
        vt_ref[r // tk, :, r % tk:r % tk + tq] = _dot_nt(wvt_ref[...], h).astype(BF16)


def _diff_inproj(x, gain, w_in, tm=512):
    m, d = x.shape
    dq = 2 * DIFF_HEADS * DIFF_HEAD_DIM
    wqt = w_in[:, :dq].T.astype(BF16)
    wk = w_in[:, dq:2 * dq].astype(BF16)
    wvt = w_in[:, 2 * dq:].T.astype(BF16)
    dv = wvt.shape[0]
    tq, tk = Q_BLOCK, KEY_BLOCK
    fixed = lambda i: (0, 0)
    return pl.pallas_call(
        _diff_inproj_body,
        grid=(m // tm,),
        in_specs=[
            pl.BlockSpec((tm, d), lambda i: (i, 0)),
            pl.BlockSpec((1, d), fixed),
            pl.BlockSpec(wqt.shape, fixed),
            pl.BlockSpec(wk.shape, fixed),
            pl.BlockSpec(wvt.shape, fixed),
        ],
        out_specs=[
            pl.BlockSpec((tm // tq, dq, tq), lambda i: (i, 0, 0)),
            pl.BlockSpec((tm, dq), lambda i: (i, 0)),
            pl.BlockSpec((tm // tk, dv, tk), lambda i: (i, 0, 0)),
        ],
        out_shape=[
            jax.ShapeDtypeStruct((m // tq, dq, tq), BF16),
            jax.ShapeDtypeStruct((m, dq), BF16),
            jax.ShapeDtypeStruct((m // tk, dv, tk), BF16),
        ],
        compiler_params=_params("parallel"),
        name="diff_inproj",
    )(x, gain.reshape(1, d), wqt, wk, wvt)


ATT_HEADS_PER_STEP = 4
KEY_SPLIT = 32
ONES_ROWS = 16
FEAT_ROWS = 8


def _bf16_pieces(c):
    c1 = c.astype(BF16).astype(F32)
    c2 = (c - c1).astype(BF16).astype(F32)
    c3 = (c - c1 - c2).astype(BF16).astype(F32)
    return c1, c2, c3


def _diff_attn_body(lam_init, n_cast, slopes_ref, qt_ref, k_ref, vt_ref, lq1_ref, lk1_ref, lq2_ref, lk2_ref,
                    hn_ref, *refs):
    o_ref = refs[n_cast]
    m_scr, acc_scr, s_scr, mx_scr = refs[2 * n_cast + 1:]
    tq, tk = Q_BLOCK, KEY_BLOCK
    hd = 2 * DIFF_HEAD_DIM
    nh = ATT_HEADS_PER_STEP
    hg = pl.program_id(1)
    n_full = pl.program_id(2)

    def tri_mask(t):
        key = lax.broadcasted_iota(jnp.int32, (tq, 2 * tq), 0)
        qry = lax.broadcasted_iota(jnp.int32, (tq, 2 * tq), 1) % tq
        return jnp.where(key <= qry, t, NEG_INF)

    kj = lax.broadcasted_iota(jnp.int32, (tk, LANES), 0)
    kl = lax.broadcasted_iota(jnp.int32, (tk, LANES), 1)
    kfeat = jnp.where(kl < 3, kj // KEY_SPLIT, jnp.where(kl < 6, kj % KEY_SPLIT, 0)).astype(F32).astype(BF16)
    ones = jnp.ones((ONES_ROWS, tk), BF16)
    half = lax.broadcasted_iota(jnp.int32, (hd, tq), 0) < DIFF_HEAD_DIM
    frow = lax.broadcasted_iota(jnp.int32, (FEAT_ROWS, 2 * tq), 0)

    def make_heads(slot):
        heads = []
        for j in range(nh):
            slope = slopes_ref[hg * nh + j] * LOG2E
            qt = qt_ref[slot, j * hd:(j + 1) * hd, :]
            zero = jnp.zeros_like(qt)
            q2t = jnp.concatenate([jnp.where(half, qt, zero), jnp.where(half, zero, qt)], axis=1)
            c1, c2, c3 = _bf16_pieces(jnp.full(frow.shape, slope, F32))
            s = float(KEY_SPLIT)
            qfeat = jnp.where(frow == 0, s * c1, jnp.where(frow == 1, s * c2, jnp.where(frow == 2, s * c3,
                    jnp.where(frow == 3, c1, jnp.where(frow == 4, c2, jnp.where(frow == 5, c3, 0.0))))))
            qfeat = jnp.concatenate([qfeat, jnp.zeros((hd - FEAT_ROWS, 2 * tq), F32)], axis=0).astype(BF16)
            heads.append((slope, jnp.concatenate([q2t, qfeat], axis=0)))
        return heads

    def scores(heads, j, kb, rows=tk):
        k_blk = k_ref[pl.ds(pl.multiple_of(kb * tk, tk), rows), j * hd:(j + 1) * hd]
        return _dot(jnp.concatenate([k_blk, kfeat[:rows]], axis=1), heads[j][1])

    def stage(j, t):
        s_scr[j, :t.shape[0]] = t
        mx_scr[j] = jnp.max(t, axis=0, keepdims=True)

    def consume(heads, slot, j, kb, rows=tk):
        q_tile = 2 * n_full + slot
        shift = heads[j][0] * (kb * tk - q_tile * tq).astype(F32)
        m_old = m_scr[j]
        m_new = jnp.maximum(m_old, mx_scr[j] + shift)
        alpha = jnp.exp2(m_old - m_new)
        p = jnp.exp2(s_scr[j, :rows] - (m_new - shift)).astype(BF16)
        vt_aug = jnp.concatenate([vt_ref[kb, j * hd:(j + 1) * hd, :rows], ones[:, :rows]], axis=0)
        acc_scr[j] = alpha * acc_scr[j] + _dot(vt_aug, p)
        m_scr[j] = m_new

    def pipelined(heads, slot, kb, next_tile):
        for j in range(nh):
            t_next = next_tile(j)
            consume(heads, slot, j, kb)
            stage(j, t_next)

    def plain_stages(heads, slot):
        n_plain = jnp.maximum(n_full - 1, 0)

        def two_tiles(i, carry):
            for kb in (2 * i, 2 * i + 1):
                pipelined(heads, slot, kb, lambda j, kb=kb: scores(heads, j, kb + 1))
            return carry

        lax.fori_loop(0, n_plain // 2, two_tiles, 0)

        @pl.when(n_plain % 2 == 1)
        def _():
            pipelined(heads, slot, n_plain - 1, lambda j: scores(heads, j, n_plain))

    def reset():
        m_scr[...] = jnp.full_like(m_scr, NEG_INF)
        acc_scr[...] = jnp.zeros_like(acc_scr)

    def finish(slot):
        lam = (jnp.exp(jnp.sum(lq1_ref[...] * lk1_ref[...], keepdims=True))
               - jnp.exp(jnp.sum(lq2_ref[...] * lk2_ref[...], keepdims=True)) + lam_init)
        for j in range(nh):
            acc = acc_scr[j]
            o = acc[:hd] / acc[hd:hd + 1]
            ot = o[:, :tq] - lam * o[:, tq:]
            ot = ot * lax.rsqrt(jnp.mean(ot * ot, axis=0, keepdims=True) + EPS)
            ot = ot * hn_ref[...] * (1.0 - lam_init)
            o_ref[slot * tq:(slot + 1) * tq, j * hd:(j + 1) * hd] = ot.T.astype(o_ref.dtype)

    even, odd = make_heads(0), make_heads(1)
    last_even = lambda j: tri_mask(scores(even, j, n_full, rows=tq))

    def last_odd(j):
        t = scores(odd, j, n_full)
        return jnp.concatenate([t[:tq], tri_mask(t[tq:])], axis=0)

    reset()

    @pl.when(n_full == 0)
    def _():
        for j in range(nh):
            stage(j, last_even(j))

    @pl.when(n_full > 0)
    def _():
        for j in range(nh):
            stage(j, scores(even, j, 0))

    plain_stages(even, 0)

    @pl.when(n_full > 0)
    def _():
        pipelined(even, 0, n_full - 1, last_even)

    @pl.when(n_full == 0)
    def _():
        for j in range(nh):
            t_next = last_odd(j)
            consume(even, 0, j, n_full, rows=tq)
            stage(j, t_next)

    @pl.when(n_full > 0)
    def _():
        for j in range(nh):
            t_next = scores(odd, j, 0)
            consume(even, 0, j, n_full, rows=tq)
            stage(j, t_next)

    finish(0)

    reset()
    plain_stages(odd, 1)

    @pl.when(n_full > 0)
    def _():
        pipelined(odd, 1, n_full - 1, last_odd)

    _cast_blocks(refs[:n_cast], refs[n_cast + 1:2 * n_cast + 1])
    for j in range(nh):
        consume(odd, 1, j, n_full)
    finish(1)


def _diff_attn(qt, k, vt, lq1, lk1, lq2, lk2, head_norm, batch, seq, layer_idx, cast=()):
    tq, tk = Q_BLOCK, KEY_BLOCK
    nh = ATT_HEADS_PER_STEP
    m, dq = k.shape
    hd = 2 * DIFF_HEAD_DIM
    assert tk == 2 * tq
    nq = seq // tk
    ng = DIFF_HEADS // nh
    cast_in, cast_out, cast_shapes = _cast_specs(cast, batch * ng * nq, lambda b, h, i: (b * ng + h) * nq + i)
    lam_init = 0.8 - 0.6 * math.exp(-0.3 * layer_idx)
    slopes = 2.0 ** (-8.0 * jnp.arange(1, DIFF_HEADS + 1, dtype=F32) / DIFF_HEADS)
    vec = lambda a: a.reshape(1, DIFF_HEAD_DIM).astype(F32)
    lam_spec = pl.BlockSpec((1, DIFF_HEAD_DIM), lambda b, h, i: (0, 0))
    return pl.pallas_call(
        functools.partial(_diff_attn_body, lam_init, len(cast)),
        grid=(batch, ng, nq),
        in_specs=[
            pl.BlockSpec(memory_space=pltpu.SMEM),
            pl.BlockSpec((2, nh * hd, tq), lambda b, h, i: (b * nq + i, h, 0)),
            pl.BlockSpec((seq, nh * hd), lambda b, h, i: (b, h)),
            pl.BlockSpec((seq // tk, nh * hd, tk), lambda b, h, i: (b, h, 0)),
            lam_spec, lam_spec, lam_spec, lam_spec,
            pl.BlockSpec((hd, 1), lambda b, h, i: (0, 0)),
        ] + cast_in,
        out_specs=[pl.BlockSpec((tk, nh * hd), lambda b, h, i: (b * nq + i, h))] + cast_out,
        out_shape=[jax.ShapeDtypeStruct((m, dq), BF16)] + cast_shapes,
        scratch_shapes=[
            pltpu.VMEM((nh, 1, 2 * tq), F32),
            pltpu.VMEM((nh, hd + ONES_ROWS, 2 * tq), F32),
            pltpu.VMEM((nh, tk, 2 * tq), F32),
            pltpu.VMEM((nh, 1, 2 * tq), F32),
        ],
        compiler_params=_params("parallel", "parallel", "arbitrary"),
        name="diff_attn",
    )(slopes, qt, k, vt, vec(lq1), vec(lk1), vec(lq2), vec(lk2), head_norm.reshape(hd, 1).astype(F32), *cast)


SGU_ROWS = 256


def _sgu_body(x_ref, g_ref, win_ref, bin_ref, vn_ref, ws_ref, bs_ref, y_ref):
    tm = x_ref.shape[0]
    width = vn_ref.shape[1]
    gd = width // SGU_GROUPS
    c = SGU_CHUNK
    nc = SGU_ROWS // c
    row = lax.broadcasted_iota(jnp.int32, (c, c), 0)
    col = lax.broadcasted_iota(jnp.int32, (c, c), 1)
    ws = [jnp.where(col <= row, ws_ref[gi], 0.0).astype(BF16) for gi in range(SGU_GROUPS)]
    w_in = win_ref[...].astype(BF16)
    uvs = [_dot(_rms(x_ref[r:r + SGU_ROWS], g_ref[...]).astype(BF16), w_in) + bin_ref[...]
           for r in range(0, tm, SGU_ROWS)]
    for i, uv in enumerate(uvs):
        uv = 0.5 * uv * (1.0 + jnp.tanh(math.sqrt(2.0 / math.pi) * (uv + 0.044715 * (uv * uv * uv))))
        u = uv[:, :width]
        v = _rms(uv[:, width:], vn_ref[...]).astype(BF16)
        cols = []
        for gi in range(SGU_GROUPS):
            vg = jnp.concatenate([v[n * c:(n + 1) * c, gi * gd:(gi + 1) * gd] for n in range(nc)], axis=1)
            sg = _dot(ws[gi], vg)
            bias = bs_ref[gi]
            cols.append(jnp.concatenate([sg[:, n * gd:(n + 1) * gd] + bias for n in range(nc)], axis=0))
        s = jnp.concatenate(cols, axis=1)
        y_ref[i * SGU_ROWS:(i + 1) * SGU_ROWS] = (u * s).astype(y_ref.dtype)


def _sgu(x, gain, w_in, b_in, v_norm, w_s, b_s, tm=1024):
    m, d = x.shape
    width = v_norm.shape[0]
    gd = width // SGU_GROUPS
    bs = jnp.broadcast_to(b_s[:, :, None], (SGU_GROUPS, SGU_CHUNK, gd)).astype(F32)
    fixed2 = lambda i: (0, 0)
    fixed3 = lambda i: (0, 0, 0)
    return pl.pallas_call(
        _sgu_body,
        grid=(m // tm,),
        in_specs=[
            pl.BlockSpec((tm, d), lambda i: (i, 0)),
            pl.BlockSpec((1, d), fixed2),
            pl.BlockSpec(w_in.shape, fixed2),
            pl.BlockSpec((1, 2 * width), fixed2),
            pl.BlockSpec((1, width), fixed2),
            pl.BlockSpec(w_s.shape, fixed3),
            pl.BlockSpec(bs.shape, fixed3),
        ],
        out_specs=pl.BlockSpec((tm, width), lambda i: (i, 0)),
        out_shape=jax.ShapeDtypeStruct((m, width), BF16),
        compiler_params=_params("parallel"),
        name="sgu",
    )(x, gain.reshape(1, d), w_in, b_in.reshape(1, 2 * width), v_norm.reshape(1, width),
      w_s, bs)


def _gla_mixer(x, batch, seq, norm1, w_in, gw1, gw2, gb, head_norm, cast=()):
    q, k, v, og, la, *cast_out = _gla_inproj(x, norm1, w_in, gw1, gw2, gb, cast)
    return _gla_core(q, k, v, og, la, head_norm, batch, seq), cast_out


def _diff_mixer(x, batch, seq, norm1, w_in, lq1, lk1, lq2, lk2, head_norm, layer_idx, cast=()):
    qt, k, vt = _diff_inproj(x, norm1, w_in)
    o, *cast_out = _diff_attn(qt, k, vt, lq1, lk1, lq2, lk2, head_norm, batch, seq, layer_idx, cast)
    return o, cast_out


def kernel(x, l0_norm1, l0_w_in, l0_gate_w1, l0_gate_w2, l0_gate_b, l0_head_norm, l0_w_out, l0_norm2, l0_mlp_w1, l0_mlp_w2, l1_norm1, l1_w_in, l1_lambda_q1, l1_lambda_k1, l1_lambda_q2, l1_lambda_k2, l1_head_norm, l1_w_out, l1_norm2, l1_mlp_w1, l1_mlp_w2, l2_norm1, l2_w_in, l2_b_in, l2_v_norm, l2_w_s, l2_b_s, l2_w_out, l2_norm2, l2_mlp_w1, l2_mlp_w2, l3_norm1, l3_w_in, l3_gate_w1, l3_gate_w2, l3_gate_b, l3_head_norm, l3_w_out, l3_norm2, l3_mlp_w1, l3_mlp_w2, final_norm):
    batch, seq, d = x.shape
    h = x.reshape(batch * seq, d)
    o, (w1, w2) = _gla_mixer(h, batch, seq, l0_norm1, l0_w_in, l0_gate_w1, l0_gate_w2, l0_gate_b, l0_head_norm,
                             cast=(l0_mlp_w1, l0_mlp_w2))
    h = _mlp(h, o, l0_w_out, l0_norm2, w1, w2)
    later = (l1_mlp_w1, l1_mlp_w2, l2_mlp_w1, l2_mlp_w2, l3_mlp_w1, l3_mlp_w2)
    o, later = _diff_mixer(h, batch, seq, l1_norm1, l1_w_in, l1_lambda_q1, l1_lambda_k1, l1_lambda_q2, l1_lambda_k2,
                           l1_head_norm, 1, cast=later)
    h = _mlp(h, o, l1_w_out, l1_norm2, later[0], later[1])
    o = _sgu(h, l2_norm1, l2_w_in, l2_b_in, l2_v_norm, l2_w_s, l2_b_s)
    h = _mlp(h, o, l2_w_out, l2_norm2, later[2], later[3])
    o, _ = _gla_mixer(h, batch, seq, l3_norm1, l3_w_in, l3_gate_w1, l3_gate_w2, l3_gate_b, l3_head_norm)
    h = _mlp(h, o, l3_w_out, l3_norm2, later[4], later[5], final_gain=final_norm)
    return h.reshape(batch, seq, d)
```

```python
import functools
import math

import jax
import jax.numpy as jnp
from jax import lax
from jax.experimental import pallas as pl
from jax.experimental.pallas import tpu as pltpu

F32 = jnp.float32
BF16 = jnp.bfloat16

EPS = 1e-6
NEG_INF = -1e30

GLA_HEADS = 4
GLA_RANK = 16
GLA_GATE_NORM = 16.0
GLA_CHUNK = 64

DIFF_HEAD_DIM = 64
DIFF_HEADS = 8

SGU_CHUNK = 128
SGU_GROUPS = 8

LANES = 128
LOG2E = 1.4426950408889634
VMEM_LIMIT = 52 * 1024 * 1024


def _params(*sem):
    return pltpu.CompilerParams(dimension_semantics=sem, vmem_limit_bytes=VMEM_LIMIT)


def _dot(a, b):
    return jnp.dot(a, b, preferred_element_type=F32)


def _dot_nt(a, b):
    return lax.dot_general(a, b, (((1,), (1,)), ((), ())), preferred_element_type=F32)


def _dot_tn(a, b):
    return lax.dot_general(a, b, (((0,), (0,)), ((), ())), preferred_element_type=F32)


def _rms(x, g):
    return x * lax.rsqrt(jnp.mean(x * x, axis=-1, keepdims=True) + EPS) * g


BF16_ROWS = 16


def _cast_specs(weights, steps, step_of):
    in_specs, out_specs, out_shapes = [], [], []
    for w in weights:
        rows, cols = w.shape
        blk = max(BF16_ROWS, rows // steps)
        nblk = rows // blk
        index = lambda *ids, nblk=nblk: (step_of(*ids) * nblk // steps, 0)
        in_specs.append(pl.BlockSpec((blk, cols), index))
        out_specs.append(pl.BlockSpec((blk, cols), index))
        out_shapes.append(jax.ShapeDtypeStruct(w.shape, BF16))
    return in_specs, out_specs, out_shapes


def _cast_blocks(src_refs, dst_refs):
    for src, dst in zip(src_refs, dst_refs):
        dst[...] = src[...].astype(BF16)


INPROJ_GROUPS = 2


def _gla_inproj_body(n_cast, x_ref, g_ref, w_ref, gw1_ref, gw2_ref, gb_ref, *refs):
    q_ref, k_ref, v_ref, og_ref, la_ref = refs[n_cast:n_cast + 5]
    _cast_blocks(refs[:n_cast], refs[n_cast + 5:])
    dk = q_ref.shape[1]
    dv = v_ref.shape[1]
    rows = x_ref.shape[0] // INPROJ_GROUPS
    for r in range(0, x_ref.shape[0], rows):
        sl = slice(r, r + rows)
        h = _rms(x_ref[sl], g_ref[...]).astype(BF16)
        t = _dot(h, gw1_ref[...]).astype(BF16)
        q_ref[sl] = _dot(h, w_ref[:, 0:dk].astype(BF16))
        z = _dot(t, gw2_ref[...]) + gb_ref[...]
        k_ref[sl] = _dot(h, w_ref[:, dk:2 * dk].astype(BF16))
        la_ref[sl] = (jnp.minimum(z, 0.0) - jnp.log(1.0 + jnp.exp(-jnp.abs(z)))) / GLA_GATE_NORM
        v_ref[sl] = _dot(h, w_ref[:, 2 * dk:2 * dk + dv].astype(BF16)).astype(BF16)
        og_ref[sl] = _dot(h, w_ref[:, 2 * dk + dv:2 * dk + 2 * dv].astype(BF16))


def _gla_inproj(x, gain, w_in, gw1, gw2, gb, cast=(), tm=512):
    m, d = x.shape
    cast_in, cast_out, cast_shapes = _cast_specs(cast, m // tm, lambda i: i)
    dk = gw2.shape[1]
    dv = (w_in.shape[1] - 2 * dk) // 2
    gw1p = jnp.zeros((d, LANES), BF16).at[:, :GLA_RANK].set(gw1.astype(BF16))
    gw2p = jnp.zeros((LANES, dk), BF16).at[:GLA_RANK, :].set(gw2.astype(BF16))
    row = lambda i: (i, 0)
    fixed = lambda i: (0, 0)
    return pl.pallas_call(
        functools.partial(_gla_inproj_body, len(cast)),
        grid=(m // tm,),
        in_specs=[
            pl.BlockSpec((tm, d), row),
            pl.BlockSpec((1, d), fixed),
            pl.BlockSpec(w_in.shape, fixed),
            pl.BlockSpec(gw1p.shape, fixed),
            pl.BlockSpec(gw2p.shape, fixed),
            pl.BlockSpec((1, dk), fixed),
        ] + cast_in,
        out_specs=[
            pl.BlockSpec((tm, dk), row),
            pl.BlockSpec((tm, dk), row),
            pl.BlockSpec((tm, dv), row),
            pl.BlockSpec((tm, dv), row),
            pl.BlockSpec((tm, dk), row),
        ] + cast_out,
        out_shape=[
            jax.ShapeDtypeStruct((m, dk), F32),
            jax.ShapeDtypeStruct((m, dk), F32),
            jax.ShapeDtypeStruct((m, dv), BF16),
            jax.ShapeDtypeStruct((m, dv), F32),
            jax.ShapeDtypeStruct((m, dk), F32),
        ] + cast_shapes,
        compiler_params=_params("parallel"),
        name="gla_inproj",
    )(x, gain.reshape(1, d), w_in, gw1p, gw2p, gb.reshape(1, dk), *cast)


def _gla_body(tri_ref, q_ref, k_ref, v_ref, og_ref, la_ref, hn_ref, o_ref, st_ref):
    c = GLA_CHUNK
    tc, hk = q_ref.shape
    n_chunks = tc // c

    @pl.when(pl.program_id(2) == 0)
    def _():
        st_ref[...] = jnp.zeros_like(st_ref)

    tri = tri_ref[...]
    tg = tri.shape[0]
    la = la_ref[...]
    la_hi = la.astype(BF16)
    la_lo = (la - la_hi.astype(F32)).astype(BF16)
    b = jnp.concatenate([_dot(tri, la_hi[r:r + tg]) + _dot(tri, la_lo[r:r + tg]) for r in range(0, tc, tg)],
                        axis=0)
    b_lasts = [b[(n + 1) * c - 1:(n + 1) * c, :] for n in range(n_chunks)]
    b_end = jnp.concatenate([jnp.broadcast_to(bl, (c, hk)) for bl in b_lasts], axis=0)

    q = q_ref[...] * (hk ** -0.5)
    k = k_ref[...]
    q_dec = (q * jnp.exp(b)).astype(BF16)
    k_inv = (k * jnp.exp(-b)).astype(BF16)
    k_end = (k * jnp.exp(b_end - b)).astype(BF16)

    row = lax.broadcasted_iota(jnp.int32, (c, c), 0)
    col = lax.broadcasted_iota(jnp.int32, (c, c), 1)
    causal = col <= row
    chunk = lambda a, n: a[n * c:(n + 1) * c]

    atts = [jnp.where(causal, _dot_nt(chunk(q_dec, n), chunk(k_inv, n)), 0.0).astype(BF16)
            for n in range(n_chunks)]
    kvs = [_dot_tn(v_ref[n * c:(n + 1) * c, :], chunk(k_end, n)) for n in range(n_chunks)]
    intra = [_dot(atts[n], v_ref[n * c:(n + 1) * c, :]) for n in range(n_chunks)]

    st = st_ref[...]
    outs = []
    for n in range(n_chunks):
        outs.append(intra[n] + _dot_nt(chunk(q_dec, n), st.astype(BF16)))
        st = jnp.exp(b_lasts[n]) * st + kvs[n]
    st_ref[...] = st

    o = jnp.concatenate(outs, axis=0)
    o = o * lax.rsqrt(jnp.mean(o * o, axis=-1, keepdims=True) + EPS) * hn_ref[...]
    g = og_ref[...]
    o_ref[...] = (o * (g / (1.0 + jnp.exp(-g)))).astype(o_ref.dtype)


GLA_CUMSUM_ROWS = 256


def _gla_core(q, k, v, og, la, head_norm, batch, seq, tc=2048):
    tc = min(tc, seq)
    m, dk = q.shape
    dv = v.shape[1]
    hk = dk // GLA_HEADS
    hv = dv // GLA_HEADS
    nt = seq // tc
    idx = jnp.arange(GLA_CUMSUM_ROWS)
    tri = ((idx[:, None] // GLA_CHUNK == idx[None, :] // GLA_CHUNK)
           & (idx[None, :] <= idx[:, None])).astype(BF16)
    blk = lambda b, h, t: (b * nt + t, h)
    return pl.pallas_call(
        _gla_body,
        grid=(batch, GLA_HEADS, nt),
        in_specs=[
            pl.BlockSpec(tri.shape, lambda b, h, t: (0, 0)),
            pl.BlockSpec((tc, hk), blk),
            pl.BlockSpec((tc, hk), blk),
            pl.BlockSpec((tc, hv), blk),
            pl.BlockSpec((tc, hv), blk),
            pl.BlockSpec((tc, hk), blk),
            pl.BlockSpec((1, hv), lambda b, h, t: (0, 0)),
        ],
        out_specs=pl.BlockSpec((tc, hv), blk),
        out_shape=jax.ShapeDtypeStruct((m, dv), BF16),
        scratch_shapes=[pltpu.VMEM((hv, hk), F32)],
        compiler_params=_params("parallel", "parallel", "arbitrary"),
        name="gla_core",
    )(tri, q, k, v, og, la, head_norm.reshape(1, hv))


MLP_INIT_GROUPS = 4
MLP_FF_CHUNK = 512


def _mlp_body(final, x_ref, o_ref, wo_ref, g_ref, w1_ref, w2_ref, gf_ref, y_ref, h_scr):
    tm = x_ref.shape[0]
    rows = tm // MLP_INIT_GROUPS
    wo = wo_ref[...].astype(BF16)
    for r in range(0, tm, rows):
        x1 = x_ref[r:r + rows] + _dot(o_ref[r:r + rows], wo)
        y_ref[r:r + rows] = x1
        h_scr[r:r + rows] = _rms(x1, g_ref[...]).astype(BF16)
    for c in range(0, w1_ref.shape[1], MLP_FF_CHUNK):
        a = jnp.maximum(_dot(h_scr[...], w1_ref[:, c:c + MLP_FF_CHUNK]), 0.0)
        y_ref[...] += _dot((a * a).astype(BF16), w2_ref[c:c + MLP_FF_CHUNK, :])
    if final:
        y_ref[...] = _rms(y_ref[...], gf_ref[...])


def _mlp(x, o, w_out, gain, w1, w2, final_gain=None, tm=1024):
    m, d = x.shape
    kdim = o.shape[1]
    final = final_gain is not None
    gf = (final_gain if final else gain).reshape(1, d)
    resident = lambda a: pl.BlockSpec(a.shape, lambda i: (0, 0), pipeline_mode=pl.Buffered(1))
    return pl.pallas_call(
        functools.partial(_mlp_body, final),
        grid=(m // tm,),
        in_specs=[
            pl.BlockSpec((tm, d), lambda i: (i, 0)),
            pl.BlockSpec((tm, kdim), lambda i: (i, 0)),
            resident(w_out),
            pl.BlockSpec((1, d), lambda i: (0, 0)),
            resident(w1),
            resident(w2),
            pl.BlockSpec((1, d), lambda i: (0, 0)),
        ],
        out_specs=pl.BlockSpec((tm, d), lambda i: (i, 0)),
        out_shape=jax.ShapeDtypeStruct((m, d), F32),
        scratch_shapes=[pltpu.VMEM((tm, d), BF16)],
        compiler_params=_params("parallel"),
        name="mlp",
    )(x, o, w_out, gain.reshape(1, d), w1, w2, gf)


Q_BLOCK = 256
KEY_BLOCK = 512


def _diff_inproj_body(x_ref, g_ref, wqt_ref, wk_ref, wvt_ref, qt_ref, k_ref, vt_ref):
    tq = qt_ref.shape[2]
    tk = vt_ref.shape[2]
    for r in range(0, x_ref.shape[0], tq):
        h = _rms(x_ref[r:r + tq], g_ref[...]).astype(BF16)
        k_ref[r:r + tq] = _dot(h, wk_ref[...]).astype(BF16)
        qt = _dot_nt(wqt_ref[...], h) * (LOG2E * DIFF_HEAD_DIM ** -0.5)
        qt_ref[r // tq] = qt.astype(BF16)
        vt_ref[r // tk, :, r % tk:r % tk + tq] = _dot_nt(wvt_ref[...], h).astype(BF16)


def _diff_inproj(x, gain, w_in, tm=512):
    m, d = x.shape
    dq = 2 * DIFF_HEADS * DIFF_HEAD_DIM
    wqt = w_in[:, :dq].T.astype(BF16)
    wk = w_in[:, dq:2 * dq].astype(BF16)
    wvt = w_in[:, 2 * dq:].T.astype(BF16)
    dv = wvt.shape[0]
    tq, tk = Q_BLOCK, KEY_BLOCK
    fixed = lambda i: (0, 0)
    return pl.pallas_call(
        _diff_inproj_body,
        grid=(m // tm,),
        in_specs=[
            pl.BlockSpec((tm, d), lambda i: (i, 0)),
            pl.BlockSpec((1, d), fixed),
            pl.BlockSpec(wqt.shape, fixed),
            pl.BlockSpec(wk.shape, fixed),
            pl.BlockSpec(wvt.shape, fixed),
        ],
        out_specs=[
            pl.BlockSpec((tm // tq, dq, tq), lambda i: (i, 0, 0)),
            pl.BlockSpec((tm, dq), lambda i: (i, 0)),
            pl.BlockSpec((tm // tk, dv, tk), lambda i: (i, 0, 0)),
        ],
        out_shape=[
            jax.ShapeDtypeStruct((m // tq, dq, tq), BF16),
            jax.ShapeDtypeStruct((m, dq), BF16),
            jax.ShapeDtypeStruct((m // tk, dv, tk), BF16),
        ],
        compiler_params=_params("parallel"),
        name="diff_inproj",
    )(x, gain.reshape(1, d), wqt, wk, wvt)


ATT_HEADS_PER_STEP = 4
KEY_SPLIT = 32
ONES_ROWS = 16
FEAT_ROWS = 8


def _bf16_pieces(c):
    c1 = c.astype(BF16).astype(F32)
    c2 = (c - c1).astype(BF16).astype(F32)
    c3 = (c - c1 - c2).astype(BF16).astype(F32)
    return c1, c2, c3


def _diff_attn_body(lam_init, n_cast, slopes_ref, qt_ref, k_ref, vt_ref, lq1_ref, lk1_ref, lq2_ref, lk2_ref,
                    hn_ref, *refs):
    o_ref = refs[n_cast]
    m_scr, acc_scr, s_scr, mx_scr = refs[2 * n_cast + 1:]
    tq, tk = Q_BLOCK, KEY_BLOCK
    hd = 2 * DIFF_HEAD_DIM
    nh = ATT_HEADS_PER_STEP
    hg = pl.program_id(1)
    n_full = pl.program_id(2)

    def tri_mask(t):
        key = lax.broadcasted_iota(jnp.int32, (tq, 2 * tq), 0)
        qry = lax.broadcasted_iota(jnp.int32, (tq, 2 * tq), 1) % tq
        return jnp.where(key <= qry, t, NEG_INF)

    kj = lax.broadcasted_iota(jnp.int32, (tk, LANES), 0)
    kl = lax.broadcasted_iota(jnp.int32, (tk, LANES), 1)
    kfeat = jnp.where(kl < 3, kj // KEY_SPLIT, jnp.where(kl < 6, kj % KEY_SPLIT, 0)).astype(F32).astype(BF16)
    ones = jnp.ones((ONES_ROWS, tk), BF16)
    half = lax.broadcasted_iota(jnp.int32, (hd, tq), 0) < DIFF_HEAD_DIM
    frow = lax.broadcasted_iota(jnp.int32, (FEAT_ROWS, 2 * tq), 0)

    def make_heads(slot):
        heads = []
        for j in range(nh):
            slope = slopes_ref[hg * nh + j] * LOG2E
            qt = qt_ref[slot, j * hd:(j + 1) * hd, :]
            zero = jnp.zeros_like(qt)
            q2t = jnp.concatenate([jnp.where(half, qt, zero), jnp.where(half, zero, qt)], axis=1)
            c1, c2, c3 = _bf16_pieces(jnp.full(frow.shape, slope, F32))
            s = float(KEY_SPLIT)
            qfeat = jnp.where(frow == 0, s * c1, jnp.where(frow == 1, s * c2, jnp.where(frow == 2, s * c3,
                    jnp.where(frow == 3, c1, jnp.where(frow == 4, c2, jnp.where(frow == 5, c3, 0.0))))))
            qfeat = jnp.concatenate([qfeat, jnp.zeros((hd - FEAT_ROWS, 2 * tq), F32)], axis=0).astype(BF16)
            heads.append((slope, jnp.concatenate([q2t, qfeat], axis=0)))
        return heads

    def scores(heads, j, kb, rows=tk):
        k_blk = k_ref[pl.ds(pl.multiple_of(kb * tk, tk), rows), j * hd:(j + 1) * hd]
        return _dot(jnp.concatenate([k_blk, kfeat[:rows]], axis=1), heads[j][1])

    def stage(j, t):
        s_scr[j, :t.shape[0]] = t
        mx_scr[j] = jnp.max(t, axis=0, keepdims=True)

    def consume(heads, slot, j, kb, rows=tk):
        q_tile = 2 * n_full + slot
        shift = heads[j][0] * (kb * tk - q_tile * tq).astype(F32)
        m_old = m_scr[slot, j]
        m_new = jnp.maximum(m_old, mx_scr[j] + shift)
        alpha = jnp.exp2(m_old - m_new)
        p = jnp.exp2(s_scr[j, :rows] - (m_new - shift)).astype(BF16)
        vt_aug = jnp.concatenate([vt_ref[kb, j * hd:(j + 1) * hd, :rows], ones[:, :rows]], axis=0)
        acc_scr[slot, j] = alpha * acc_scr[slot, j] + _dot(vt_aug, p)
        m_scr[slot, j] = m_new

    def pipelined(heads, slot, kb, next_tile):
        for j in range(nh):
            t_next = next_tile(j)
            consume(heads, slot, j, kb)
            stage(j, t_next)

    def plain_stages(heads, slot):
        n_plain = jnp.maximum(n_full - 1, 0)

        def two_tiles(i, carry):
            for kb in (2 * i, 2 * i + 1):
                pipelined(heads, slot, kb, lambda j, kb=kb: scores(heads, j, kb + 1))
            return carry

        lax.fori_loop(0, n_plain // 2, two_tiles, 0)

        @pl.when(n_plain % 2 == 1)
        def _():
            pipelined(heads, slot, n_plain - 1, lambda j: scores(heads, j, n_plain))

    def finish(slot):
        lam = (jnp.exp(jnp.sum(lq1_ref[...] * lk1_ref[...], keepdims=True))
               - jnp.exp(jnp.sum(lq2_ref[...] * lk2_ref[...], keepdims=True)) + lam_init)
        for j in range(nh):
            acc = acc_scr[slot, j]
            o = acc[:hd] / acc[hd:hd + 1]
            ot = o[:, :tq] - lam * o[:, tq:]
            ot = ot * lax.rsqrt(jnp.mean(ot * ot, axis=0, keepdims=True) + EPS)
            ot = ot * hn_ref[...] * (1.0 - lam_init)
            o_ref[slot * tq:(slot + 1) * tq, j * hd:(j + 1) * hd] = ot.T.astype(o_ref.dtype)

    even, odd = make_heads(0), make_heads(1)
    last_even = lambda j: tri_mask(scores(even, j, n_full, rows=tq))

    def last_odd(j):
        t = scores(odd, j, n_full)
        return jnp.concatenate([t[:tq], tri_mask(t[tq:])], axis=0)

    m_scr[...] = jnp.full_like(m_scr, NEG_INF)
    acc_scr[...] = jnp.zeros_like(acc_scr)


    @pl.when(n_full == 0)
    def _():
        for j in range(nh):
            stage(j, last_even(j))

    @pl.when(n_full > 0)
    def _():
        for j in range(nh):
            stage(j, scores(even, j, 0))

    plain_stages(even, 0)

    @pl.when(n_full > 0)
    def _():
        pipelined(even, 0, n_full - 1, last_even)

    @pl.when(n_full == 0)
    def _():
        for j in range(nh):
            t_next = last_odd(j)
            consume(even, 0, j, n_full, rows=tq)
            stage(j, t_next)

    @pl.when(n_full > 0)
    def _():
        for j in range(nh):
            t_next = scores(odd, j, 0)
            consume(even, 0, j, n_full, rows=tq)
            stage(j, t_next)

    plain_stages(odd, 1)

    @pl.when(n_full > 0)
    def _():
        pipelined(odd, 1, n_full - 1, last_odd)

    _cast_blocks(refs[:n_cast], refs[n_cast + 1:2 * n_cast + 1])
    finish(0)
    for j in range(nh):
        consume(odd, 1, j, n_full)
    finish(1)


def _diff_attn(qt, k, vt, lq1, lk1, lq2, lk2, head_norm, batch, seq, layer_idx, cast=()):
    tq, tk = Q_BLOCK, KEY_BLOCK
    nh = ATT_HEADS_PER_STEP
    m, dq = k.shape
    hd = 2 * DIFF_HEAD_DIM
    assert tk == 2 * tq
    nq = seq // tk
    ng = DIFF_HEADS // nh
    cast_in, cast_out, cast_shapes = _cast_specs(cast, batch * ng * nq, lambda b, h, i: (b * ng + h) * nq + i)
    lam_init = 0.8 - 0.6 * math.exp(-0.3 * layer_idx)
    slopes = 2.0 ** (-8.0 * jnp.arange(1, DIFF_HEADS + 1, dtype=F32) / DIFF_HEADS)
    vec = lambda a: a.reshape(1, DIFF_HEAD_DIM).astype(F32)
    lam_spec = pl.BlockSpec((1, DIFF_HEAD_DIM), lambda b, h, i: (0, 0))
    return pl.pallas_call(
        functools.partial(_diff_attn_body, lam_init, len(cast)),
        grid=(batch, ng, nq),
        in_specs=[
            pl.BlockSpec(memory_space=pltpu.SMEM),
            pl.BlockSpec((2, nh * hd, tq), lambda b, h, i: (b * nq + i, h, 0)),
            pl.BlockSpec((seq, nh * hd), lambda b, h, i: (b, h)),
            pl.BlockSpec((seq // tk, nh * hd, tk), lambda b, h, i: (b, h, 0)),
            lam_spec, lam_spec, lam_spec, lam_spec,
            pl.BlockSpec((hd, 1), lambda b, h, i: (0, 0)),
        ] + cast_in,
        out_specs=[pl.BlockSpec((tk, nh * hd), lambda b, h, i: (b * nq + i, h))] + cast_out,
        out_shape=[jax.ShapeDtypeStruct((m, dq), BF16)] + cast_shapes,
        scratch_shapes=[
            pltpu.VMEM((2, nh, 1, 2 * tq), F32),
            pltpu.VMEM((2, nh, hd + ONES_ROWS, 2 * tq), F32),
            pltpu.VMEM((nh, tk, 2 * tq), F32),
            pltpu.VMEM((nh, 1, 2 * tq), F32),
        ],
        compiler_params=_params("parallel", "parallel", "arbitrary"),
        name="diff_attn",
    )(slopes, qt, k, vt, vec(lq1), vec(lk1), vec(lq2), vec(lk2), head_norm.reshape(hd, 1).astype(F32), *cast)


SGU_ROWS = 256
GELU_C = math.sqrt(2.0 / math.pi)


def _sgu_body(x_ref, g_ref, win_ref, bin_ref, vn_ref, ws_ref, bs_ref, y_ref):
    tm = x_ref.shape[0]
    width = vn_ref.shape[1]
    gd = width // SGU_GROUPS
    c = SGU_CHUNK
    nc = SGU_ROWS // c
    row = lax.broadcasted_iota(jnp.int32, (c, c), 0)
    col = lax.broadcasted_iota(jnp.int32, (c, c), 1)
    ws = [jnp.where(col <= row, ws_ref[gi], 0.0).astype(BF16) for gi in range(SGU_GROUPS)]
    w_in = win_ref[...].astype(BF16)
    uvs = [_dot(_rms(x_ref[r:r + SGU_ROWS], g_ref[...]).astype(BF16), w_in) + bin_ref[...]
           for r in range(0, tm, SGU_ROWS)]
    for i, uv in enumerate(uvs):
        uv = 0.5 * uv * (1.0 + jnp.tanh(uv * (GELU_C + (GELU_C * 0.044715) * (uv * uv))))
        u = uv[:, :width]
        v = _rms(uv[:, width:], vn_ref[...]).astype(BF16)
        cols = []
        for gi in range(SGU_GROUPS):
            vg = jnp.concatenate([v[n * c:(n + 1) * c, gi * gd:(gi + 1) * gd] for n in range(nc)], axis=1)
            sg = _dot(ws[gi], vg)
            bias = bs_ref[gi]
            cols.append(jnp.concatenate([sg[:, n * gd:(n + 1) * gd] + bias for n in range(nc)], axis=0))
        s = jnp.concatenate(cols, axis=1)
        y_ref[i * SGU_ROWS:(i + 1) * SGU_ROWS] = (u * s).astype(y_ref.dtype)


def _sgu(x, gain, w_in, b_in, v_norm, w_s, b_s, tm=1024):
    m, d = x.shape
    width = v_norm.shape[0]
    gd = width // SGU_GROUPS
    bs = jnp.broadcast_to(b_s[:, :, None], (SGU_GROUPS, SGU_CHUNK, gd)).astype(F32)
    fixed2 = lambda i: (0, 0)
    fixed3 = lambda i: (0, 0, 0)
    return pl.pallas_call(
        _sgu_body,
        grid=(m // tm,),
        in_specs=[
            pl.BlockSpec((tm, d), lambda i: (i, 0)),
            pl.BlockSpec((1, d), fixed2),
            pl.BlockSpec(w_in.shape, fixed2),
            pl.BlockSpec((1, 2 * width), fixed2),
            pl.BlockSpec((1, width), fixed2),
            pl.BlockSpec(w_s.shape, fixed3),
            pl.BlockSpec(bs.shape, fixed3),
        ],
        out_specs=pl.BlockSpec((tm, width), lambda i: (i, 0)),
        out_shape=jax.ShapeDtypeStruct((m, width), BF16),
        compiler_params=_params("parallel"),
        name="sgu",
    )(x, gain.reshape(1, d), w_in, b_in.reshape(1, 2 * width), v_norm.reshape(1, width),
      w_s, bs)


def _gla_mixer(x, batch, seq, norm1, w_in, gw1, gw2, gb, head_norm, cast=()):
    q, k, v, og, la, *cast_out = _gla_inproj(x, norm1, w_in, gw1, gw2, gb, cast)
    return _gla_core(q, k, v, og, la, head_norm, batch, seq), cast_out


def _diff_mixer(x, batch, seq, norm1, w_in, lq1, lk1, lq2, lk2, head_norm, layer_idx, cast=()):
    qt, k, vt = _diff_inproj(x, norm1, w_in)
    o, *cast_out = _diff_attn(qt, k, vt, lq1, lk1, lq2, lk2, head_norm, batch, seq, layer_idx, cast)
    return o, cast_out


def kernel(x, l0_norm1, l0_w_in, l0_gate_w1, l0_gate_w2, l0_gate_b, l0_head_norm, l0_w_out, l0_norm2, l0_mlp_w1, l0_mlp_w2, l1_norm1, l1_w_in, l1_lambda_q1, l1_lambda_k1, l1_lambda_q2, l1_lambda_k2, l1_head_norm, l1_w_out, l1_norm2, l1_mlp_w1, l1_mlp_w2, l2_norm1, l2_w_in, l2_b_in, l2_v_norm, l2_w_s, l2_b_s, l2_w_out, l2_norm2, l2_mlp_w1, l2_mlp_w2, l3_norm1, l3_w_in, l3_gate_w1, l3_gate_w2, l3_gate_b, l3_head_norm, l3_w_out, l3_norm2, l3_mlp_w1, l3_mlp_w2, final_norm):
    batch, seq, d = x.shape
    h = x.reshape(batch * seq, d)
    o, (w1, w2) = _gla_mixer(h, batch, seq, l0_norm1, l0_w_in, l0_gate_w1, l0_gate_w2, l0_gate_b, l0_head_norm,
                             cast=(l0_mlp_w1, l0_mlp_w2))
    h = _mlp(h, o, l0_w_out, l0_norm2, w1, w2)
    later = (l1_mlp_w1, l1_mlp_w2, l2_mlp_w1, l2_mlp_w2, l3_mlp_w1, l3_mlp_w2)
    o, later = _diff_mixer(h, batch, seq, l1_norm1, l1_w_in, l1_lambda_q1, l1_lambda_k1, l1_lambda_q2, l1_lambda_k2,
                           l1_head_norm, 1, cast=later)
    h = _mlp(h, o, l1_w_out, l1_norm2, later[0], later[1])
    o = _sgu(h, l2_norm1, l2_w_in, l2_b_in, l2_v_norm, l2_w_s, l2_b_s)
    h = _mlp(h, o, l2_w_out, l2_norm2, later[2], later[3])
    o, _ = _gla_mixer(h, batch, seq, l3_norm1, l3_w_in, l3_gate_w1, l3_gate_w2, l3_gate_b, l3_head_norm)
    h = _mlp(h, o, l3_w_out, l3_norm2, later[4], later[5], final_gain=final_norm)
    return h.reshape(batch, seq, d)
```

```python
import functools
import math

import jax
import jax.numpy as jnp
from jax import lax
from jax.experimental import pallas as pl
from jax.experimental.pallas import tpu as pltpu

F32 = jnp.float32
BF16 = jnp.bfloat16

EPS = 1e-6
NEG_INF = -1e30

GLA_HEADS = 4
GLA_RANK = 16
GLA_GATE_NORM = 16.0
GLA_CHUNK = 64

DIFF_HEAD_DIM = 64
DIFF_HEADS = 8

SGU_CHUNK = 128
SGU_GROUPS = 8

LANES = 128
LOG2E = 1.4426950408889634
VMEM_LIMIT = 52 * 1024 * 1024


def _params(*sem):
    return pltpu.CompilerParams(dimension_semantics=sem, vmem_limit_bytes=VMEM_LIMIT)


def _dot(a, b):
    return jnp.dot(a, b, preferred_element_type=F32)


def _dot_nt(a, b):
    return lax.dot_general(a, b, (((1,), (1,)), ((), ())), preferred_element_type=F32)


def _dot_tn(a, b):
    return lax.dot_general(a, b, (((0,), (0,)), ((), ())), preferred_element_type=F32)


def _rms(x, g):
    return x * lax.rsqrt(jnp.mean(x * x, axis=-1, keepdims=True) + EPS) * g


BF16_ROWS = 16


def _cast_specs(weights, steps, step_of):
    in_specs, out_specs, out_shapes = [], [], []
    for w in weights:
        rows, cols = w.shape
        blk = max(BF16_ROWS, rows // steps)
        nblk = rows // blk
        index = lambda *ids, nblk=nblk: (step_of(*ids) * nblk // steps, 0)
        in_specs.append(pl.BlockSpec((blk, cols), index))
        out_specs.append(pl.BlockSpec((blk, cols), index))
        out_shapes.append(jax.ShapeDtypeStruct(w.shape, BF16))
    return in_specs, out_specs, out_shapes


def _cast_blocks(src_refs, dst_refs):
    for src, dst in zip(src_refs, dst_refs):
        dst[...] = src[...].astype(BF16)


INPROJ_GROUPS = 2


def _gla_inproj_body(n_cast, x_ref, g_ref, w_ref, gw1_ref, gw2_ref, gb_ref, *refs):
    q_ref, k_ref, v_ref, og_ref, la_ref = refs[n_cast:n_cast + 5]
    _cast_blocks(refs[:n_cast], refs[n_cast + 5:])
    dk = q_ref.shape[1]
    dv = v_ref.shape[1]
    rows = x_ref.shape[0] // INPROJ_GROUPS
    for r in range(0, x_ref.shape[0], rows):
        sl = slice(r, r + rows)
        h = _rms(x_ref[sl], g_ref[...]).astype(BF16)
        t = _dot(h, gw1_ref[...]).astype(BF16)
        q_ref[sl] = _dot(h, w_ref[:, 0:dk].astype(BF16))
        z = _dot(t, gw2_ref[...]) + gb_ref[...]
        k_ref[sl] = _dot(h, w_ref[:, dk:2 * dk].astype(BF16))
        la_ref[sl] = (jnp.minimum(z, 0.0) - jnp.log(1.0 + jnp.exp(-jnp.abs(z)))) / GLA_GATE_NORM
        v_ref[sl] = _dot(h, w_ref[:, 2 * dk:2 * dk + dv].astype(BF16)).astype(BF16)
        og_ref[sl] = _dot(h, w_ref[:, 2 * dk + dv:2 * dk + 2 * dv].astype(BF16))


def _gla_inproj(x, gain, w_in, gw1, gw2, gb, cast=(), tm=512):
    m, d = x.shape
    cast_in, cast_out, cast_shapes = _cast_specs(cast, m // tm, lambda i: i)
    dk = gw2.shape[1]
    dv = (w_in.shape[1] - 2 * dk) // 2
    gw1p = jnp.zeros((d, LANES), BF16).at[:, :GLA_RANK].set(gw1.astype(BF16))
    gw2p = jnp.zeros((LANES, dk), BF16).at[:GLA_RANK, :].set(gw2.astype(BF16))
    row = lambda i: (i, 0)
    fixed = lambda i: (0, 0)
    return pl.pallas_call(
        functools.partial(_gla_inproj_body, len(cast)),
        grid=(m // tm,),
        in_specs=[
            pl.BlockSpec((tm, d), row),
            pl.BlockSpec((1, d), fixed),
            pl.BlockSpec(w_in.shape, fixed),
            pl.BlockSpec(gw1p.shape, fixed),
            pl.BlockSpec(gw2p.shape, fixed),
            pl.BlockSpec((1, dk), fixed),
        ] + cast_in,
        out_specs=[
            pl.BlockSpec((tm, dk), row),
            pl.BlockSpec((tm, dk), row),
            pl.BlockSpec((tm, dv), row),
            pl.BlockSpec((tm, dv), row),
            pl.BlockSpec((tm, dk), row),
        ] + cast_out,
        out_shape=[
            jax.ShapeDtypeStruct((m, dk), F32),
            jax.ShapeDtypeStruct((m, dk), F32),
            jax.ShapeDtypeStruct((m, dv), BF16),
            jax.ShapeDtypeStruct((m, dv), F32),
            jax.ShapeDtypeStruct((m, dk), F32),
        ] + cast_shapes,
        compiler_params=_params("parallel"),
        name="gla_inproj",
    )(x, gain.reshape(1, d), w_in, gw1p, gw2p, gb.reshape(1, dk), *cast)


def _gla_body(tri_ref, q_ref, k_ref, v_ref, og_ref, la_ref, hn_ref, o_ref, st_ref):
    c = GLA_CHUNK
    tc, hk = q_ref.shape
    n_chunks = tc // c

    @pl.when(pl.program_id(2) == 0)
    def _():
        st_ref[...] = jnp.zeros_like(st_ref)

    tri = tri_ref[...]
    tg = tri.shape[0]
    la = la_ref[...]
    la_hi = la.astype(BF16)
    la_lo = (la - la_hi.astype(F32)).astype(BF16)
    b = jnp.concatenate([_dot(tri, la_hi[r:r + tg]) + _dot(tri, la_lo[r:r + tg]) for r in range(0, tc, tg)],
                        axis=0)
    b_lasts = [b[(n + 1) * c - 1:(n + 1) * c, :] for n in range(n_chunks)]
    b_end = jnp.concatenate([jnp.broadcast_to(bl, (c, hk)) for bl in b_lasts], axis=0)

    q = q_ref[...] * (hk ** -0.5)
    k = k_ref[...]
    q_dec = (q * jnp.exp(b)).astype(BF16)
    k_inv = (k * jnp.exp(-b)).astype(BF16)
    k_end = (k * jnp.exp(b_end - b)).astype(BF16)

    row = lax.broadcasted_iota(jnp.int32, (c, c), 0)
    col = lax.broadcasted_iota(jnp.int32, (c, c), 1)
    causal = col <= row
    chunk = lambda a, n: a[n * c:(n + 1) * c]

    atts = [jnp.where(causal, _dot_nt(chunk(q_dec, n), chunk(k_inv, n)), 0.0).astype(BF16)
            for n in range(n_chunks)]
    kvs = [_dot_tn(v_ref[n * c:(n + 1) * c, :], chunk(k_end, n)) for n in range(n_chunks)]
    intra = [_dot(atts[n], v_ref[n * c:(n + 1) * c, :]) for n in range(n_chunks)]

    st = st_ref[...]
    outs = []
    for n in range(n_chunks):
        outs.append(intra[n] + _dot_nt(chunk(q_dec, n), st.astype(BF16)))
        st = jnp.exp(b_lasts[n]) * st + kvs[n]
    st_ref[...] = st

    o = jnp.concatenate(outs, axis=0)
    o = o * lax.rsqrt(jnp.mean(o * o, axis=-1, keepdims=True) + EPS) * hn_ref[...]
    g = og_ref[...]
    o_ref[...] = (o * (g / (1.0 + jnp.exp(-g)))).astype(o_ref.dtype)


GLA_CUMSUM_ROWS = 256


def _gla_core(q, k, v, og, la, head_norm, batch, seq, tc=2048):
    tc = min(tc, seq)
    m, dk = q.shape
    dv = v.shape[1]
    hk = dk // GLA_HEADS
    hv = dv // GLA_HEADS
    nt = seq // tc
    idx = jnp.arange(GLA_CUMSUM_ROWS)
    tri = ((idx[:, None] // GLA_CHUNK == idx[None, :] // GLA_CHUNK)
           & (idx[None, :] <= idx[:, None])).astype(BF16)
    blk = lambda b, h, t: (b * nt + t, h)
    return pl.pallas_call(
        _gla_body,
        grid=(batch, GLA_HEADS, nt),
        in_specs=[
            pl.BlockSpec(tri.shape, lambda b, h, t: (0, 0)),
            pl.BlockSpec((tc, hk), blk),
            pl.BlockSpec((tc, hk), blk),
            pl.BlockSpec((tc, hv), blk),
            pl.BlockSpec((tc, hv), blk),
            pl.BlockSpec((tc, hk), blk),
            pl.BlockSpec((1, hv), lambda b, h, t: (0, 0)),
        ],
        out_specs=pl.BlockSpec((tc, hv), blk),
        out_shape=jax.ShapeDtypeStruct((m, dv), BF16),
        scratch_shapes=[pltpu.VMEM((hv, hk), F32)],
        compiler_params=_params("parallel", "parallel", "arbitrary"),
        name="gla_core",
    )(tri, q, k, v, og, la, head_norm.reshape(1, hv))


MLP_INIT_GROUPS = 4
MLP_FF_CHUNK = 512


def _mlp_body(final, x_ref, o_ref, wo_ref, g_ref, w1_ref, w2_ref, gf_ref, y_ref, h_scr):
    tm = x_ref.shape[0]
    rows = tm // MLP_INIT_GROUPS
    wo = wo_ref[...].astype(BF16)
    for r in range(0, tm, rows):
        x1 = x_ref[r:r + rows] + _dot(o_ref[r:r + rows], wo)
        y_ref[r:r + rows] = x1
        h_scr[r:r + rows] = _rms(x1, g_ref[...]).astype(BF16)
    for c in range(0, w1_ref.shape[1], MLP_FF_CHUNK):
        a = jnp.maximum(_dot(h_scr[...], w1_ref[:, c:c + MLP_FF_CHUNK]), 0.0)
        y_ref[...] += _dot((a * a).astype(BF16), w2_ref[c:c + MLP_FF_CHUNK, :])
    if final:
        y_ref[...] = _rms(y_ref[...], gf_ref[...])


def _mlp(x, o, w_out, gain, w1, w2, final_gain=None, tm=1024):
    m, d = x.shape
    kdim = o.shape[1]
    final = final_gain is not None
    gf = (final_gain if final else gain).reshape(1, d)
    resident = lambda a: pl.BlockSpec(a.shape, lambda i: (0, 0), pipeline_mode=pl.Buffered(1))
    return pl.pallas_call(
        functools.partial(_mlp_body, final),
        grid=(m // tm,),
        in_specs=[
            pl.BlockSpec((tm, d), lambda i: (i, 0)),
            pl.BlockSpec((tm, kdim), lambda i: (i, 0)),
            resident(w_out),
            pl.BlockSpec((1, d), lambda i: (0, 0)),
            resident(w1),
            resident(w2),
            pl.BlockSpec((1, d), lambda i: (0, 0)),
        ],
        out_specs=pl.BlockSpec((tm, d), lambda i: (i, 0)),
        out_shape=jax.ShapeDtypeStruct((m, d), F32),
        scratch_shapes=[pltpu.VMEM((tm, d), BF16)],
        compiler_params=_params("parallel"),
        name="mlp",
    )(x, o, w_out, gain.reshape(1, d), w1, w2, gf)


Q_BLOCK = 256
KEY_BLOCK = 512


def _diff_inproj_body(x_ref, g_ref, wqt_ref, wk_ref, wvt_ref, qt_ref, k_ref, vt_ref):
    tq = qt_ref.shape[2]
    tk = vt_ref.shape[2]
    for r in range(0, x_ref.shape[0], tq):
        h = _rms(x_ref[r:r + tq], g_ref[...]).astype(BF16)
        k_ref[r:r + tq] = _dot(h, wk_ref[...]).astype(BF16)
        qt = _dot_nt(wqt_ref[...], h) * (LOG2E * DIFF_HEAD_DIM ** -0.5)
        qt_ref[r // tq] = qt.astype(BF16)
        vt_ref[r // tk, :, r % tk:r % tk + tq] = _dot_nt(wvt_ref[...], h).astype(BF16)


def _diff_inproj(x, gain, w_in, tm=512):
    m, d = x.shape
    dq = 2 * DIFF_HEADS * DIFF_HEAD_DIM
    wqt = w_in[:, :dq].T.astype(BF16)
    wk = w_in[:, dq:2 * dq].astype(BF16)
    wvt = w_in[:, 2 * dq:].T.astype(BF16)
    dv = wvt.shape[0]
    tq, tk = Q_BLOCK, KEY_BLOCK
    fixed = lambda i: (0, 0)
    return pl.pallas_call(
        _diff_inproj_body,
        grid=(m // tm,),
        in_specs=[
            pl.BlockSpec((tm, d), lambda i: (i, 0)),
            pl.BlockSpec((1, d), fixed),
            pl.BlockSpec(wqt.shape, fixed),
            pl.BlockSpec(wk.shape, fixed),
            pl.BlockSpec(wvt.shape, fixed),
        ],
        out_specs=[
            pl.BlockSpec((tm // tq, dq, tq), lambda i: (i, 0, 0)),
            pl.BlockSpec((tm, dq), lambda i: (i, 0)),
            pl.BlockSpec((tm // tk, dv, tk), lambda i: (i, 0, 0)),
        ],
        out_shape=[
            jax.ShapeDtypeStruct((m // tq, dq, tq), BF16),
            jax.ShapeDtypeStruct((m, dq), BF16),
            jax.ShapeDtypeStruct((m // tk, dv, tk), BF16),
        ],
        compiler_params=_params("parallel"),
        name="diff_inproj",
    )(x, gain.reshape(1, d), wqt, wk, wvt)


ATT_HEADS_PER_STEP = 4
ATT_STAGES_PER_TRIP = 4
KEY_SPLIT = 32
ONES_ROWS = 16
FEAT_ROWS = 8


def _bf16_pieces(c):
    c1 = c.astype(BF16).astype(F32)
    c2 = (c - c1).astype(BF16).astype(F32)
    c3 = (c - c1 - c2).astype(BF16).astype(F32)
    return c1, c2, c3


def _diff_attn_body(lam_init, n_cast, slopes_ref, qt_ref, k_ref, vt_ref, lq1_ref, lk1_ref, lq2_ref, lk2_ref,
                    hn_ref, *refs):
    o_ref = refs[n_cast]
    m_scr, acc_scr, s_scr, mx_scr = refs[2 * n_cast + 1:]
    tq, tk = Q_BLOCK, KEY_BLOCK
    hd = 2 * DIFF_HEAD_DIM
    nh = ATT_HEADS_PER_STEP
    hg = pl.program_id(1)
    n_full = pl.program_id(2)

    def tri_mask(t):
        key = lax.broadcasted_iota(jnp.int32, (tq, 2 * tq), 0)
        qry = lax.broadcasted_iota(jnp.int32, (tq, 2 * tq), 1) % tq
        return jnp.where(key <= qry, t, NEG_INF)

    kj = lax.broadcasted_iota(jnp.int32, (tk, LANES), 0)
    kl = lax.broadcasted_iota(jnp.int32, (tk, LANES), 1)
    kfeat = jnp.where(kl < 3, kj // KEY_SPLIT, jnp.where(kl < 6, kj % KEY_SPLIT, 0)).astype(F32).astype(BF16)
    ones = jnp.ones((ONES_ROWS, tk), BF16)
    half = lax.broadcasted_iota(jnp.int32, (hd, tq), 0) < DIFF_HEAD_DIM
    frow = lax.broadcasted_iota(jnp.int32, (FEAT_ROWS, 2 * tq), 0)

    def make_heads(slot):
        heads = []
        for j in range(nh):
            slope = slopes_ref[hg * nh + j] * LOG2E
            qt = qt_ref[slot, j * hd:(j + 1) * hd, :]
            zero = jnp.zeros_like(qt)
            q2t = jnp.concatenate([jnp.where(half, qt, zero), jnp.where(half, zero, qt)], axis=1)
            c1, c2, c3 = _bf16_pieces(jnp.full(frow.shape, slope, F32))
            s = float(KEY_SPLIT)
            qfeat = jnp.where(frow == 0, s * c1, jnp.where(frow == 1, s * c2, jnp.where(frow == 2, s * c3,
                    jnp.where(frow == 3, c1, jnp.where(frow == 4, c2, jnp.where(frow == 5, c3, 0.0))))))
            qfeat = jnp.concatenate([qfeat, jnp.zeros((hd - FEAT_ROWS, 2 * tq), F32)], axis=0).astype(BF16)
            heads.append((slope, jnp.concatenate([q2t, qfeat], axis=0)))
        return heads

    def scores(heads, j, kb, rows=tk):
        k_blk = k_ref[pl.ds(pl.multiple_of(kb * tk, tk), rows), j * hd:(j + 1) * hd]
        return _dot(jnp.concatenate([k_blk, kfeat[:rows]], axis=1), heads[j][1])

    def stage(j, t):
        s_scr[j, :t.shape[0]] = t
        mx_scr[j] = jnp.max(t, axis=0, keepdims=True)

    def consume(heads, slot, j, kb, rows=tk):
        q_tile = 2 * n_full + slot
        shift = heads[j][0] * (kb * tk - q_tile * tq).astype(F32)
        m_old = m_scr[slot, j]
        m_new = jnp.maximum(m_old, mx_scr[j] + shift)
        alpha = jnp.exp2(m_old - m_new)
        p = jnp.exp2(s_scr[j, :rows] - (m_new - shift)).astype(BF16)
        vt_aug = jnp.concatenate([vt_ref[kb, j * hd:(j + 1) * hd, :rows], ones[:, :rows]], axis=0)
        acc_scr[slot, j] = alpha * acc_scr[slot, j] + _dot(vt_aug, p)
        m_scr[slot, j] = m_new

    def pipelined(heads, slot, kb, next_tile):
        for j in range(nh):
            t_next = next_tile(j)
            consume(heads, slot, j, kb)
            stage(j, t_next)

    def plain_stages(heads, slot):
        n_plain = jnp.maximum(n_full - 1, 0)

        def stages(first, count):
            for kb in range(count):
                pipelined(heads, slot, first + kb, lambda j, kb=kb: scores(heads, j, first + kb + 1))

        def trip(i, carry):
            stages(ATT_STAGES_PER_TRIP * i, ATT_STAGES_PER_TRIP)
            return carry

        lax.fori_loop(0, n_plain // ATT_STAGES_PER_TRIP, trip, 0)
        done = n_plain - n_plain % ATT_STAGES_PER_TRIP

        @pl.when(n_plain % ATT_STAGES_PER_TRIP >= 2)
        def _():
            stages(done, 2)

        @pl.when(n_plain % 2 == 1)
        def _():
            stages(n_plain - 1, 1)

    def finish(slot):
        lam = (jnp.exp(jnp.sum(lq1_ref[...] * lk1_ref[...], keepdims=True))
               - jnp.exp(jnp.sum(lq2_ref[...] * lk2_ref[...], keepdims=True)) + lam_init)
        for j in range(nh):
            acc = acc_scr[slot, j]
            o = acc[:hd] / acc[hd:hd + 1]
            ot = o[:, :tq] - lam * o[:, tq:]
            ot = ot * lax.rsqrt(jnp.mean(ot * ot, axis=0, keepdims=True) + EPS)
            ot = ot * hn_ref[...] * (1.0 - lam_init)
            o_ref[slot * tq:(slot + 1) * tq, j * hd:(j + 1) * hd] = ot.T.astype(o_ref.dtype)

    even, odd = make_heads(0), make_heads(1)
    last_even = lambda j: tri_mask(scores(even, j, n_full, rows=tq))

    def last_odd(j):
        t = scores(odd, j, n_full)
        return jnp.concatenate([t[:tq], tri_mask(t[tq:])], axis=0)

    m_scr[...] = jnp.full_like(m_scr, NEG_INF)
    acc_scr[...] = jnp.zeros_like(acc_scr)


    @pl.when(n_full == 0)
    def _():
        for j in range(nh):
            stage(j, last_even(j))

    @pl.when(n_full > 0)
    def _():
        for j in range(nh):
            stage(j, scores(even, j, 0))

    plain_stages(even, 0)

    @pl.when(n_full > 0)
    def _():
        pipelined(even, 0, n_full - 1, last_even)

    @pl.when(n_full == 0)
    def _():
        for j in range(nh):
            t_next = last_odd(j)
            consume(even, 0, j, n_full, rows=tq)
            stage(j, t_next)

    @pl.when(n_full > 0)
    def _():
        for j in range(nh):
            t_next = scores(odd, j, 0)
            consume(even, 0, j, n_full, rows=tq)
            stage(j, t_next)

    plain_stages(odd, 1)

    @pl.when(n_full > 0)
    def _():
        pipelined(odd, 1, n_full - 1, last_odd)

    _cast_blocks(refs[:n_cast], refs[n_cast + 1:2 * n_cast + 1])
    finish(0)
    for j in range(nh):
        consume(odd, 1, j, n_full)
    finish(1)


def _diff_attn(qt, k, vt, lq1, lk1, lq2, lk2, head_norm, batch, seq, layer_idx, cast=()):
    tq, tk = Q_BLOCK, KEY_BLOCK
    nh = ATT_HEADS_PER_STEP
    m, dq = k.shape
    hd = 2 * DIFF_HEAD_DIM
    assert tk == 2 * tq
    nq = seq // tk
    ng = DIFF_HEADS // nh
    cast_in, cast_out, cast_shapes = _cast_specs(cast, batch * ng * nq, lambda b, h, i: (b * ng + h) * nq + i)
    lam_init = 0.8 - 0.6 * math.exp(-0.3 * layer_idx)
    slopes = 2.0 ** (-8.0 * jnp.arange(1, DIFF_HEADS + 1, dtype=F32) / DIFF_HEADS)
    vec = lambda a: a.reshape(1, DIFF_HEAD_DIM).astype(F32)
    lam_spec = pl.BlockSpec((1, DIFF_HEAD_DIM), lambda b, h, i: (0, 0))
    return pl.pallas_call(
        functools.partial(_diff_attn_body, lam_init, len(cast)),
        grid=(batch, ng, nq),
        in_specs=[
            pl.BlockSpec(memory_space=pltpu.SMEM),
            pl.BlockSpec((2, nh * hd, tq), lambda b, h, i: (b * nq + i, h, 0)),
            pl.BlockSpec((seq, nh * hd), lambda b, h, i: (b, h)),
            pl.BlockSpec((seq // tk, nh * hd, tk), lambda b, h, i: (b, h, 0)),
            lam_spec, lam_spec, lam_spec, lam_spec,
            pl.BlockSpec((hd, 1), lambda b, h, i: (0, 0)),
        ] + cast_in,
        out_specs=[pl.BlockSpec((tk, nh * hd), lambda b, h, i: (b * nq + i, h))] + cast_out,
        out_shape=[jax.ShapeDtypeStruct((m, dq), BF16)] + cast_shapes,
        scratch_shapes=[
            pltpu.VMEM((2, nh, 1, 2 * tq), F32),
            pltpu.VMEM((2, nh, hd + ONES_ROWS, 2 * tq), F32),
            pltpu.VMEM((nh, tk, 2 * tq), F32),
            pltpu.VMEM((nh, 1, 2 * tq), F32),
        ],
        compiler_params=_params("parallel", "parallel", "arbitrary"),
        name="diff_attn",
    )(slopes, qt, k, vt, vec(lq1), vec(lk1), vec(lq2), vec(lk2), head_norm.reshape(hd, 1).astype(F32), *cast)


SGU_ROWS = 256
GELU_C = math.sqrt(2.0 / math.pi)


def _sgu_body(x_ref, g_ref, win_ref, bin_ref, vn_ref, ws_ref, bs_ref, y_ref):
    tm = x_ref.shape[0]
    width = vn_ref.shape[1]
    gd = width // SGU_GROUPS
    c = SGU_CHUNK
    nc = SGU_ROWS // c
    row = lax.broadcasted_iota(jnp.int32, (c, c), 0)
    col = lax.broadcasted_iota(jnp.int32, (c, c), 1)
    ws = [jnp.where(col <= row, ws_ref[gi], 0.0).astype(BF16) for gi in range(SGU_GROUPS)]
    w_in = win_ref[...].astype(BF16)
    uvs = [_dot(_rms(x_ref[r:r + SGU_ROWS], g_ref[...]).astype(BF16), w_in) + bin_ref[...]
           for r in range(0, tm, SGU_ROWS)]
    for i, uv in enumerate(uvs):
        uv = 0.5 * uv * (1.0 + jnp.tanh(uv * (GELU_C + (GELU_C * 0.044715) * (uv * uv))))
        u = uv[:, :width]
        v = _rms(uv[:, width:], vn_ref[...]).astype(BF16)
        cols = []
        for gi in range(SGU_GROUPS):
            vg = jnp.concatenate([v[n * c:(n + 1) * c, gi * gd:(gi + 1) * gd] for n in range(nc)], axis=1)
            sg = _dot(ws[gi], vg)
            bias = bs_ref[gi]
            cols.append(jnp.concatenate([sg[:, n * gd:(n + 1) * gd] + bias for n in range(nc)], axis=0))
        s = jnp.concatenate(cols, axis=1)
        y_ref[i * SGU_ROWS:(i + 1) * SGU_ROWS] = (u * s).astype(y_ref.dtype)


def _sgu(x, gain, w_in, b_in, v_norm, w_s, b_s, tm=1024):
    m, d = x.shape
    width = v_norm.shape[0]
    gd = width // SGU_GROUPS
    bs = jnp.broadcast_to(b_s[:, :, None], (SGU_GROUPS, SGU_CHUNK, gd)).astype(F32)
    fixed2 = lambda i: (0, 0)
    fixed3 = lambda i: (0, 0, 0)
    return pl.pallas_call(
        _sgu_body,
        grid=(m // tm,),
        in_specs=[
            pl.BlockSpec((tm, d), lambda i: (i, 0)),
            pl.BlockSpec((1, d), fixed2),
            pl.BlockSpec(w_in.shape, fixed2),
            pl.BlockSpec((1, 2 * width), fixed2),
            pl.BlockSpec((1, width), fixed2),
            pl.BlockSpec(w_s.shape, fixed3),
            pl.BlockSpec(bs.shape, fixed3),
        ],
        out_specs=pl.BlockSpec((tm, width), lambda i: (i, 0)),
        out_shape=jax.ShapeDtypeStruct((m, width), BF16),
        compiler_params=_params("parallel"),
        name="sgu",
    )(x, gain.reshape(1, d), w_in, b_in.reshape(1, 2 * width), v_norm.reshape(1, width),
      w_s, bs)


def _gla_mixer(x, batch, seq, norm1, w_in, gw1, gw2, gb, head_norm, cast=()):
    q, k, v, og, la, *cast_out = _gla_inproj(x, norm1, w_in, gw1, gw2, gb, cast)
    return _gla_core(q, k, v, og, la, head_norm, batch, seq), cast_out


def _diff_mixer(x, batch, seq, norm1, w_in, lq1, lk1, lq2, lk2, head_norm, layer_idx, cast=()):
    qt, k, vt = _diff_inproj(x, norm1, w_in)
    o, *cast_out = _diff_attn(qt, k, vt, lq1, lk1, lq2, lk2, head_norm, batch, seq, layer_idx, cast)
    return o, cast_out


def kernel(x, l0_norm1, l0_w_in, l0_gate_w1, l0_gate_w2, l0_gate_b, l0_head_norm, l0_w_out, l0_norm2, l0_mlp_w1, l0_mlp_w2, l1_norm1, l1_w_in, l1_lambda_q1, l1_lambda_k1, l1_lambda_q2, l1_lambda_k2, l1_head_norm, l1_w_out, l1_norm2, l1_mlp_w1, l1_mlp_w2, l2_norm1, l2_w_in, l2_b_in, l2_v_norm, l2_w_s, l2_b_s, l2_w_out, l2_norm2, l2_mlp_w1, l2_mlp_w2, l3_norm1, l3_w_in, l3_gate_w1, l3_gate_w2, l3_gate_b, l3_head_norm, l3_w_out, l3_norm2, l3_mlp_w1, l3_mlp_w2, final_norm):
    batch, seq, d = x.shape
    h = x.reshape(batch * seq, d)
    o, (w1, w2) = _gla_mixer(h, batch, seq, l0_norm1, l0_w_in, l0_gate_w1, l0_gate_w2, l0_gate_b, l0_head_norm,
                             cast=(l0_mlp_w1, l0_mlp_w2))
    h = _mlp(h, o, l0_w_out, l0_norm2, w1, w2)
    later = (l1_mlp_w1, l1_mlp_w2, l2_mlp_w1, l2_mlp_w2, l3_mlp_w1, l3_mlp_w2)
    o, later = _diff_mixer(h, batch, seq, l1_norm1, l1_w_in, l1_lambda_q1, l1_lambda_k1, l1_lambda_q2, l1_lambda_k2,
                           l1_head_norm, 1, cast=later)
    h = _mlp(h, o, l1_w_out, l1_norm2, later[0], later[1])
    o = _sgu(h, l2_norm1, l2_w_in, l2_b_in, l2_v_norm, l2_w_s, l2_b_s)
    h = _mlp(h, o, l2_w_out, l2_norm2, later[2], later[3])
    o, _ = _gla_mixer(h, batch, seq, l3_norm1, l3_w_in, l3_gate_w1, l3_gate_w2, l3_gate_b, l3_head_norm)
    h = _mlp(h, o, l3_w_out, l3_norm2, later[4], later[5], final_gain=final_norm)
    return h.reshape(batch, seq, d)
```

```python
import functools
import math

import jax
import jax.numpy as jnp
from jax import lax
from jax.experimental import pallas as pl
from jax.experimental.pallas import tpu as pltpu

F32 = jnp.float32
BF16 = jnp.bfloat16

EPS = 1e-6
NEG_INF = -1e30

GLA_HEADS = 4
GLA_RANK = 16
GLA_GATE_NORM = 16.0
GLA_CHUNK = 64

DIFF_HEAD_DIM = 64
DIFF_HEADS = 8

SGU_CHUNK = 128
SGU_GROUPS = 8

LANES = 128
LOG2E = 1.4426950408889634
VMEM_LIMIT = 52 * 1024 * 1024


def _params(*sem):
    return pltpu.CompilerParams(dimension_semantics=sem, vmem_limit_bytes=VMEM_LIMIT)


def _dot(a, b):
    return jnp.dot(a, b, preferred_element_type=F32)


def _dot_nt(a, b):
    return lax.dot_general(a, b, (((1,), (1,)), ((), ())), preferred_element_type=F32)


def _dot_tn(a, b):
    return lax.dot_general(a, b, (((0,), (0,)), ((), ())), preferred_element_type=F32)


def _rms(x, g):
    return x * lax.rsqrt(jnp.mean(x * x, axis=-1, keepdims=True) + EPS) * g


BF16_ROWS = 16


def _cast_specs(weights, steps, step_of):
    in_specs, out_specs, out_shapes = [], [], []
    for w in weights:
        rows, cols = w.shape
        blk = max(BF16_ROWS, rows // steps)
        nblk = rows // blk
        index = lambda *ids, nblk=nblk: (step_of(*ids) * nblk // steps, 0)
        in_specs.append(pl.BlockSpec((blk, cols), index))
        out_specs.append(pl.BlockSpec((blk, cols), index))
        out_shapes.append(jax.ShapeDtypeStruct(w.shape, BF16))
    return in_specs, out_specs, out_shapes


def _cast_blocks(src_refs, dst_refs):
    for src, dst in zip(src_refs, dst_refs):
        dst[...] = src[...].astype(BF16)


INPROJ_GROUPS = 2


def _gla_inproj_body(n_cast, x_ref, g_ref, w_ref, gw1_ref, gw2_ref, gb_ref, *refs):
    q_ref, k_ref, v_ref, og_ref, la_ref = refs[n_cast:n_cast + 5]
    w_scr = refs[2 * n_cast + 5]
    _cast_blocks(refs[:n_cast], refs[n_cast + 5:2 * n_cast + 5])
    dk = q_ref.shape[1]
    dv = v_ref.shape[1]

    @pl.when(pl.program_id(0) == 0)
    def _():
        for c0 in range(0, w_ref.shape[1], dk):
            w_scr[:, c0:c0 + dk] = w_ref[:, c0:c0 + dk].astype(BF16)

    rows = x_ref.shape[0] // INPROJ_GROUPS
    for r in range(0, x_ref.shape[0], rows):
        sl = slice(r, r + rows)
        h = _rms(x_ref[sl], g_ref[...]).astype(BF16)
        t = _dot(h, gw1_ref[...]).astype(BF16)
        q_ref[sl] = _dot(h, w_scr[:, 0:dk])
        z = _dot(t, gw2_ref[...]) + gb_ref[...]
        k_ref[sl] = _dot(h, w_scr[:, dk:2 * dk])
        la_ref[sl] = (jnp.minimum(z, 0.0) - jnp.log(1.0 + jnp.exp(-jnp.abs(z)))) / GLA_GATE_NORM
        v_ref[sl] = _dot(h, w_scr[:, 2 * dk:2 * dk + dv]).astype(BF16)
        og_ref[sl] = _dot(h, w_scr[:, 2 * dk + dv:2 * dk + 2 * dv])


def _gla_inproj(x, gain, w_in, gw1, gw2, gb, cast=(), tm=512):
    m, d = x.shape
    cast_in, cast_out, cast_shapes = _cast_specs(cast, m // tm, lambda i: i)
    dk = gw2.shape[1]
    dv = (w_in.shape[1] - 2 * dk) // 2
    gw1p = jnp.zeros((d, LANES), BF16).at[:, :GLA_RANK].set(gw1.astype(BF16))
    gw2p = jnp.zeros((LANES, dk), BF16).at[:GLA_RANK, :].set(gw2.astype(BF16))
    row = lambda i: (i, 0)
    fixed = lambda i: (0, 0)
    return pl.pallas_call(
        functools.partial(_gla_inproj_body, len(cast)),
        grid=(m // tm,),
        in_specs=[
            pl.BlockSpec((tm, d), row),
            pl.BlockSpec((1, d), fixed),
            pl.BlockSpec(w_in.shape, fixed, pipeline_mode=pl.Buffered(1)),
            pl.BlockSpec(gw1p.shape, fixed),
            pl.BlockSpec(gw2p.shape, fixed),
            pl.BlockSpec((1, dk), fixed),
        ] + cast_in,
        out_specs=[
            pl.BlockSpec((tm, dk), row),
            pl.BlockSpec((tm, dk), row),
            pl.BlockSpec((tm, dv), row),
            pl.BlockSpec((tm, dv), row),
            pl.BlockSpec((tm, dk), row),
        ] + cast_out,
        out_shape=[
            jax.ShapeDtypeStruct((m, dk), F32),
            jax.ShapeDtypeStruct((m, dk), F32),
            jax.ShapeDtypeStruct((m, dv), BF16),
            jax.ShapeDtypeStruct((m, dv), F32),
            jax.ShapeDtypeStruct((m, dk), F32),
        ] + cast_shapes,
        scratch_shapes=[pltpu.VMEM(w_in.shape, BF16)],
        compiler_params=_params("arbitrary"),
        name="gla_inproj",
    )(x, gain.reshape(1, d), w_in, gw1p, gw2p, gb.reshape(1, dk), *cast)


def _gla_body(tri_ref, q_ref, k_ref, v_ref, og_ref, la_ref, hn_ref, o_ref, st_ref):
    c = GLA_CHUNK
    tc, hk = q_ref.shape
    n_chunks = tc // c

    @pl.when(pl.program_id(2) == 0)
    def _():
        st_ref[...] = jnp.zeros_like(st_ref)

    tri = tri_ref[...]
    tg = tri.shape[0]
    la = la_ref[...]
    la_hi = la.astype(BF16)
    la_lo = (la - la_hi.astype(F32)).astype(BF16)
    b = jnp.concatenate([_dot(tri, la_hi[r:r + tg]) + _dot(tri, la_lo[r:r + tg]) for r in range(0, tc, tg)],
                        axis=0)
    b_lasts = [b[(n + 1) * c - 1:(n + 1) * c, :] for n in range(n_chunks)]
    b_end = jnp.concatenate([jnp.broadcast_to(bl, (c, hk)) for bl in b_lasts], axis=0)

    q = q_ref[...] * (hk ** -0.5)
    k = k_ref[...]
    q_dec = (q * jnp.exp(b)).astype(BF16)
    k_inv = (k * jnp.exp(-b)).astype(BF16)
    k_end = (k * jnp.exp(b_end - b)).astype(BF16)

    row = lax.broadcasted_iota(jnp.int32, (c, c), 0)
    col = lax.broadcasted_iota(jnp.int32, (c, c), 1)
    causal = col <= row
    chunk = lambda a, n: a[n * c:(n + 1) * c]

    atts = [jnp.where(causal, _dot_nt(chunk(q_dec, n), chunk(k_inv, n)), 0.0).astype(BF16)
            for n in range(n_chunks)]
    kvs = [_dot_tn(v_ref[n * c:(n + 1) * c, :], chunk(k_end, n)) for n in range(n_chunks)]
    intra = [_dot(atts[n], v_ref[n * c:(n + 1) * c, :]) for n in range(n_chunks)]

    st = st_ref[...]
    outs = []
    for n in range(n_chunks):
        outs.append(intra[n] + _dot_nt(chunk(q_dec, n), st.astype(BF16)))
        st = jnp.exp(b_lasts[n]) * st + kvs[n]
    st_ref[...] = st

    o = jnp.concatenate(outs, axis=0)
    o = o * lax.rsqrt(jnp.mean(o * o, axis=-1, keepdims=True) + EPS) * hn_ref[...]
    g = og_ref[...]
    o_ref[...] = (o * (g / (1.0 + jnp.exp(-g)))).astype(o_ref.dtype)


GLA_CUMSUM_ROWS = 256


def _gla_core(q, k, v, og, la, head_norm, batch, seq, tc=2048):
    tc = min(tc, seq)
    m, dk = q.shape
    dv = v.shape[1]
    hk = dk // GLA_HEADS
    hv = dv // GLA_HEADS
    nt = seq // tc
    idx = jnp.arange(GLA_CUMSUM_ROWS)
    tri = ((idx[:, None] // GLA_CHUNK == idx[None, :] // GLA_CHUNK)
           & (idx[None, :] <= idx[:, None])).astype(BF16)
    blk = lambda b, h, t: (b * nt + t, h)
    return pl.pallas_call(
        _gla_body,
        grid=(batch, GLA_HEADS, nt),
        in_specs=[
            pl.BlockSpec(tri.shape, lambda b, h, t: (0, 0)),
            pl.BlockSpec((tc, hk), blk),
            pl.BlockSpec((tc, hk), blk),
            pl.BlockSpec((tc, hv), blk),
            pl.BlockSpec((tc, hv), blk),
            pl.BlockSpec((tc, hk), blk),
            pl.BlockSpec((1, hv), lambda b, h, t: (0, 0)),
        ],
        out_specs=pl.BlockSpec((tc, hv), blk),
        out_shape=jax.ShapeDtypeStruct((m, dv), BF16),
        scratch_shapes=[pltpu.VMEM((hv, hk), F32)],
        compiler_params=_params("parallel", "parallel", "arbitrary"),
        name="gla_core",
    )(tri, q, k, v, og, la, head_norm.reshape(1, hv))


MLP_INIT_GROUPS = 4
MLP_FF_CHUNK = 512


def _mlp_body(final, x_ref, o_ref, wo_ref, g_ref, w1_ref, w2_ref, gf_ref, y_ref, h_scr):
    tm = x_ref.shape[0]
    rows = tm // MLP_INIT_GROUPS
    wo = wo_ref[...].astype(BF16)
    for r in range(0, tm, rows):
        x1 = x_ref[r:r + rows] + _dot(o_ref[r:r + rows], wo)
        y_ref[r:r + rows] = x1
        h_scr[r:r + rows] = _rms(x1, g_ref[...]).astype(BF16)
    for c in range(0, w1_ref.shape[1], MLP_FF_CHUNK):
        a = jnp.maximum(_dot(h_scr[...], w1_ref[:, c:c + MLP_FF_CHUNK]), 0.0)
        y_ref[...] += _dot((a * a).astype(BF16), w2_ref[c:c + MLP_FF_CHUNK, :])
    if final:
        y_ref[...] = _rms(y_ref[...], gf_ref[...])


def _mlp(x, o, w_out, gain, w1, w2, final_gain=None, tm=1024):
    m, d = x.shape
    kdim = o.shape[1]
    final = final_gain is not None
    gf = (final_gain if final else gain).reshape(1, d)
    resident = lambda a: pl.BlockSpec(a.shape, lambda i: (0, 0), pipeline_mode=pl.Buffered(1))
    return pl.pallas_call(
        functools.partial(_mlp_body, final),
        grid=(m // tm,),
        in_specs=[
            pl.BlockSpec((tm, d), lambda i: (i, 0)),
            pl.BlockSpec((tm, kdim), lambda i: (i, 0)),
            resident(w_out),
            pl.BlockSpec((1, d), lambda i: (0, 0)),
            resident(w1),
            resident(w2),
            pl.BlockSpec((1, d), lambda i: (0, 0)),
        ],
        out_specs=pl.BlockSpec((tm, d), lambda i: (i, 0)),
        out_shape=jax.ShapeDtypeStruct((m, d), F32),
        scratch_shapes=[pltpu.VMEM((tm, d), BF16)],
        compiler_params=_params("parallel"),
        name="mlp",
    )(x, o, w_out, gain.reshape(1, d), w1, w2, gf)


Q_BLOCK = 256
KEY_BLOCK = 512


def _diff_inproj_body(x_ref, g_ref, wqt_ref, wk_ref, wvt_ref, qt_ref, k_ref, vt_ref):
    tq = qt_ref.shape[2]
    tk = vt_ref.shape[2]
    for r in range(0, x_ref.shape[0], tq):
        h = _rms(x_ref[r:r + tq], g_ref[...]).astype(BF16)
        k_ref[r:r + tq] = _dot(h, wk_ref[...]).astype(BF16)
        qt = _dot_nt(wqt_ref[...], h) * (LOG2E * DIFF_HEAD_DIM ** -0.5)
        qt_ref[r // tq] = qt.astype(BF16)
        vt_ref[r // tk, :, r % tk:r % tk + tq] = _dot_nt(wvt_ref[...], h).astype(BF16)


def _diff_inproj(x, gain, w_in, tm=512):
    m, d = x.shape
    dq = 2 * DIFF_HEADS * DIFF_HEAD_DIM
    wqt = w_in[:, :dq].T.astype(BF16)
    wk = w_in[:, dq:2 * dq].astype(BF16)
    wvt = w_in[:, 2 * dq:].T.astype(BF16)
    dv = wvt.shape[0]
    tq, tk = Q_BLOCK, KEY_BLOCK
    fixed = lambda i: (0, 0)
    return pl.pallas_call(
        _diff_inproj_body,
        grid=(m // tm,),
        in_specs=[
            pl.BlockSpec((tm, d), lambda i: (i, 0)),
            pl.BlockSpec((1, d), fixed),
            pl.BlockSpec(wqt.shape, fixed),
            pl.BlockSpec(wk.shape, fixed),
            pl.BlockSpec(wvt.shape, fixed),
        ],
        out_specs=[
            pl.BlockSpec((tm // tq, dq, tq), lambda i: (i, 0, 0)),
            pl.BlockSpec((tm, dq), lambda i: (i, 0)),
            pl.BlockSpec((tm // tk, dv, tk), lambda i: (i, 0, 0)),
        ],
        out_shape=[
            jax.ShapeDtypeStruct((m // tq, dq, tq), BF16),
            jax.ShapeDtypeStruct((m, dq), BF16),
            jax.ShapeDtypeStruct((m // tk, dv, tk), BF16),
        ],
        compiler_params=_params("parallel"),
        name="diff_inproj",
    )(x, gain.reshape(1, d), wqt, wk, wvt)


ATT_HEADS_PER_STEP = 4
ATT_STAGES_PER_TRIP = 4
KEY_SPLIT = 32
ONES_ROWS = 16
FEAT_ROWS = 8


def _bf16_pieces(c):
    c1 = c.astype(BF16).astype(F32)
    c2 = (c - c1).astype(BF16).astype(F32)
    c3 = (c - c1 - c2).astype(BF16).astype(F32)
    return c1, c2, c3


def _diff_attn_body(lam_init, n_cast, slopes_ref, qt_ref, k_ref, vt_ref, lq1_ref, lk1_ref, lq2_ref, lk2_ref,
                    hn_ref, *refs):
    o_ref = refs[n_cast]
    m_scr, acc_scr, s_scr, mx_scr = refs[2 * n_cast + 1:]
    tq, tk = Q_BLOCK, KEY_BLOCK
    hd = 2 * DIFF_HEAD_DIM
    nh = ATT_HEADS_PER_STEP
    hg = pl.program_id(1)
    n_full = pl.program_id(2)

    def tri_mask(t):
        key = lax.broadcasted_iota(jnp.int32, (tq, 2 * tq), 0)
        qry = lax.broadcasted_iota(jnp.int32, (tq, 2 * tq), 1) % tq
        return jnp.where(key <= qry, t, NEG_INF)

    kj = lax.broadcasted_iota(jnp.int32, (tk, LANES), 0)
    kl = lax.broadcasted_iota(jnp.int32, (tk, LANES), 1)
    kfeat = jnp.where(kl < 3, kj // KEY_SPLIT, jnp.where(kl < 6, kj % KEY_SPLIT, 0)).astype(F32).astype(BF16)
    ones = jnp.ones((ONES_ROWS, tk), BF16)
    half = lax.broadcasted_iota(jnp.int32, (hd, tq), 0) < DIFF_HEAD_DIM
    frow = lax.broadcasted_iota(jnp.int32, (FEAT_ROWS, 2 * tq), 0)

    slopes, qfeats = [], []
    for j in range(nh):
        slope = slopes_ref[hg * nh + j] * LOG2E
        c1, c2, c3 = _bf16_pieces(jnp.full(frow.shape, slope, F32))
        s = float(KEY_SPLIT)
        qfeat = jnp.where(frow == 0, s * c1, jnp.where(frow == 1, s * c2, jnp.where(frow == 2, s * c3,
                jnp.where(frow == 3, c1, jnp.where(frow == 4, c2, jnp.where(frow == 5, c3, 0.0))))))
        slopes.append(slope)
        qfeats.append(jnp.concatenate([qfeat, jnp.zeros((hd - FEAT_ROWS, 2 * tq), F32)], axis=0).astype(BF16))

    def make_heads(slot):
        heads = []
        for j in range(nh):
            qt = qt_ref[slot, j * hd:(j + 1) * hd, :]
            zero = jnp.zeros_like(qt)
            q2t = jnp.concatenate([jnp.where(half, qt, zero), jnp.where(half, zero, qt)], axis=1)
            heads.append((slopes[j], jnp.concatenate([q2t, qfeats[j]], axis=0)))
        return heads

    def scores(heads, j, kb, rows=tk):
        k_blk = k_ref[pl.ds(pl.multiple_of(kb * tk, tk), rows), j * hd:(j + 1) * hd]
        return _dot(jnp.concatenate([k_blk, kfeat[:rows]], axis=1), heads[j][1])

    def stage(j, t):
        s_scr[j, :t.shape[0]] = t
        mx_scr[j] = jnp.max(t, axis=0, keepdims=True)

    def consume(heads, slot, j, kb, rows=tk):
        q_tile = 2 * n_full + slot
        shift = heads[j][0] * (kb * tk - q_tile * tq).astype(F32)
        m_old = m_scr[slot, j]
        m_new = jnp.maximum(m_old, mx_scr[j] + shift)
        alpha = jnp.exp2(m_old - m_new)
        p = jnp.exp2(s_scr[j, :rows] - (m_new - shift)).astype(BF16)
        vt_aug = jnp.concatenate([vt_ref[kb, j * hd:(j + 1) * hd, :rows], ones[:, :rows]], axis=0)
        acc_scr[slot, j] = alpha * acc_scr[slot, j] + _dot(vt_aug, p)
        m_scr[slot, j] = m_new

    def pipelined(heads, slot, kb, next_tile):
        for j in range(nh):
            t_next = next_tile(j)
            consume(heads, slot, j, kb)
            stage(j, t_next)

    def plain_stages(heads, slot):
        n_plain = jnp.maximum(n_full - 1, 0)

        def stages(first, count):
            for kb in range(count):
                pipelined(heads, slot, first + kb, lambda j, kb=kb: scores(heads, j, first + kb + 1))

        def trip(i, carry):
            stages(ATT_STAGES_PER_TRIP * i, ATT_STAGES_PER_TRIP)
            return carry

        lax.fori_loop(0, n_plain // ATT_STAGES_PER_TRIP, trip, 0)
        done = n_plain - n_plain % ATT_STAGES_PER_TRIP

        @pl.when(n_plain % ATT_STAGES_PER_TRIP >= 2)
        def _():
            stages(done, 2)

        @pl.when(n_plain % 2 == 1)
        def _():
            stages(n_plain - 1, 1)

    def finish(slot):
        lam = (jnp.exp(jnp.sum(lq1_ref[...] * lk1_ref[...], keepdims=True))
               - jnp.exp(jnp.sum(lq2_ref[...] * lk2_ref[...], keepdims=True)) + lam_init)
        for j in range(nh):
            acc = acc_scr[slot, j]
            o = acc[:hd] / acc[hd:hd + 1]
            ot = o[:, :tq] - lam * o[:, tq:]
            ot = ot * lax.rsqrt(jnp.mean(ot * ot, axis=0, keepdims=True) + EPS)
            ot = ot * hn_ref[...] * (1.0 - lam_init)
            o_ref[slot * tq:(slot + 1) * tq, j * hd:(j + 1) * hd] = ot.T.astype(o_ref.dtype)

    even, odd = make_heads(0), make_heads(1)
    last_even = lambda j: tri_mask(scores(even, j, n_full, rows=tq))

    def last_odd(j):
        t = scores(odd, j, n_full)
        return jnp.concatenate([t[:tq], tri_mask(t[tq:])], axis=0)

    def reset():
        m_scr[...] = jnp.full_like(m_scr, NEG_INF)
        acc_scr[...] = jnp.zeros_like(acc_scr)


    @pl.when(n_full == 0)
    def _():
        reset()
        for j in range(nh):
            stage(j, last_even(j))

    @pl.when(n_full > 0)
    def _():
        reset()
        for j in range(nh):
            stage(j, scores(even, j, 0))

    plain_stages(even, 0)

    @pl.when(n_full > 0)
    def _():
        pipelined(even, 0, n_full - 1, last_even)

    @pl.when(n_full == 0)
    def _():
        for j in range(nh):
            t_next = last_odd(j)
            consume(even, 0, j, n_full, rows=tq)
            stage(j, t_next)

    @pl.when(n_full > 0)
    def _():
        for j in range(nh):
            t_next = scores(odd, j, 0)
            consume(even, 0, j, n_full, rows=tq)
            stage(j, t_next)

    plain_stages(odd, 1)

    @pl.when(n_full > 0)
    def _():
        pipelined(odd, 1, n_full - 1, last_odd)

    _cast_blocks(refs[:n_cast], refs[n_cast + 1:2 * n_cast + 1])
    finish(0)
    for j in range(nh):
        consume(odd, 1, j, n_full)
    finish(1)


def _diff_attn(qt, k, vt, lq1, lk1, lq2, lk2, head_norm, batch, seq, layer_idx, cast=()):
    tq, tk = Q_BLOCK, KEY_BLOCK
    nh = ATT_HEADS_PER_STEP
    m, dq = k.shape
    hd = 2 * DIFF_HEAD_DIM
    assert tk == 2 * tq
    nq = seq // tk
    ng = DIFF_HEADS // nh
    cast_in, cast_out, cast_shapes = _cast_specs(cast, batch * ng * nq, lambda b, h, i: (b * ng + h) * nq + i)
    lam_init = 0.8 - 0.6 * math.exp(-0.3 * layer_idx)
    slopes = 2.0 ** (-8.0 * jnp.arange(1, DIFF_HEADS + 1, dtype=F32) / DIFF_HEADS)
    vec = lambda a: a.reshape(1, DIFF_HEAD_DIM).astype(F32)
    lam_spec = pl.BlockSpec((1, DIFF_HEAD_DIM), lambda b, h, i: (0, 0))
    return pl.pallas_call(
        functools.partial(_diff_attn_body, lam_init, len(cast)),
        grid=(batch, ng, nq),
        in_specs=[
            pl.BlockSpec(memory_space=pltpu.SMEM),
            pl.BlockSpec((2, nh * hd, tq), lambda b, h, i: (b * nq + i, h, 0)),
            pl.BlockSpec((seq, nh * hd), lambda b, h, i: (b, h)),
            pl.BlockSpec((seq // tk, nh * hd, tk), lambda b, h, i: (b, h, 0)),
            lam_spec, lam_spec, lam_spec, lam_spec,
            pl.BlockSpec((hd, 1), lambda b, h, i: (0, 0)),
        ] + cast_in,
        out_specs=[pl.BlockSpec((tk, nh * hd), lambda b, h, i: (b * nq + i, h))] + cast_out,
        out_shape=[jax.ShapeDtypeStruct((m, dq), BF16)] + cast_shapes,
        scratch_shapes=[
            pltpu.VMEM((2, nh, 1, 2 * tq), F32),
            pltpu.VMEM((2, nh, hd + ONES_ROWS, 2 * tq), F32),
            pltpu.VMEM((nh, tk, 2 * tq), F32),
            pltpu.VMEM((nh, 1, 2 * tq), F32),
        ],
        compiler_params=_params("parallel", "parallel", "arbitrary"),
        name="diff_attn",
    )(slopes, qt, k, vt, vec(lq1), vec(lk1), vec(lq2), vec(lk2), head_norm.reshape(hd, 1).astype(F32), *cast)


SGU_ROWS = 256
GELU_C = math.sqrt(2.0 / math.pi)


def _sgu_body(x_ref, g_ref, win_ref, bin_ref, vn_ref, ws_ref, bs_ref, y_ref):
    tm = x_ref.shape[0]
    width = vn_ref.shape[1]
    gd = width // SGU_GROUPS
    c = SGU_CHUNK
    nc = SGU_ROWS // c
    row = lax.broadcasted_iota(jnp.int32, (c, c), 0)
    col = lax.broadcasted_iota(jnp.int32, (c, c), 1)
    ws = [jnp.where(col <= row, ws_ref[gi], 0.0).astype(BF16) for gi in range(SGU_GROUPS)]
    w_in = win_ref[...].astype(BF16)
    uvs = [_dot(_rms(x_ref[r:r + SGU_ROWS], g_ref[...]).astype(BF16), w_in) + bin_ref[...]
           for r in range(0, tm, SGU_ROWS)]
    for i, uv in enumerate(uvs):
        uv = 0.5 * uv * (1.0 + jnp.tanh(uv * (GELU_C + (GELU_C * 0.044715) * (uv * uv))))
        u = uv[:, :width]
        v = _rms(uv[:, width:], vn_ref[...]).astype(BF16)
        cols = []
        for gi in range(SGU_GROUPS):
            vg = jnp.concatenate([v[n * c:(n + 1) * c, gi * gd:(gi + 1) * gd] for n in range(nc)], axis=1)
            sg = _dot(ws[gi], vg)
            bias = bs_ref[gi]
            cols.append(jnp.concatenate([sg[:, n * gd:(n + 1) * gd] + bias for n in range(nc)], axis=0))
        s = jnp.concatenate(cols, axis=1)
        y_ref[i * SGU_ROWS:(i + 1) * SGU_ROWS] = (u * s).astype(y_ref.dtype)


def _sgu(x, gain, w_in, b_in, v_norm, w_s, b_s, tm=1024):
    m, d = x.shape
    width = v_norm.shape[0]
    gd = width // SGU_GROUPS
    bs = jnp.broadcast_to(b_s[:, :, None], (SGU_GROUPS, SGU_CHUNK, gd)).astype(F32)
    fixed2 = lambda i: (0, 0)
    fixed3 = lambda i: (0, 0, 0)
    return pl.pallas_call(
        _sgu_body,
        grid=(m // tm,),
        in_specs=[
            pl.BlockSpec((tm, d), lambda i: (i, 0)),
            pl.BlockSpec((1, d), fixed2),
            pl.BlockSpec(w_in.shape, fixed2),
            pl.BlockSpec((1, 2 * width), fixed2),
            pl.BlockSpec((1, width), fixed2),
            pl.BlockSpec(w_s.shape, fixed3),
            pl.BlockSpec(bs.shape, fixed3),
        ],
        out_specs=pl.BlockSpec((tm, width), lambda i: (i, 0)),
        out_shape=jax.ShapeDtypeStruct((m, width), BF16),
        compiler_params=_params("parallel"),
        name="sgu",
    )(x, gain.reshape(1, d), w_in, b_in.reshape(1, 2 * width), v_norm.reshape(1, width),
      w_s, bs)


def _gla_mixer(x, batch, seq, norm1, w_in, gw1, gw2, gb, head_norm, cast=()):
    q, k, v, og, la, *cast_out = _gla_inproj(x, norm1, w_in, gw1, gw2, gb, cast)
    return _gla_core(q, k, v, og, la, head_norm, batch, seq), cast_out


def _diff_mixer(x, batch, seq, norm1, w_in, lq1, lk1, lq2, lk2, head_norm, layer_idx, cast=()):
    qt, k, vt = _diff_inproj(x, norm1, w_in)
    o, *cast_out = _diff_attn(qt, k, vt, lq1, lk1, lq2, lk2, head_norm, batch, seq, layer_idx, cast)
    return o, cast_out


def kernel(x, l0_norm1, l0_w_in, l0_gate_w1, l0_gate_w2, l0_gate_b, l0_head_norm, l0_w_out, l0_norm2, l0_mlp_w1, l0_mlp_w2, l1_norm1, l1_w_in, l1_lambda_q1, l1_lambda_k1, l1_lambda_q2, l1_lambda_k2, l1_head_norm, l1_w_out, l1_norm2, l1_mlp_w1, l1_mlp_w2, l2_norm1, l2_w_in, l2_b_in, l2_v_norm, l2_w_s, l2_b_s, l2_w_out, l2_norm2, l2_mlp_w1, l2_mlp_w2, l3_norm1, l3_w_in, l3_gate_w1, l3_gate_w2, l3_gate_b, l3_head_norm, l3_w_out, l3_norm2, l3_mlp_w1, l3_mlp_w2, final_norm):
    batch, seq, d = x.shape
    h = x.reshape(batch * seq, d)
    o, (w1, w2) = _gla_mixer(h, batch, seq, l0_norm1, l0_w_in, l0_gate_w1, l0_gate_w2, l0_gate_b, l0_head_norm,
                             cast=(l0_mlp_w1, l0_mlp_w2))
    h = _mlp(h, o, l0_w_out, l0_norm2, w1, w2)
    later = (l1_mlp_w1, l1_mlp_w2, l2_mlp_w1, l2_mlp_w2, l3_mlp_w1, l3_mlp_w2)
    o, later = _diff_mixer(h, batch, seq, l1_norm1, l1_w_in, l1_lambda_q1, l1_lambda_k1, l1_lambda_q2, l1_lambda_k2,
                           l1_head_norm, 1, cast=later)
    h = _mlp(h, o, l1_w_out, l1_norm2, later[0], later[1])
    o = _sgu(h, l2_norm1, l2_w_in, l2_b_in, l2_v_norm, l2_w_s, l2_b_s)
    h = _mlp(h, o, l2_w_out, l2_norm2, later[2], later[3])
    o, _ = _gla_mixer(h, batch, seq, l3_norm1, l3_w_in, l3_gate_w1, l3_gate_w2, l3_gate_b, l3_head_norm)
    h = _mlp(h, o, l3_w_out, l3_norm2, later[4], later[5], final_gain=final_norm)
    return h.reshape(batch, seq, d)
```

```python
import functools
import math

import jax
import jax.numpy as jnp
from jax import lax
from jax.experimental import pallas as pl
from jax.experimental.pallas import tpu as pltpu

F32 = jnp.float32
BF16 = jnp.bfloat16

EPS = 1e-6
NEG_INF = -1e30

GLA_HEADS = 4
GLA_RANK = 16
GLA_GATE_NORM = 16.0
GLA_CHUNK = 64

DIFF_HEAD_DIM = 64
DIFF_HEADS = 8

SGU_CHUNK = 128
SGU_GROUPS = 8

LANES = 128
LOG2E = 1.4426950408889634
VMEM_LIMIT = 52 * 1024 * 1024


def _params(*sem):
    return pltpu.CompilerParams(dimension_semantics=sem, vmem_limit_bytes=VMEM_LIMIT)


def _dot(a, b):
    return jnp.dot(a, b, preferred_element_type=F32)


def _dot_nt(a, b):
    return lax.dot_general(a, b, (((1,), (1,)), ((), ())), preferred_element_type=F32)


def _dot_tn(a, b):
    return lax.dot_general(a, b, (((0,), (0,)), ((), ())), preferred_element_type=F32)


def _rms(x, g):
    return x * lax.rsqrt(jnp.mean(x * x, axis=-1, keepdims=True) + EPS) * g


BF16_ROWS = 16


def _cast_specs(weights, steps, step_of):
    in_specs, out_specs, out_shapes = [], [], []
    for w in weights:
        rows, cols = w.shape
        blk = max(BF16_ROWS, rows // steps)
        nblk = rows // blk
        index = lambda *ids, nblk=nblk: (step_of(*ids) * nblk // steps, 0)
        in_specs.append(pl.BlockSpec((blk, cols), index))
        out_specs.append(pl.BlockSpec((blk, cols), index))
        out_shapes.append(jax.ShapeDtypeStruct(w.shape, BF16))
    return in_specs, out_specs, out_shapes


def _cast_blocks(src_refs, dst_refs):
    for src, dst in zip(src_refs, dst_refs):
        dst[...] = src[...].astype(BF16)


INPROJ_GROUPS = 2


def _gla_inproj_body(n_cast, x_ref, g_ref, w_ref, gw1_ref, gw2_ref, gb_ref, *refs):
    q_ref, k_ref, v_ref, og_ref, la_ref = refs[n_cast:n_cast + 5]
    w_scr = refs[2 * n_cast + 5]
    _cast_blocks(refs[:n_cast], refs[n_cast + 5:2 * n_cast + 5])
    dk = q_ref.shape[1]
    dv = v_ref.shape[1]

    @pl.when(pl.program_id(0) == 0)
    def _():
        for c0 in range(0, w_ref.shape[1], dk):
            w_scr[:, c0:c0 + dk] = w_ref[:, c0:c0 + dk].astype(BF16)

    rows = x_ref.shape[0] // INPROJ_GROUPS
    for r in range(0, x_ref.shape[0], rows):
        sl = slice(r, r + rows)
        h = _rms(x_ref[sl], g_ref[...]).astype(BF16)
        t = _dot(h, gw1_ref[...]).astype(BF16)
        q_ref[sl] = _dot(h, w_scr[:, 0:dk])
        z = _dot(t, gw2_ref[...]) + gb_ref[...]
        k_ref[sl] = _dot(h, w_scr[:, dk:2 * dk])
        la_ref[sl] = (jnp.minimum(z, 0.0) - jnp.log(1.0 + jnp.exp(-jnp.abs(z)))) / GLA_GATE_NORM
        v_ref[sl] = _dot(h, w_scr[:, 2 * dk:2 * dk + dv]).astype(BF16)
        og_ref[sl] = _dot(h, w_scr[:, 2 * dk + dv:2 * dk + 2 * dv])


def _gla_inproj(x, gain, w_in, gw1, gw2, gb, cast=(), tm=512):
    m, d = x.shape
    cast_in, cast_out, cast_shapes = _cast_specs(cast, m // tm, lambda i: i)
    dk = gw2.shape[1]
    dv = (w_in.shape[1] - 2 * dk) // 2
    gw1p = jnp.zeros((d, LANES), BF16).at[:, :GLA_RANK].set(gw1.astype(BF16))
    gw2p = jnp.zeros((LANES, dk), BF16).at[:GLA_RANK, :].set(gw2.astype(BF16))
    row = lambda i: (i, 0)
    fixed = lambda i: (0, 0)
    return pl.pallas_call(
        functools.partial(_gla_inproj_body, len(cast)),
        grid=(m // tm,),
        in_specs=[
            pl.BlockSpec((tm, d), row),
            pl.BlockSpec((1, d), fixed),
            pl.BlockSpec(w_in.shape, fixed, pipeline_mode=pl.Buffered(1)),
            pl.BlockSpec(gw1p.shape, fixed),
            pl.BlockSpec(gw2p.shape, fixed),
            pl.BlockSpec((1, dk), fixed),
        ] + cast_in,
        out_specs=[
            pl.BlockSpec((tm, dk), row),
            pl.BlockSpec((tm, dk), row),
            pl.BlockSpec((tm, dv), row),
            pl.BlockSpec((tm, dv), row),
            pl.BlockSpec((tm, dk), row),
        ] + cast_out,
        out_shape=[
            jax.ShapeDtypeStruct((m, dk), F32),
            jax.ShapeDtypeStruct((m, dk), F32),
            jax.ShapeDtypeStruct((m, dv), BF16),
            jax.ShapeDtypeStruct((m, dv), F32),
            jax.ShapeDtypeStruct((m, dk), F32),
        ] + cast_shapes,
        scratch_shapes=[pltpu.VMEM(w_in.shape, BF16)],
        compiler_params=_params("arbitrary"),
        name="gla_inproj",
    )(x, gain.reshape(1, d), w_in, gw1p, gw2p, gb.reshape(1, dk), *cast)


def _gla_body(tri_ref, q_ref, k_ref, v_ref, og_ref, la_ref, hn_ref, o_ref, st_ref):
    c = GLA_CHUNK
    tc, hk = q_ref.shape
    n_chunks = tc // c

    @pl.when(pl.program_id(2) == 0)
    def _():
        st_ref[...] = jnp.zeros_like(st_ref)

    tri = tri_ref[...]
    tg = tri.shape[0]
    la = la_ref[...]
    la_hi = la.astype(BF16)
    la_lo = (la - la_hi.astype(F32)).astype(BF16)
    b = jnp.concatenate([_dot(tri, la_hi[r:r + tg]) + _dot(tri, la_lo[r:r + tg]) for r in range(0, tc, tg)],
                        axis=0)
    b_lasts = [b[(n + 1) * c - 1:(n + 1) * c, :] for n in range(n_chunks)]
    b_end = jnp.concatenate([jnp.broadcast_to(bl, (c, hk)) for bl in b_lasts], axis=0)

    q = q_ref[...] * (hk ** -0.5)
    k = k_ref[...]
    q_dec = (q * jnp.exp(b)).astype(BF16)
    k_inv = (k * jnp.exp(-b)).astype(BF16)
    k_end = (k * jnp.exp(b_end - b)).astype(BF16)

    row = lax.broadcasted_iota(jnp.int32, (c, c), 0)
    col = lax.broadcasted_iota(jnp.int32, (c, c), 1)
    causal = col <= row
    chunk = lambda a, n: a[n * c:(n + 1) * c]

    atts = [jnp.where(causal, _dot_nt(chunk(q_dec, n), chunk(k_inv, n)), 0.0).astype(BF16)
            for n in range(n_chunks)]
    kvs = [_dot_tn(v_ref[n * c:(n + 1) * c, :], chunk(k_end, n)) for n in range(n_chunks)]
    intra = [_dot(atts[n], v_ref[n * c:(n + 1) * c, :]) for n in range(n_chunks)]

    st = st_ref[...]
    outs = []
    for n in range(n_chunks):
        outs.append(intra[n] + _dot_nt(chunk(q_dec, n), st.astype(BF16)))
        st = jnp.exp(b_lasts[n]) * st + kvs[n]
    st_ref[...] = st

    o = jnp.concatenate(outs, axis=0)
    o = o * lax.rsqrt(jnp.mean(o * o, axis=-1, keepdims=True) + EPS) * hn_ref[...]
    g = og_ref[...]
    o_ref[...] = (o * (g / (1.0 + jnp.exp(-g)))).astype(o_ref.dtype)


GLA_CUMSUM_ROWS = 256


def _gla_core(q, k, v, og, la, head_norm, batch, seq, tc=2048):
    tc = min(tc, seq)
    m, dk = q.shape
    dv = v.shape[1]
    hk = dk // GLA_HEADS
    hv = dv // GLA_HEADS
    nt = seq // tc
    idx = jnp.arange(GLA_CUMSUM_ROWS)
    tri = ((idx[:, None] // GLA_CHUNK == idx[None, :] // GLA_CHUNK)
           & (idx[None, :] <= idx[:, None])).astype(BF16)
    blk = lambda b, h, t: (b * nt + t, h)
    return pl.pallas_call(
        _gla_body,
        grid=(batch, GLA_HEADS, nt),
        in_specs=[
            pl.BlockSpec(tri.shape, lambda b, h, t: (0, 0)),
            pl.BlockSpec((tc, hk), blk),
            pl.BlockSpec((tc, hk), blk),
            pl.BlockSpec((tc, hv), blk),
            pl.BlockSpec((tc, hv), blk),
            pl.BlockSpec((tc, hk), blk),
            pl.BlockSpec((1, hv), lambda b, h, t: (0, 0)),
        ],
        out_specs=pl.BlockSpec((tc, hv), blk),
        out_shape=jax.ShapeDtypeStruct((m, dv), BF16),
        scratch_shapes=[pltpu.VMEM((hv, hk), F32)],
        compiler_params=_params("parallel", "parallel", "arbitrary"),
        name="gla_core",
    )(tri, q, k, v, og, la, head_norm.reshape(1, hv))


MLP_INIT_GROUPS = 4
MLP_FF_CHUNK = 512


def _mlp_body(final, x_ref, o_ref, wo_ref, g_ref, w1_ref, w2_ref, gf_ref, y_ref, h_scr):
    tm = x_ref.shape[0]
    rows = tm // MLP_INIT_GROUPS
    wo = wo_ref[...].astype(BF16)
    for r in range(0, tm, rows):
        x1 = x_ref[r:r + rows] + _dot(o_ref[r:r + rows], wo)
        y_ref[r:r + rows] = x1
        h_scr[r:r + rows] = _rms(x1, g_ref[...]).astype(BF16)
    for c in range(0, w1_ref.shape[1], MLP_FF_CHUNK):
        a = jnp.maximum(_dot(h_scr[...], w1_ref[:, c:c + MLP_FF_CHUNK]), 0.0)
        y_ref[...] += _dot((a * a).astype(BF16), w2_ref[c:c + MLP_FF_CHUNK, :])
    if final:
        y_ref[...] = _rms(y_ref[...], gf_ref[...])


def _mlp(x, o, w_out, gain, w1, w2, final_gain=None, tm=1024):
    m, d = x.shape
    kdim = o.shape[1]
    final = final_gain is not None
    gf = (final_gain if final else gain).reshape(1, d)
    resident = lambda a: pl.BlockSpec(a.shape, lambda i: (0, 0), pipeline_mode=pl.Buffered(1))
    return pl.pallas_call(
        functools.partial(_mlp_body, final),
        grid=(m // tm,),
        in_specs=[
            pl.BlockSpec((tm, d), lambda i: (i, 0)),
            pl.BlockSpec((tm, kdim), lambda i: (i, 0)),
            resident(w_out),
            pl.BlockSpec((1, d), lambda i: (0, 0)),
            resident(w1),
            resident(w2),
            pl.BlockSpec((1, d), lambda i: (0, 0)),
        ],
        out_specs=pl.BlockSpec((tm, d), lambda i: (i, 0)),
        out_shape=jax.ShapeDtypeStruct((m, d), F32),
        scratch_shapes=[pltpu.VMEM((tm, d), BF16)],
        compiler_params=_params("parallel"),
        name="mlp",
    )(x, o, w_out, gain.reshape(1, d), w1, w2, gf)


Q_BLOCK = 256
KEY_BLOCK = 512


def _diff_inproj_body(x_ref, g_ref, w_ref, qt_ref, k_ref, vt_ref, wqt_scr, wk_scr, wvt_scr):
    tq = qt_ref.shape[2]
    tk = vt_ref.shape[2]
    dq = k_ref.shape[1]

    @pl.when(pl.program_id(0) == 0)
    def _():
        wqt_scr[...] = w_ref[:, :dq].T.astype(BF16)
        wk_scr[...] = w_ref[:, dq:2 * dq].astype(BF16)
        wvt_scr[...] = w_ref[:, 2 * dq:].T.astype(BF16)

    for r in range(0, x_ref.shape[0], tq):
        h = _rms(x_ref[r:r + tq], g_ref[...]).astype(BF16)
        k_ref[r:r + tq] = _dot(h, wk_scr[...]).astype(BF16)
        qt = _dot_nt(wqt_scr[...], h) * (LOG2E * DIFF_HEAD_DIM ** -0.5)
        qt_ref[r // tq] = qt.astype(BF16)
        vt_ref[r // tk, :, r % tk:r % tk + tq] = _dot_nt(wvt_scr[...], h).astype(BF16)


def _diff_inproj(x, gain, w_in, tm=512):
    m, d = x.shape
    dq = 2 * DIFF_HEADS * DIFF_HEAD_DIM
    dv = w_in.shape[1] - 2 * dq
    tq, tk = Q_BLOCK, KEY_BLOCK
    fixed = lambda i: (0, 0)
    return pl.pallas_call(
        _diff_inproj_body,
        grid=(m // tm,),
        in_specs=[
            pl.BlockSpec((tm, d), lambda i: (i, 0)),
            pl.BlockSpec((1, d), fixed),
            pl.BlockSpec(w_in.shape, fixed, pipeline_mode=pl.Buffered(1)),
        ],
        out_specs=[
            pl.BlockSpec((tm // tq, dq, tq), lambda i: (i, 0, 0)),
            pl.BlockSpec((tm, dq), lambda i: (i, 0)),
            pl.BlockSpec((tm // tk, dv, tk), lambda i: (i, 0, 0)),
        ],
        out_shape=[
            jax.ShapeDtypeStruct((m // tq, dq, tq), BF16),
            jax.ShapeDtypeStruct((m, dq), BF16),
            jax.ShapeDtypeStruct((m // tk, dv, tk), BF16),
        ],
        scratch_shapes=[pltpu.VMEM((dq, d), BF16), pltpu.VMEM((d, dq), BF16), pltpu.VMEM((dv, d), BF16)],
        compiler_params=_params("arbitrary"),
        name="diff_inproj",
    )(x, gain.reshape(1, d), w_in)


ATT_HEADS_PER_STEP = 4
ATT_STAGES_PER_TRIP = 4
KEY_SPLIT = 32
ONES_ROWS = 16
FEAT_ROWS = 8


def _bf16_pieces(c):
    c1 = c.astype(BF16).astype(F32)
    c2 = (c - c1).astype(BF16).astype(F32)
    c3 = (c - c1 - c2).astype(BF16).astype(F32)
    return c1, c2, c3


def _diff_attn_body(lam_init, n_cast, slopes_ref, qt_ref, k_ref, vt_ref, lq1_ref, lk1_ref, lq2_ref, lk2_ref,
                    hn_ref, *refs):
    o_ref = refs[n_cast]
    m_scr, acc_scr, s_scr, mx_scr = refs[2 * n_cast + 1:]
    tq, tk = Q_BLOCK, KEY_BLOCK
    hd = 2 * DIFF_HEAD_DIM
    nh = ATT_HEADS_PER_STEP
    hg = pl.program_id(1)
    n_full = pl.program_id(2)

    def tri_mask(t):
        key = lax.broadcasted_iota(jnp.int32, (tq, 2 * tq), 0)
        qry = lax.broadcasted_iota(jnp.int32, (tq, 2 * tq), 1) % tq
        return jnp.where(key <= qry, t, NEG_INF)

    kj = lax.broadcasted_iota(jnp.int32, (tk, LANES), 0)
    kl = lax.broadcasted_iota(jnp.int32, (tk, LANES), 1)
    kfeat = jnp.where(kl < 3, kj // KEY_SPLIT, jnp.where(kl < 6, kj % KEY_SPLIT, 0)).astype(F32).astype(BF16)
    ones = jnp.ones((ONES_ROWS, tk), BF16)
    half = lax.broadcasted_iota(jnp.int32, (hd, tq), 0) < DIFF_HEAD_DIM
    frow = lax.broadcasted_iota(jnp.int32, (FEAT_ROWS, 2 * tq), 0)

    slopes, qfeats = [], []
    for j in range(nh):
        slope = slopes_ref[hg * nh + j] * LOG2E
        c1, c2, c3 = _bf16_pieces(jnp.full(frow.shape, slope, F32))
        s = float(KEY_SPLIT)
        qfeat = jnp.where(frow == 0, s * c1, jnp.where(frow == 1, s * c2, jnp.where(frow == 2, s * c3,
                jnp.where(frow == 3, c1, jnp.where(frow == 4, c2, jnp.where(frow == 5, c3, 0.0))))))
        slopes.append(slope)
        qfeats.append(jnp.concatenate([qfeat, jnp.zeros((hd - FEAT_ROWS, 2 * tq), F32)], axis=0).astype(BF16))

    def make_heads(slot):
        heads = []
        for j in range(nh):
            qt = qt_ref[slot, j * hd:(j + 1) * hd, :]
            zero = jnp.zeros_like(qt)
            q2t = jnp.concatenate([jnp.where(half, qt, zero), jnp.where(half, zero, qt)], axis=1)
            heads.append((slopes[j], jnp.concatenate([q2t, qfeats[j]], axis=0)))
        return heads

    def scores(heads, j, kb, rows=tk):
        k_blk = k_ref[pl.ds(pl.multiple_of(kb * tk, tk), rows), j * hd:(j + 1) * hd]
        return _dot(jnp.concatenate([k_blk, kfeat[:rows]], axis=1), heads[j][1])

    def stage(j, t):
        s_scr[j, :t.shape[0]] = t
        mx_scr[j] = jnp.max(t, axis=0, keepdims=True)

    def consume(heads, slot, j, kb, rows=tk):
        q_tile = 2 * n_full + slot
        shift = heads[j][0] * (kb * tk - q_tile * tq).astype(F32)
        m_old = m_scr[slot, j]
        m_new = jnp.maximum(m_old, mx_scr[j] + shift)
        alpha = jnp.exp2(m_old - m_new)
        p = jnp.exp2(s_scr[j, :rows] - (m_new - shift)).astype(BF16)
        vt_aug = jnp.concatenate([vt_ref[kb, j * hd:(j + 1) * hd, :rows], ones[:, :rows]], axis=0)
        acc_scr[slot, j] = alpha * acc_scr[slot, j] + _dot(vt_aug, p)
        m_scr[slot, j] = m_new

    def pipelined(heads, slot, kb, next_tile):
        for j in range(nh):
            t_next = next_tile(j)
            consume(heads, slot, j, kb)
            stage(j, t_next)

    def plain_stages(heads, slot):
        n_plain = jnp.maximum(n_full - 1, 0)

        def stages(first, count):
            for kb in range(count):
                pipelined(heads, slot, first + kb, lambda j, kb=kb: scores(heads, j, first + kb + 1))

        def trip(i, carry):
            stages(ATT_STAGES_PER_TRIP * i, ATT_STAGES_PER_TRIP)
            return carry

        lax.fori_loop(0, n_plain // ATT_STAGES_PER_TRIP, trip, 0)
        done = n_plain - n_plain % ATT_STAGES_PER_TRIP

        @pl.when(n_plain % ATT_STAGES_PER_TRIP >= 2)
        def _():
            stages(done, 2)

        @pl.when(n_plain % 2 == 1)
        def _():
            stages(n_plain - 1, 1)

    def finish(slot):
        lam = (jnp.exp(jnp.sum(lq1_ref[...] * lk1_ref[...], keepdims=True))
               - jnp.exp(jnp.sum(lq2_ref[...] * lk2_ref[...], keepdims=True)) + lam_init)
        for j in range(nh):
            acc = acc_scr[slot, j]
            o = acc[:hd] / acc[hd:hd + 1]
            ot = o[:, :tq] - lam * o[:, tq:]
            ot = ot * lax.rsqrt(jnp.mean(ot * ot, axis=0, keepdims=True) + EPS)
            ot = ot * hn_ref[...] * (1.0 - lam_init)
            o_ref[slot * tq:(slot + 1) * tq, j * hd:(j + 1) * hd] = ot.T.astype(o_ref.dtype)

    even, odd = make_heads(0), make_heads(1)
    last_even = lambda j: tri_mask(scores(even, j, n_full, rows=tq))

    def last_odd(j):
        t = scores(odd, j, n_full)
        return jnp.concatenate([t[:tq], tri_mask(t[tq:])], axis=0)

    def reset():
        m_scr[...] = jnp.full_like(m_scr, NEG_INF)
        acc_scr[...] = jnp.zeros_like(acc_scr)


    @pl.when(n_full == 0)
    def _():
        reset()
        for j in range(nh):
            stage(j, last_even(j))

    @pl.when(n_full > 0)
    def _():
        reset()
        for j in range(nh):
            stage(j, scores(even, j, 0))

    plain_stages(even, 0)

    @pl.when(n_full > 0)
    def _():
        pipelined(even, 0, n_full - 1, last_even)

    @pl.when(n_full == 0)
    def _():
        for j in range(nh):
            t_next = last_odd(j)
            consume(even, 0, j, n_full, rows=tq)
            stage(j, t_next)

    @pl.when(n_full > 0)
    def _():
        for j in range(nh):
            t_next = scores(odd, j, 0)
            consume(even, 0, j, n_full, rows=tq)
            stage(j, t_next)

    plain_stages(odd, 1)

    @pl.when(n_full > 0)
    def _():
        pipelined(odd, 1, n_full - 1, last_odd)

    _cast_blocks(refs[:n_cast], refs[n_cast + 1:2 * n_cast + 1])
    finish(0)
    for j in range(nh):
        consume(odd, 1, j, n_full)
    finish(1)


def _diff_attn(qt, k, vt, lq1, lk1, lq2, lk2, head_norm, batch, seq, layer_idx, cast=()):
    tq, tk = Q_BLOCK, KEY_BLOCK
    nh = ATT_HEADS_PER_STEP
    m, dq = k.shape
    hd = 2 * DIFF_HEAD_DIM
    assert tk == 2 * tq
    nq = seq // tk
    ng = DIFF_HEADS // nh
    cast_in, cast_out, cast_shapes = _cast_specs(cast, batch * ng * nq, lambda b, h, i: (b * ng + h) * nq + i)
    lam_init = 0.8 - 0.6 * math.exp(-0.3 * layer_idx)
    slopes = 2.0 ** (-8.0 * jnp.arange(1, DIFF_HEADS + 1, dtype=F32) / DIFF_HEADS)
    vec = lambda a: a.reshape(1, DIFF_HEAD_DIM).astype(F32)
    lam_spec = pl.BlockSpec((1, DIFF_HEAD_DIM), lambda b, h, i: (0, 0))
    return pl.pallas_call(
        functools.partial(_diff_attn_body, lam_init, len(cast)),
        grid=(batch, ng, nq),
        in_specs=[
            pl.BlockSpec(memory_space=pltpu.SMEM),
            pl.BlockSpec((2, nh * hd, tq), lambda b, h, i: (b * nq + i, h, 0)),
            pl.BlockSpec((seq, nh * hd), lambda b, h, i: (b, h)),
            pl.BlockSpec((seq // tk, nh * hd, tk), lambda b, h, i: (b, h, 0)),
            lam_spec, lam_spec, lam_spec, lam_spec,
            pl.BlockSpec((hd, 1), lambda b, h, i: (0, 0)),
        ] + cast_in,
        out_specs=[pl.BlockSpec((tk, nh * hd), lambda b, h, i: (b * nq + i, h))] + cast_out,
        out_shape=[jax.ShapeDtypeStruct((m, dq), BF16)] + cast_shapes,
        scratch_shapes=[
            pltpu.VMEM((2, nh, 1, 2 * tq), F32),
            pltpu.VMEM((2, nh, hd + ONES_ROWS, 2 * tq), F32),
            pltpu.VMEM((nh, tk, 2 * tq), F32),
            pltpu.VMEM((nh, 1, 2 * tq), F32),
        ],
        compiler_params=_params("parallel", "parallel", "arbitrary"),
        name="diff_attn",
    )(slopes, qt, k, vt, vec(lq1), vec(lk1), vec(lq2), vec(lk2), head_norm.reshape(hd, 1).astype(F32), *cast)


SGU_ROWS = 256
GELU_C = math.sqrt(2.0 / math.pi)


def _sgu_body(x_ref, g_ref, win_ref, bin_ref, vn_ref, ws_ref, bs_ref, y_ref):
    tm = x_ref.shape[0]
    width = vn_ref.shape[1]
    gd = width // SGU_GROUPS
    c = SGU_CHUNK
    nc = SGU_ROWS // c
    row = lax.broadcasted_iota(jnp.int32, (c, c), 0)
    col = lax.broadcasted_iota(jnp.int32, (c, c), 1)
    ws = [jnp.where(col <= row, ws_ref[gi], 0.0).astype(BF16) for gi in range(SGU_GROUPS)]
    w_in = win_ref[...].astype(BF16)
    uvs = [_dot(_rms(x_ref[r:r + SGU_ROWS], g_ref[...]).astype(BF16), w_in) + bin_ref[...]
           for r in range(0, tm, SGU_ROWS)]
    for i, uv in enumerate(uvs):
        uv = 0.5 * uv * (1.0 + jnp.tanh(uv * (GELU_C + (GELU_C * 0.044715) * (uv * uv))))
        u = uv[:, :width]
        v = _rms(uv[:, width:], vn_ref[...]).astype(BF16)
        cols = []
        for gi in range(SGU_GROUPS):
            vg = jnp.concatenate([v[n * c:(n + 1) * c, gi * gd:(gi + 1) * gd] for n in range(nc)], axis=1)
            sg = _dot(ws[gi], vg)
            bias = bs_ref[gi]
            cols.append(jnp.concatenate([sg[:, n * gd:(n + 1) * gd] + bias for n in range(nc)], axis=0))
        s = jnp.concatenate(cols, axis=1)
        y_ref[i * SGU_ROWS:(i + 1) * SGU_ROWS] = (u * s).astype(y_ref.dtype)


def _sgu(x, gain, w_in, b_in, v_norm, w_s, b_s, tm=1024):
    m, d = x.shape
    width = v_norm.shape[0]
    gd = width // SGU_GROUPS
    bs = jnp.broadcast_to(b_s[:, :, None], (SGU_GROUPS, SGU_CHUNK, gd)).astype(F32)
    fixed2 = lambda i: (0, 0)
    fixed3 = lambda i: (0, 0, 0)
    return pl.pallas_call(
        _sgu_body,
        grid=(m // tm,),
        in_specs=[
            pl.BlockSpec((tm, d), lambda i: (i, 0)),
            pl.BlockSpec((1, d), fixed2),
            pl.BlockSpec(w_in.shape, fixed2),
            pl.BlockSpec((1, 2 * width), fixed2),
            pl.BlockSpec((1, width), fixed2),
            pl.BlockSpec(w_s.shape, fixed3),
            pl.BlockSpec(bs.shape, fixed3),
        ],
        out_specs=pl.BlockSpec((tm, width), lambda i: (i, 0)),
        out_shape=jax.ShapeDtypeStruct((m, width), BF16),
        compiler_params=_params("parallel"),
        name="sgu",
    )(x, gain.reshape(1, d), w_in, b_in.reshape(1, 2 * width), v_norm.reshape(1, width),
      w_s, bs)


def _gla_mixer(x, batch, seq, norm1, w_in, gw1, gw2, gb, head_norm, cast=()):
    q, k, v, og, la, *cast_out = _gla_inproj(x, norm1, w_in, gw1, gw2, gb, cast)
    return _gla_core(q, k, v, og, la, head_norm, batch, seq), cast_out


def _diff_mixer(x, batch, seq, norm1, w_in, lq1, lk1, lq2, lk2, head_norm, layer_idx, cast=()):
    qt, k, vt = _diff_inproj(x, norm1, w_in)
    o, *cast_out = _diff_attn(qt, k, vt, lq1, lk1, lq2, lk2, head_norm, batch, seq, layer_idx, cast)
    return o, cast_out


def kernel(x, l0_norm1, l0_w_in, l0_gate_w1, l0_gate_w2, l0_gate_b, l0_head_norm, l0_w_out, l0_norm2, l0_mlp_w1, l0_mlp_w2, l1_norm1, l1_w_in, l1_lambda_q1, l1_lambda_k1, l1_lambda_q2, l1_lambda_k2, l1_head_norm, l1_w_out, l1_norm2, l1_mlp_w1, l1_mlp_w2, l2_norm1, l2_w_in, l2_b_in, l2_v_norm, l2_w_s, l2_b_s, l2_w_out, l2_norm2, l2_mlp_w1, l2_mlp_w2, l3_norm1, l3_w_in, l3_gate_w1, l3_gate_w2, l3_gate_b, l3_head_norm, l3_w_out, l3_norm2, l3_mlp_w1, l3_mlp_w2, final_norm):
    batch, seq, d = x.shape
    h = x.reshape(batch * seq, d)
    o, (w1, w2) = _gla_mixer(h, batch, seq, l0_norm1, l0_w_in, l0_gate_w1, l0_gate_w2, l0_gate_b, l0_head_norm,
                             cast=(l0_mlp_w1, l0_mlp_w2))
    h = _mlp(h, o, l0_w_out, l0_norm2, w1, w2)
    later = (l1_mlp_w1, l1_mlp_w2, l2_mlp_w1, l2_mlp_w2, l3_mlp_w1, l3_mlp_w2)
    o, later = _diff_mixer(h, batch, seq, l1_norm1, l1_w_in, l1_lambda_q1, l1_lambda_k1, l1_lambda_q2, l1_lambda_k2,
                           l1_head_norm, 1, cast=later)
    h = _mlp(h, o, l1_w_out, l1_norm2, later[0], later[1])
    o = _sgu(h, l2_norm1, l2_w_in, l2_b_in, l2_v_norm, l2_w_s, l2_b_s)
    h = _mlp(h, o, l2_w_out, l2_norm2, later[2], later[3])
    o, _ = _gla_mixer(h, batch, seq, l3_norm1, l3_w_in, l3_gate_w1, l3_gate_w2, l3_gate_b, l3_head_norm)
    h = _mlp(h, o, l3_w_out, l3_norm2, later[4], later[5], final_gain=final_norm)
    return h.reshape(batch, seq, d)
```

```python
import functools
import math

import jax
import jax.numpy as jnp
from jax import lax
from jax.experimental import pallas as pl
from jax.experimental.pallas import tpu as pltpu

F32 = jnp.float32
BF16 = jnp.bfloat16

EPS = 1e-6
NEG_INF = -1e30

GLA_HEADS = 4
GLA_RANK = 16
GLA_GATE_NORM = 16.0
GLA_CHUNK = 64

DIFF_HEAD_DIM = 64
DIFF_HEADS = 8

SGU_CHUNK = 128
SGU_GROUPS = 8

LANES = 128
LOG2E = 1.4426950408889634
VMEM_LIMIT = 58 * 1024 * 1024


def _params(*sem):
    return pltpu.CompilerParams(dimension_semantics=sem, vmem_limit_bytes=VMEM_LIMIT)


def _dot(a, b):
    return jnp.dot(a, b, preferred_element_type=F32)


def _dot_nt(a, b):
    return lax.dot_general(a, b, (((1,), (1,)), ((), ())), preferred_element_type=F32)


def _dot_tn(a, b):
    return lax.dot_general(a, b, (((0,), (0,)), ((), ())), preferred_element_type=F32)


def _rms(x, g):
    return x * lax.rsqrt(jnp.mean(x * x, axis=-1, keepdims=True) + EPS) * g


BF16_ROWS = 16


def _cast_specs(weights, steps, step_of):
    in_specs, out_specs, out_shapes = [], [], []
    for w in weights:
        rows, cols = w.shape
        blk = max(BF16_ROWS, rows // steps)
        nblk = rows // blk
        index = lambda *ids, nblk=nblk: (step_of(*ids) * nblk // steps, 0)
        in_specs.append(pl.BlockSpec((blk, cols), index))
        out_specs.append(pl.BlockSpec((blk, cols), index))
        out_shapes.append(jax.ShapeDtypeStruct(w.shape, BF16))
    return in_specs, out_specs, out_shapes


def _cast_blocks(src_refs, dst_refs):
    for src, dst in zip(src_refs, dst_refs):
        dst[...] = src[...].astype(BF16)


INPROJ_GROUPS = 2


def _gla_inproj_body(n_cast, x_ref, g_ref, w_ref, gw1_ref, gw2_ref, gb_ref, *refs):
    q_ref, k_ref, v_ref, og_ref, la_ref = refs[n_cast:n_cast + 5]
    w_scr = refs[2 * n_cast + 5]
    _cast_blocks(refs[:n_cast], refs[n_cast + 5:2 * n_cast + 5])
    dk = q_ref.shape[1]
    dv = v_ref.shape[1]

    @pl.when(pl.program_id(0) == 0)
    def _():
        for c0 in range(0, w_ref.shape[1], dk):
            w_scr[:, c0:c0 + dk] = w_ref[:, c0:c0 + dk].astype(BF16)

    rows = x_ref.shape[0] // INPROJ_GROUPS
    for r in range(0, x_ref.shape[0], rows):
        sl = slice(r, r + rows)
        h = _rms(x_ref[sl], g_ref[...]).astype(BF16)
        t = _dot(h, gw1_ref[...]).astype(BF16)
        q_ref[sl] = _dot(h, w_scr[:, 0:dk])
        z = _dot(t, gw2_ref[...]) + gb_ref[...]
        k_ref[sl] = _dot(h, w_scr[:, dk:2 * dk])
        la_ref[sl] = (jnp.minimum(z, 0.0) - jnp.log(1.0 + jnp.exp(-jnp.abs(z)))) / GLA_GATE_NORM
        v_ref[sl] = _dot(h, w_scr[:, 2 * dk:2 * dk + dv]).astype(BF16)
        og_ref[sl] = _dot(h, w_scr[:, 2 * dk + dv:2 * dk + 2 * dv])


def _gla_inproj(x, gain, w_in, gw1, gw2, gb, cast=(), tm=512):
    m, d = x.shape
    cast_in, cast_out, cast_shapes = _cast_specs(cast, m // tm, lambda i: i)
    dk = gw2.shape[1]
    dv = (w_in.shape[1] - 2 * dk) // 2
    gw1p = jnp.zeros((d, LANES), BF16).at[:, :GLA_RANK].set(gw1.astype(BF16))
    gw2p = jnp.zeros((LANES, dk), BF16).at[:GLA_RANK, :].set(gw2.astype(BF16))
    row = lambda i: (i, 0)
    fixed = lambda i: (0, 0)
    return pl.pallas_call(
        functools.partial(_gla_inproj_body, len(cast)),
        grid=(m // tm,),
        in_specs=[
            pl.BlockSpec((tm, d), row),
            pl.BlockSpec((1, d), fixed),
            pl.BlockSpec(w_in.shape, fixed, pipeline_mode=pl.Buffered(1)),
            pl.BlockSpec(gw1p.shape, fixed),
            pl.BlockSpec(gw2p.shape, fixed),
            pl.BlockSpec((1, dk), fixed),
        ] + cast_in,
        out_specs=[
            pl.BlockSpec((tm, dk), row),
            pl.BlockSpec((tm, dk), row),
            pl.BlockSpec((tm, dv), row),
            pl.BlockSpec((tm, dv), row),
            pl.BlockSpec((tm, dk), row),
        ] + cast_out,
        out_shape=[
            jax.ShapeDtypeStruct((m, dk), F32),
            jax.ShapeDtypeStruct((m, dk), F32),
            jax.ShapeDtypeStruct((m, dv), BF16),
            jax.ShapeDtypeStruct((m, dv), F32),
            jax.ShapeDtypeStruct((m, dk), F32),
        ] + cast_shapes,
        scratch_shapes=[pltpu.VMEM(w_in.shape, BF16)],
        compiler_params=_params("arbitrary"),
        name="gla_inproj",
    )(x, gain.reshape(1, d), w_in, gw1p, gw2p, gb.reshape(1, dk), *cast)


def _gla_body(tri_ref, q_ref, k_ref, v_ref, og_ref, la_ref, hn_ref, o_ref, st_ref):
    c = GLA_CHUNK
    tc, hk = q_ref.shape
    n_chunks = tc // c

    @pl.when(pl.program_id(2) == 0)
    def _():
        st_ref[...] = jnp.zeros_like(st_ref)

    tri = tri_ref[...]
    tg = tri.shape[0]
    la = la_ref[...]
    la_hi = la.astype(BF16)
    la_lo = (la - la_hi.astype(F32)).astype(BF16)
    b = jnp.concatenate([_dot(tri, la_hi[r:r + tg]) + _dot(tri, la_lo[r:r + tg]) for r in range(0, tc, tg)],
                        axis=0)
    b_lasts = [b[(n + 1) * c - 1:(n + 1) * c, :] for n in range(n_chunks)]
    b_end = jnp.concatenate([jnp.broadcast_to(bl, (c, hk)) for bl in b_lasts], axis=0)

    q = q_ref[...] * (hk ** -0.5)
    k = k_ref[...]
    q_dec = (q * jnp.exp(b)).astype(BF16)
    k_inv = (k * jnp.exp(-b)).astype(BF16)
    k_end = (k * jnp.exp(b_end - b)).astype(BF16)

    row = lax.broadcasted_iota(jnp.int32, (c, c), 0)
    col = lax.broadcasted_iota(jnp.int32, (c, c), 1)
    causal = col <= row
    chunk = lambda a, n: a[n * c:(n + 1) * c]

    atts = [jnp.where(causal, _dot_nt(chunk(q_dec, n), chunk(k_inv, n)), 0.0).astype(BF16)
            for n in range(n_chunks)]
    kvs = [_dot_tn(v_ref[n * c:(n + 1) * c, :], chunk(k_end, n)) for n in range(n_chunks)]
    intra = [_dot(atts[n], v_ref[n * c:(n + 1) * c, :]) for n in range(n_chunks)]

    st = st_ref[...]
    outs = []
    for n in range(n_chunks):
        outs.append(intra[n] + _dot_nt(chunk(q_dec, n), st.astype(BF16)))
        st = jnp.exp(b_lasts[n]) * st + kvs[n]
    st_ref[...] = st

    o = jnp.concatenate(outs, axis=0)
    o = o * lax.rsqrt(jnp.mean(o * o, axis=-1, keepdims=True) + EPS) * hn_ref[...]
    g = og_ref[...]
    o_ref[...] = (o * (g / (1.0 + jnp.exp(-g)))).astype(o_ref.dtype)


GLA_CUMSUM_ROWS = 256


def _gla_core(q, k, v, og, la, head_norm, batch, seq, tc=2048):
    tc = min(tc, seq)
    m, dk = q.shape
    dv = v.shape[1]
    hk = dk // GLA_HEADS
    hv = dv // GLA_HEADS
    nt = seq // tc
    idx = jnp.arange(GLA_CUMSUM_ROWS)
    tri = ((idx[:, None] // GLA_CHUNK == idx[None, :] // GLA_CHUNK)
           & (idx[None, :] <= idx[:, None])).astype(BF16)
    blk = lambda b, h, t: (b * nt + t, h)
    return pl.pallas_call(
        _gla_body,
        grid=(batch, GLA_HEADS, nt),
        in_specs=[
            pl.BlockSpec(tri.shape, lambda b, h, t: (0, 0)),
            pl.BlockSpec((tc, hk), blk),
            pl.BlockSpec((tc, hk), blk),
            pl.BlockSpec((tc, hv), blk),
            pl.BlockSpec((tc, hv), blk),
            pl.BlockSpec((tc, hk), blk),
            pl.BlockSpec((1, hv), lambda b, h, t: (0, 0)),
        ],
        out_specs=pl.BlockSpec((tc, hv), blk),
        out_shape=jax.ShapeDtypeStruct((m, dv), BF16),
        scratch_shapes=[pltpu.VMEM((hv, hk), F32)],
        compiler_params=_params("parallel", "parallel", "arbitrary"),
        name="gla_core",
    )(tri, q, k, v, og, la, head_norm.reshape(1, hv))


MLP_INIT_GROUPS = 4
MLP_FF_CHUNK = 512


def _mlp_body(final, x_ref, o_ref, wo_ref, g_ref, w1_ref, w2_ref, gf_ref, y_ref, h_scr):
    tm = x_ref.shape[0]
    rows = tm // MLP_INIT_GROUPS
    wo = wo_ref[...].astype(BF16)
    for r in range(0, tm, rows):
        x1 = x_ref[r:r + rows] + _dot(o_ref[r:r + rows], wo)
        y_ref[r:r + rows] = x1
        h_scr[r:r + rows] = _rms(x1, g_ref[...]).astype(BF16)
    for c in range(0, w1_ref.shape[1], MLP_FF_CHUNK):
        a = jnp.maximum(_dot(h_scr[...], w1_ref[:, c:c + MLP_FF_CHUNK]), 0.0)
        y_ref[...] += _dot((a * a).astype(BF16), w2_ref[c:c + MLP_FF_CHUNK, :])
    if final:
        y_ref[...] = _rms(y_ref[...], gf_ref[...])


def _mlp(x, o, w_out, gain, w1, w2, final_gain=None, tm=1024):
    m, d = x.shape
    kdim = o.shape[1]
    final = final_gain is not None
    gf = (final_gain if final else gain).reshape(1, d)
    resident = lambda a: pl.BlockSpec(a.shape, lambda i: (0, 0), pipeline_mode=pl.Buffered(1))
    return pl.pallas_call(
        functools.partial(_mlp_body, final),
        grid=(m // tm,),
        in_specs=[
            pl.BlockSpec((tm, d), lambda i: (i, 0)),
            pl.BlockSpec((tm, kdim), lambda i: (i, 0)),
            resident(w_out),
            pl.BlockSpec((1, d), lambda i: (0, 0)),
            resident(w1),
            resident(w2),
            pl.BlockSpec((1, d), lambda i: (0, 0)),
        ],
        out_specs=pl.BlockSpec((tm, d), lambda i: (i, 0)),
        out_shape=jax.ShapeDtypeStruct((m, d), F32),
        scratch_shapes=[pltpu.VMEM((tm, d), BF16)],
        compiler_params=_params("parallel"),
        name="mlp",
    )(x, o, w_out, gain.reshape(1, d), w1, w2, gf)


Q_BLOCK = 256
KEY_BLOCK = 512


def _diff_inproj_body(x_ref, g_ref, w_ref, qt_ref, k_ref, vt_ref, wqt_scr, wk_scr, wvt_scr):
    tq = qt_ref.shape[2]
    tk = vt_ref.shape[2]
    dq = k_ref.shape[1]

    @pl.when(pl.program_id(0) == 0)
    def _():
        wqt_scr[...] = w_ref[:, :dq].T.astype(BF16)
        wk_scr[...] = w_ref[:, dq:2 * dq].astype(BF16)
        wvt_scr[...] = w_ref[:, 2 * dq:].T.astype(BF16)

    for r in range(0, x_ref.shape[0], tq):
        h = _rms(x_ref[r:r + tq], g_ref[...]).astype(BF16)
        k_ref[r:r + tq] = _dot(h, wk_scr[...]).astype(BF16)
        qt = _dot_nt(wqt_scr[...], h) * (LOG2E * DIFF_HEAD_DIM ** -0.5)
        qt_ref[r // tq] = qt.astype(BF16)
        vt_ref[r // tk, :, r % tk:r % tk + tq] = _dot_nt(wvt_scr[...], h).astype(BF16)


def _diff_inproj(x, gain, w_in, tm=512):
    m, d = x.shape
    dq = 2 * DIFF_HEADS * DIFF_HEAD_DIM
    dv = w_in.shape[1] - 2 * dq
    tq, tk = Q_BLOCK, KEY_BLOCK
    fixed = lambda i: (0, 0)
    return pl.pallas_call(
        _diff_inproj_body,
        grid=(m // tm,),
        in_specs=[
            pl.BlockSpec((tm, d), lambda i: (i, 0)),
            pl.BlockSpec((1, d), fixed),
            pl.BlockSpec(w_in.shape, fixed, pipeline_mode=pl.Buffered(1)),
        ],
        out_specs=[
            pl.BlockSpec((tm // tq, dq, tq), lambda i: (i, 0, 0)),
            pl.BlockSpec((tm, dq), lambda i: (i, 0)),
            pl.BlockSpec((tm // tk, dv, tk), lambda i: (i, 0, 0)),
        ],
        out_shape=[
            jax.ShapeDtypeStruct((m // tq, dq, tq), BF16),
            jax.ShapeDtypeStruct((m, dq), BF16),
            jax.ShapeDtypeStruct((m // tk, dv, tk), BF16),
        ],
        scratch_shapes=[pltpu.VMEM((dq, d), BF16), pltpu.VMEM((d, dq), BF16), pltpu.VMEM((dv, d), BF16)],
        compiler_params=_params("arbitrary"),
        name="diff_inproj",
    )(x, gain.reshape(1, d), w_in)


ATT_HEADS_PER_STEP = 4
ATT_STAGES_PER_TRIP = 4
ATT_TILES_PER_STEP = 4
KEY_SPLIT = 32
ONES_ROWS = 16
FEAT_ROWS = 8


def _bf16_pieces(c):
    c1 = c.astype(BF16).astype(F32)
    c2 = (c - c1).astype(BF16).astype(F32)
    c3 = (c - c1 - c2).astype(BF16).astype(F32)
    return c1, c2, c3


def _diff_attn_body(lam_init, n_cast, slopes_ref, qt_ref, k_ref, vt_ref, lq1_ref, lk1_ref, lq2_ref, lk2_ref,
                    hn_ref, *refs):
    o_ref = refs[n_cast]
    m_scr, acc_scr, s_scr, mx_scr = refs[2 * n_cast + 1:]
    tq, tk = Q_BLOCK, KEY_BLOCK
    hd = 2 * DIFF_HEAD_DIM
    nh = ATT_HEADS_PER_STEP
    hg = pl.program_id(1)

    def tri_mask(t):
        key = lax.broadcasted_iota(jnp.int32, (tq, 2 * tq), 0)
        qry = lax.broadcasted_iota(jnp.int32, (tq, 2 * tq), 1) % tq
        return jnp.where(key <= qry, t, NEG_INF)

    kj = lax.broadcasted_iota(jnp.int32, (tk, LANES), 0)
    kl = lax.broadcasted_iota(jnp.int32, (tk, LANES), 1)
    kfeat = jnp.where(kl < 3, kj // KEY_SPLIT, jnp.where(kl < 6, kj % KEY_SPLIT, 0)).astype(F32).astype(BF16)
    ones = jnp.ones((ONES_ROWS, tk), BF16)
    half = lax.broadcasted_iota(jnp.int32, (hd, tq), 0) < DIFF_HEAD_DIM
    frow = lax.broadcasted_iota(jnp.int32, (FEAT_ROWS, 2 * tq), 0)

    slopes, qfeats = [], []
    for j in range(nh):
        slope = slopes_ref[hg * nh + j] * LOG2E
        c1, c2, c3 = _bf16_pieces(jnp.full(frow.shape, slope, F32))
        s = float(KEY_SPLIT)
        qfeat = jnp.where(frow == 0, s * c1, jnp.where(frow == 1, s * c2, jnp.where(frow == 2, s * c3,
                jnp.where(frow == 3, c1, jnp.where(frow == 4, c2, jnp.where(frow == 5, c3, 0.0))))))
        slopes.append(slope)
        qfeats.append(jnp.concatenate([qfeat, jnp.zeros((hd - FEAT_ROWS, 2 * tq), F32)], axis=0).astype(BF16))

    def make_heads(slot):
        heads = []
        for j in range(nh):
            qt = qt_ref[slot, j * hd:(j + 1) * hd, :]
            zero = jnp.zeros_like(qt)
            q2t = jnp.concatenate([jnp.where(half, qt, zero), jnp.where(half, zero, qt)], axis=1)
            heads.append((slopes[j], jnp.concatenate([q2t, qfeats[j]], axis=0)))
        return heads

    def scores(heads, j, kb, rows=tk):
        k_blk = k_ref[pl.ds(pl.multiple_of(kb * tk, tk), rows), j * hd:(j + 1) * hd]
        return _dot(jnp.concatenate([k_blk, kfeat[:rows]], axis=1), heads[j][1])

    def stage(j, t):
        s_scr[j, :t.shape[0]] = t
        mx_scr[j] = jnp.max(t, axis=0, keepdims=True)

    def consume(heads, slot, nf, j, kb, rows=tk):
        q_tile = 2 * nf + slot % 2
        shift = heads[j][0] * (kb * tk - q_tile * tq).astype(F32)
        m_old = m_scr[slot, j]
        m_new = jnp.maximum(m_old, mx_scr[j] + shift)
        alpha = jnp.exp2(m_old - m_new)
        p = jnp.exp2(s_scr[j, :rows] - (m_new - shift)).astype(BF16)
        vt_aug = jnp.concatenate([vt_ref[kb, j * hd:(j + 1) * hd, :rows], ones[:, :rows]], axis=0)
        acc_scr[slot, j] = alpha * acc_scr[slot, j] + _dot(vt_aug, p)
        m_scr[slot, j] = m_new

    def pipelined(heads, slot, nf, kb, next_tile):
        for j in range(nh):
            t_next = next_tile(j)
            consume(heads, slot, nf, j, kb)
            stage(j, t_next)

    def plain_stages(heads, slot, nf):
        n_plain = jnp.maximum(nf - 1, 0)

        def stages(first, count):
            for kb in range(count):
                pipelined(heads, slot, nf, first + kb, lambda j, kb=kb: scores(heads, j, first + kb + 1))

        def trip(i, carry):
            stages(ATT_STAGES_PER_TRIP * i, ATT_STAGES_PER_TRIP)
            return carry

        lax.fori_loop(0, n_plain // ATT_STAGES_PER_TRIP, trip, 0)
        done = n_plain - n_plain % ATT_STAGES_PER_TRIP

        @pl.when(n_plain % ATT_STAGES_PER_TRIP >= 2)
        def _():
            stages(done, 2)

        @pl.when(n_plain % 2 == 1)
        def _():
            stages(n_plain - 1, 1)

    def finish(slot):
        lam = (jnp.exp(jnp.sum(lq1_ref[...] * lk1_ref[...], keepdims=True))
               - jnp.exp(jnp.sum(lq2_ref[...] * lk2_ref[...], keepdims=True)) + lam_init)
        for j in range(nh):
            acc = acc_scr[slot, j]
            o = acc[:hd] / acc[hd:hd + 1]
            ot = o[:, :tq] - lam * o[:, tq:]
            ot = ot * lax.rsqrt(jnp.mean(ot * ot, axis=0, keepdims=True) + EPS)
            ot = ot * hn_ref[...] * (1.0 - lam_init)
            o_ref[slot * tq:(slot + 1) * tq, j * hd:(j + 1) * hd] = ot.T.astype(o_ref.dtype)

    def reset():
        m_scr[...] = jnp.full_like(m_scr, NEG_INF)
        acc_scr[...] = jnp.zeros_like(acc_scr)

    heads = [make_heads(slot) for slot in range(ATT_TILES_PER_STEP)]

    def last_even(slot, nf):
        return lambda j: tri_mask(scores(heads[slot], j, nf, rows=tq))

    def last_odd(slot, nf):
        def tile(j):
            t = scores(heads[slot], j, nf)
            return jnp.concatenate([t[:tq], tri_mask(t[tq:])], axis=0)
        return tile

    def run_pair(pair, nf, maybe_empty, deferred):
        se, so = 2 * pair, 2 * pair + 1
        plain_stages(heads[se], se, nf)

        def even_last_stage():
            pipelined(heads[se], se, nf, nf - 1, last_even(se, nf))

        def odd_last_stage():
            pipelined(heads[so], so, nf, nf - 1, last_odd(so, nf))

        def hand_over(first_odd_tile):
            for j in range(nh):
                t_next = first_odd_tile(j)
                consume(heads[se], se, nf, j, nf, rows=tq)
                stage(j, t_next)

        if maybe_empty:
            pl.when(nf > 0)(even_last_stage)
            pl.when(nf == 0)(lambda: hand_over(last_odd(so, nf)))
            pl.when(nf > 0)(lambda: hand_over(lambda j: scores(heads[so], j, 0)))
        else:
            even_last_stage()
            if deferred is not None:
                finish(deferred)
            hand_over(lambda j: scores(heads[so], j, 0))
        plain_stages(heads[so], so, nf)
        if maybe_empty:
            pl.when(nf > 0)(odd_last_stage)
        else:
            odd_last_stage()

    first_pair = (ATT_TILES_PER_STEP // 2) * pl.program_id(2)

    @pl.when(first_pair == 0)
    def _():
        reset()
        for j in range(nh):
            stage(j, last_even(0, first_pair)(j))

    @pl.when(first_pair > 0)
    def _():
        reset()
        for j in range(nh):
            stage(j, scores(heads[0], j, 0))

    run_pair(0, first_pair, True, None)
    for pair in range(1, ATT_TILES_PER_STEP // 2):
        nf = first_pair + pair
        prev = 2 * pair - 1
        finish(prev - 1)
        for j in range(nh):
            t_next = scores(heads[2 * pair], j, 0)
            consume(heads[prev], prev, nf - 1, j, nf - 1)
            stage(j, t_next)
        run_pair(pair, nf, False, prev)

    last = ATT_TILES_PER_STEP - 1
    nf_last = first_pair + ATT_TILES_PER_STEP // 2 - 1
    _cast_blocks(refs[:n_cast], refs[n_cast + 1:2 * n_cast + 1])
    finish(last - 1)
    for j in range(nh):
        consume(heads[last], last, nf_last, j, nf_last)
    finish(last)


def _diff_attn(qt, k, vt, lq1, lk1, lq2, lk2, head_norm, batch, seq, layer_idx, cast=()):
    tq, tk = Q_BLOCK, KEY_BLOCK
    nh = ATT_HEADS_PER_STEP
    m, dq = k.shape
    hd = 2 * DIFF_HEAD_DIM
    assert tk == 2 * tq
    nt = ATT_TILES_PER_STEP
    nq = seq // (nt * tq)
    ng = DIFF_HEADS // nh
    cast_in, cast_out, cast_shapes = _cast_specs(cast, batch * ng * nq, lambda b, h, i: (b * ng + h) * nq + i)
    lam_init = 0.8 - 0.6 * math.exp(-0.3 * layer_idx)
    slopes = 2.0 ** (-8.0 * jnp.arange(1, DIFF_HEADS + 1, dtype=F32) / DIFF_HEADS)
    vec = lambda a: a.reshape(1, DIFF_HEAD_DIM).astype(F32)
    lam_spec = pl.BlockSpec((1, DIFF_HEAD_DIM), lambda b, h, i: (0, 0))
    return pl.pallas_call(
        functools.partial(_diff_attn_body, lam_init, len(cast)),
        grid=(batch, ng, nq),
        in_specs=[
            pl.BlockSpec(memory_space=pltpu.SMEM),
            pl.BlockSpec((nt, nh * hd, tq), lambda b, h, i: (b * nq + i, h, 0)),
            pl.BlockSpec((seq, nh * hd), lambda b, h, i: (b, h)),
            pl.BlockSpec((seq // tk, nh * hd, tk), lambda b, h, i: (b, h, 0)),
            lam_spec, lam_spec, lam_spec, lam_spec,
            pl.BlockSpec((hd, 1), lambda b, h, i: (0, 0)),
        ] + cast_in,
        out_specs=[pl.BlockSpec((nt * tq, nh * hd), lambda b, h, i: (b * nq + i, h))] + cast_out,
        out_shape=[jax.ShapeDtypeStruct((m, dq), BF16)] + cast_shapes,
        scratch_shapes=[
            pltpu.VMEM((nt, nh, 1, 2 * tq), F32),
            pltpu.VMEM((nt, nh, hd + ONES_ROWS, 2 * tq), F32),
            pltpu.VMEM((nh, tk, 2 * tq), F32),
            pltpu.VMEM((nh, 1, 2 * tq), F32),
        ],
        compiler_params=_params("parallel", "parallel", "arbitrary"),
        name="diff_attn",
    )(slopes, qt, k, vt, vec(lq1), vec(lk1), vec(lq2), vec(lk2), head_norm.reshape(hd, 1).astype(F32), *cast)


SGU_ROWS = 256
GELU_C = math.sqrt(2.0 / math.pi)


def _sgu_body(x_ref, g_ref, win_ref, bin_ref, vn_ref, ws_ref, bs_ref, y_ref):
    tm = x_ref.shape[0]
    width = vn_ref.shape[1]
    gd = width // SGU_GROUPS
    c = SGU_CHUNK
    nc = SGU_ROWS // c
    row = lax.broadcasted_iota(jnp.int32, (c, c), 0)
    col = lax.broadcasted_iota(jnp.int32, (c, c), 1)
    ws = [jnp.where(col <= row, ws_ref[gi], 0.0).astype(BF16) for gi in range(SGU_GROUPS)]
    w_in = win_ref[...].astype(BF16)
    uvs = [_dot(_rms(x_ref[r:r + SGU_ROWS], g_ref[...]).astype(BF16), w_in) + bin_ref[...]
           for r in range(0, tm, SGU_ROWS)]
    for i, uv in enumerate(uvs):
        uv = 0.5 * uv * (1.0 + jnp.tanh(uv * (GELU_C + (GELU_C * 0.044715) * (uv * uv))))
        u = uv[:, :width]
        v = _rms(uv[:, width:], vn_ref[...]).astype(BF16)
        cols = []
        for gi in range(SGU_GROUPS):
            vg = jnp.concatenate([v[n * c:(n + 1) * c, gi * gd:(gi + 1) * gd] for n in range(nc)], axis=1)
            sg = _dot(ws[gi], vg)
            bias = bs_ref[gi]
            cols.append(jnp.concatenate([sg[:, n * gd:(n + 1) * gd] + bias for n in range(nc)], axis=0))
        s = jnp.concatenate(cols, axis=1)
        y_ref[i * SGU_ROWS:(i + 1) * SGU_ROWS] = (u * s).astype(y_ref.dtype)


def _sgu(x, gain, w_in, b_in, v_norm, w_s, b_s, tm=1024):
    m, d = x.shape
    width = v_norm.shape[0]
    gd = width // SGU_GROUPS
    bs = jnp.broadcast_to(b_s[:, :, None], (SGU_GROUPS, SGU_CHUNK, gd)).astype(F32)
    fixed2 = lambda i: (0, 0)
    fixed3 = lambda i: (0, 0, 0)
    return pl.pallas_call(
        _sgu_body,
        grid=(m // tm,),
        in_specs=[
            pl.BlockSpec((tm, d), lambda i: (i, 0)),
            pl.BlockSpec((1, d), fixed2),
            pl.BlockSpec(w_in.shape, fixed2),
            pl.BlockSpec((1, 2 * width), fixed2),
            pl.BlockSpec((1, width), fixed2),
            pl.BlockSpec(w_s.shape, fixed3),
            pl.BlockSpec(bs.shape, fixed3),
        ],
        out_specs=pl.BlockSpec((tm, width), lambda i: (i, 0)),
        out_shape=jax.ShapeDtypeStruct((m, width), BF16),
        compiler_params=_params("parallel"),
        name="sgu",
    )(x, gain.reshape(1, d), w_in, b_in.reshape(1, 2 * width), v_norm.reshape(1, width),
      w_s, bs)


def _gla_mixer(x, batch, seq, norm1, w_in, gw1, gw2, gb, head_norm, cast=()):
    q, k, v, og, la, *cast_out = _gla_inproj(x, norm1, w_in, gw1, gw2, gb, cast)
    return _gla_core(q, k, v, og, la, head_norm, batch, seq), cast_out


def _diff_mixer(x, batch, seq, norm1, w_in, lq1, lk1, lq2, lk2, head_norm, layer_idx, cast=()):
    qt, k, vt = _diff_inproj(x, norm1, w_in)
    o, *cast_out = _diff_attn(qt, k, vt, lq1, lk1, lq2, lk2, head_norm, batch, seq, layer_idx, cast)
    return o, cast_out


def kernel(x, l0_norm1, l0_w_in, l0_gate_w1, l0_gate_w2, l0_gate_b, l0_head_norm, l0_w_out, l0_norm2, l0_mlp_w1, l0_mlp_w2, l1_norm1, l1_w_in, l1_lambda_q1, l1_lambda_k1, l1_lambda_q2, l1_lambda_k2, l1_head_norm, l1_w_out, l1_norm2, l1_mlp_w1, l1_mlp_w2, l2_norm1, l2_w_in, l2_b_in, l2_v_norm, l2_w_s, l2_b_s, l2_w_out, l2_norm2, l2_mlp_w1, l2_mlp_w2, l3_norm1, l3_w_in, l3_gate_w1, l3_gate_w2, l3_gate_b, l3_head_norm, l3_w_out, l3_norm2, l3_mlp_w1, l3_mlp_w2, final_norm):
    batch, seq, d = x.shape
    h = x.reshape(batch * seq, d)
    o, (w1, w2) = _gla_mixer(h, batch, seq, l0_norm1, l0_w_in, l0_gate_w1, l0_gate_w2, l0_gate_b, l0_head_norm,
                             cast=(l0_mlp_w1, l0_mlp_w2))
    h = _mlp(h, o, l0_w_out, l0_norm2, w1, w2)
    later = (l1_mlp_w1, l1_mlp_w2, l2_mlp_w1, l2_mlp_w2, l3_mlp_w1, l3_mlp_w2)
    o, later = _diff_mixer(h, batch, seq, l1_norm1, l1_w_in, l1_lambda_q1, l1_lambda_k1, l1_lambda_q2, l1_lambda_k2,
                           l1_head_norm, 1, cast=later)
    h = _mlp(h, o, l1_w_out, l1_norm2, later[0], later[1])
    o = _sgu(h, l2_norm1, l2_w_in, l2_b_in, l2_v_norm, l2_w_s, l2_b_s)
    h = _mlp(h, o, l2_w_out, l2_norm2, later[2], later[3])
    o, _ = _gla_mixer(h, batch, seq, l3_norm1, l3_w_in, l3_gate_w1, l3_gate_w2, l3_gate_b, l3_head_norm)
    h = _mlp(h, o, l3_w_out, l3_norm2, later[4], later[5], final_gain=final_norm)
    return h.reshape(batch, seq, d)
```

```python
import functools
import math

import jax
import jax.numpy as jnp
from jax import lax
from jax.experimental import pallas as pl
from jax.experimental.pallas import tpu as pltpu

F32 = jnp.float32
BF16 = jnp.bfloat16

EPS = 1e-6
NEG_INF = -1e30

GLA_HEADS = 4
GLA_RANK = 16
GLA_GATE_NORM = 16.0
GLA_CHUNK = 64

DIFF_HEAD_DIM = 64
DIFF_HEADS = 8

SGU_CHUNK = 128
SGU_GROUPS = 8

LANES = 128
LOG2E = 1.4426950408889634
VMEM_LIMIT = 52 * 1024 * 1024


def _params(*sem):
    return pltpu.CompilerParams(dimension_semantics=sem, vmem_limit_bytes=VMEM_LIMIT)


def _dot(a, b):
    return jnp.dot(a, b, preferred_element_type=F32)


def _dot_nt(a, b):
    return lax.dot_general(a, b, (((1,), (1,)), ((), ())), preferred_element_type=F32)


def _dot_tn(a, b):
    return lax.dot_general(a, b, (((0,), (0,)), ((), ())), preferred_element_type=F32)


def _rms(x, g):
    return x * lax.rsqrt(jnp.mean(x * x, axis=-1, keepdims=True) + EPS) * g


BF16_ROWS = 16


def _cast_specs(weights, steps, step_of):
    in_specs, out_specs, out_shapes = [], [], []
    for w in weights:
        rows, cols = w.shape
        blk = max(BF16_ROWS, rows // steps)
        nblk = rows // blk
        index = lambda *ids, nblk=nblk: (step_of(*ids) * nblk // steps, 0)
        in_specs.append(pl.BlockSpec((blk, cols), index))
        out_specs.append(pl.BlockSpec((blk, cols), index))
        out_shapes.append(jax.ShapeDtypeStruct(w.shape, BF16))
    return in_specs, out_specs, out_shapes


def _cast_blocks(src_refs, dst_refs):
    for src, dst in zip(src_refs, dst_refs):
        dst[...] = src[...].astype(BF16)


INPROJ_GROUPS = 2


def _gla_inproj_body(n_cast, x_ref, g_ref, w_ref, gw1_ref, gw2_ref, gb_ref, *refs):
    q_ref, k_ref, v_ref, og_ref, la_ref = refs[n_cast:n_cast + 5]
    w_scr = refs[2 * n_cast + 5]
    _cast_blocks(refs[:n_cast], refs[n_cast + 5:2 * n_cast + 5])
    dk = q_ref.shape[1]
    dv = v_ref.shape[1]

    @pl.when(pl.program_id(0) == 0)
    def _():
        for c0 in range(0, w_ref.shape[1], dk):
            w_scr[:, c0:c0 + dk] = w_ref[:, c0:c0 + dk].astype(BF16)

    rows = x_ref.shape[0] // INPROJ_GROUPS
    for r in range(0, x_ref.shape[0], rows):
        sl = slice(r, r + rows)
        h = _rms(x_ref[sl], g_ref[...]).astype(BF16)
        t = _dot(h, gw1_ref[...]).astype(BF16)
        q_ref[sl] = _dot(h, w_scr[:, 0:dk])
        z = _dot(t, gw2_ref[...]) + gb_ref[...]
        k_ref[sl] = _dot(h, w_scr[:, dk:2 * dk])
        la_ref[sl] = (jnp.minimum(z, 0.0) - jnp.log(1.0 + jnp.exp(-jnp.abs(z)))) / GLA_GATE_NORM
        v_ref[sl] = _dot(h, w_scr[:, 2 * dk:2 * dk + dv]).astype(BF16)
        og_ref[sl] = _dot(h, w_scr[:, 2 * dk + dv:2 * dk + 2 * dv])


def _gla_inproj(x, gain, w_in, gw1, gw2, gb, cast=(), tm=512):
    m, d = x.shape
    cast_in, cast_out, cast_shapes = _cast_specs(cast, m // tm, lambda i: i)
    dk = gw2.shape[1]
    dv = (w_in.shape[1] - 2 * dk) // 2
    gw1p = jnp.zeros((d, LANES), BF16).at[:, :GLA_RANK].set(gw1.astype(BF16))
    gw2p = jnp.zeros((LANES, dk), BF16).at[:GLA_RANK, :].set(gw2.astype(BF16))
    row = lambda i: (i, 0)
    fixed = lambda i: (0, 0)
    return pl.pallas_call(
        functools.partial(_gla_inproj_body, len(cast)),
        grid=(m // tm,),
        in_specs=[
            pl.BlockSpec((tm, d), row),
            pl.BlockSpec((1, d), fixed),
            pl.BlockSpec(w_in.shape, fixed, pipeline_mode=pl.Buffered(1)),
            pl.BlockSpec(gw1p.shape, fixed),
            pl.BlockSpec(gw2p.shape, fixed),
            pl.BlockSpec((1, dk), fixed),
        ] + cast_in,
        out_specs=[
            pl.BlockSpec((tm, dk), row),
            pl.BlockSpec((tm, dk), row),
            pl.BlockSpec((tm, dv), row),
            pl.BlockSpec((tm, dv), row),
            pl.BlockSpec((tm, dk), row),
        ] + cast_out,
        out_shape=[
            jax.ShapeDtypeStruct((m, dk), F32),
            jax.ShapeDtypeStruct((m, dk), F32),
            jax.ShapeDtypeStruct((m, dv), BF16),
            jax.ShapeDtypeStruct((m, dv), F32),
            jax.ShapeDtypeStruct((m, dk), F32),
        ] + cast_shapes,
        scratch_shapes=[pltpu.VMEM(w_in.shape, BF16)],
        compiler_params=_params("arbitrary"),
        name="gla_inproj",
    )(x, gain.reshape(1, d), w_in, gw1p, gw2p, gb.reshape(1, dk), *cast)


def _gla_body(tri_ref, q_ref, k_ref, v_ref, og_ref, la_ref, hn_ref, o_ref, st_ref):
    c = GLA_CHUNK
    tc, hk = q_ref.shape
    n_chunks = tc // c

    @pl.when(pl.program_id(2) == 0)
    def _():
        st_ref[...] = jnp.zeros_like(st_ref)

    tri = tri_ref[...]
    tg = tri.shape[0]
    la = la_ref[...]
    la_hi = la.astype(BF16)
    la_lo = (la - la_hi.astype(F32)).astype(BF16)
    b = jnp.concatenate([_dot(tri, la_hi[r:r + tg]) + _dot(tri, la_lo[r:r + tg]) for r in range(0, tc, tg)],
                        axis=0)
    b_lasts = [b[(n + 1) * c - 1:(n + 1) * c, :] for n in range(n_chunks)]
    b_end = jnp.concatenate([jnp.broadcast_to(bl, (c, hk)) for bl in b_lasts], axis=0)

    q = q_ref[...] * (hk ** -0.5)
    k = k_ref[...]
    q_dec = (q * jnp.exp(b)).astype(BF16)
    k_inv = (k * jnp.exp(-b)).astype(BF16)
    k_end = (k * jnp.exp(b_end - b)).astype(BF16)

    row = lax.broadcasted_iota(jnp.int32, (c, c), 0)
    col = lax.broadcasted_iota(jnp.int32, (c, c), 1)
    causal = col <= row
    chunk = lambda a, n: a[n * c:(n + 1) * c]

    atts = [jnp.where(causal, _dot_nt(chunk(q_dec, n), chunk(k_inv, n)), 0.0).astype(BF16)
            for n in range(n_chunks)]
    kvs = [_dot_tn(v_ref[n * c:(n + 1) * c, :], chunk(k_end, n)) for n in range(n_chunks)]
    intra = [_dot(atts[n], v_ref[n * c:(n + 1) * c, :]) for n in range(n_chunks)]

    st = st_ref[...]
    outs = []
    for n in range(n_chunks):
        outs.append(intra[n] + _dot_nt(chunk(q_dec, n), st.astype(BF16)))
        st = jnp.exp(b_lasts[n]) * st + kvs[n]
    st_ref[...] = st

    o = jnp.concatenate(outs, axis=0)
    o = o * lax.rsqrt(jnp.mean(o * o, axis=-1, keepdims=True) + EPS) * hn_ref[...]
    g = og_ref[...]
    o_ref[...] = (o * (g / (1.0 + jnp.exp(-g)))).astype(o_ref.dtype)


GLA_CUMSUM_ROWS = 256


def _gla_core(q, k, v, og, la, head_norm, batch, seq, tc=2048):
    tc = min(tc, seq)
    m, dk = q.shape
    dv = v.shape[1]
    hk = dk // GLA_HEADS
    hv = dv // GLA_HEADS
    nt = seq // tc
    idx = jnp.arange(GLA_CUMSUM_ROWS)
    tri = ((idx[:, None] // GLA_CHUNK == idx[None, :] // GLA_CHUNK)
           & (idx[None, :] <= idx[:, None])).astype(BF16)
    blk = lambda b, h, t: (b * nt + t, h)
    return pl.pallas_call(
        _gla_body,
        grid=(batch, GLA_HEADS, nt),
        in_specs=[
            pl.BlockSpec(tri.shape, lambda b, h, t: (0, 0)),
            pl.BlockSpec((tc, hk), blk),
            pl.BlockSpec((tc, hk), blk),
            pl.BlockSpec((tc, hv), blk),
            pl.BlockSpec((tc, hv), blk),
            pl.BlockSpec((tc, hk), blk),
            pl.BlockSpec((1, hv), lambda b, h, t: (0, 0)),
        ],
        out_specs=pl.BlockSpec((tc, hv), blk),
        out_shape=jax.ShapeDtypeStruct((m, dv), BF16),
        scratch_shapes=[pltpu.VMEM((hv, hk), F32)],
        compiler_params=_params("parallel", "parallel", "arbitrary"),
        name="gla_core",
    )(tri, q, k, v, og, la, head_norm.reshape(1, hv))


MLP_INIT_GROUPS = 4
MLP_FF_CHUNK = 512


def _mlp_body(final, x_ref, o_ref, wo_ref, g_ref, w1_ref, w2_ref, gf_ref, y_ref, h_scr):
    tm = x_ref.shape[0]
    rows = tm // MLP_INIT_GROUPS
    wo = wo_ref[...].astype(BF16)
    for r in range(0, tm, rows):
        x1 = x_ref[r:r + rows] + _dot(o_ref[r:r + rows], wo)
        y_ref[r:r + rows] = x1
        h_scr[r:r + rows] = _rms(x1, g_ref[...]).astype(BF16)
    for c in range(0, w1_ref.shape[1], MLP_FF_CHUNK):
        a = jnp.maximum(_dot(h_scr[...], w1_ref[:, c:c + MLP_FF_CHUNK]), 0.0)
        y_ref[...] += _dot((a * a).astype(BF16), w2_ref[c:c + MLP_FF_CHUNK, :])
    if final:
        y_ref[...] = _rms(y_ref[...], gf_ref[...])


def _mlp(x, o, w_out, gain, w1, w2, final_gain=None, tm=1024):
    m, d = x.shape
    kdim = o.shape[1]
    final = final_gain is not None
    gf = (final_gain if final else gain).reshape(1, d)
    resident = lambda a: pl.BlockSpec(a.shape, lambda i: (0, 0), pipeline_mode=pl.Buffered(1))
    return pl.pallas_call(
        functools.partial(_mlp_body, final),
        grid=(m // tm,),
        in_specs=[
            pl.BlockSpec((tm, d), lambda i: (i, 0)),
            pl.BlockSpec((tm, kdim), lambda i: (i, 0)),
            resident(w_out),
            pl.BlockSpec((1, d), lambda i: (0, 0)),
            resident(w1),
            resident(w2),
            pl.BlockSpec((1, d), lambda i: (0, 0)),
        ],
        out_specs=pl.BlockSpec((tm, d), lambda i: (i, 0)),
        out_shape=jax.ShapeDtypeStruct((m, d), F32),
        scratch_shapes=[pltpu.VMEM((tm, d), BF16)],
        compiler_params=_params("parallel"),
        name="mlp",
    )(x, o, w_out, gain.reshape(1, d), w1, w2, gf)


Q_BLOCK = 256
KEY_BLOCK = 512


def _diff_inproj_body(x_ref, g_ref, w_ref, qt_ref, k_ref, vt_ref, wqt_scr, wk_scr, wvt_scr):
    tq = qt_ref.shape[2]
    tk = vt_ref.shape[2]
    dq = k_ref.shape[1]

    @pl.when(pl.program_id(0) == 0)
    def _():
        wqt_scr[...] = w_ref[:, :dq].T.astype(BF16)
        wk_scr[...] = w_ref[:, dq:2 * dq].astype(BF16)
        wvt_scr[...] = w_ref[:, 2 * dq:].T.astype(BF16)

    for r in range(0, x_ref.shape[0], tq):
        h = _rms(x_ref[r:r + tq], g_ref[...]).astype(BF16)
        k_ref[r:r + tq] = _dot(h, wk_scr[...]).astype(BF16)
        qt = _dot_nt(wqt_scr[...], h) * (LOG2E * DIFF_HEAD_DIM ** -0.5)
        qt_ref[r // tq] = qt.astype(BF16)
        vt_ref[r // tk, :, r % tk:r % tk + tq] = _dot_nt(wvt_scr[...], h).astype(BF16)


def _diff_inproj(x, gain, w_in, tm=1024):
    m, d = x.shape
    dq = 2 * DIFF_HEADS * DIFF_HEAD_DIM
    dv = w_in.shape[1] - 2 * dq
    tq, tk = Q_BLOCK, KEY_BLOCK
    fixed = lambda i: (0, 0)
    return pl.pallas_call(
        _diff_inproj_body,
        grid=(m // tm,),
        in_specs=[
            pl.BlockSpec((tm, d), lambda i: (i, 0)),
            pl.BlockSpec((1, d), fixed),
            pl.BlockSpec(w_in.shape, fixed, pipeline_mode=pl.Buffered(1)),
        ],
        out_specs=[
            pl.BlockSpec((tm // tq, dq, tq), lambda i: (i, 0, 0)),
            pl.BlockSpec((tm, dq), lambda i: (i, 0)),
            pl.BlockSpec((tm // tk, dv, tk), lambda i: (i, 0, 0)),
        ],
        out_shape=[
            jax.ShapeDtypeStruct((m // tq, dq, tq), BF16),
            jax.ShapeDtypeStruct((m, dq), BF16),
            jax.ShapeDtypeStruct((m // tk, dv, tk), BF16),
        ],
        scratch_shapes=[pltpu.VMEM((dq, d), BF16), pltpu.VMEM((d, dq), BF16), pltpu.VMEM((dv, d), BF16)],
        compiler_params=_params("arbitrary"),
        name="diff_inproj",
    )(x, gain.reshape(1, d), w_in)


ATT_HEADS_PER_STEP = 4
ATT_STAGES_PER_TRIP = 4
KEY_SPLIT = 32
ONES_ROWS = 16
FEAT_ROWS = 8


def _bf16_pieces(c):
    c1 = c.astype(BF16).astype(F32)
    c2 = (c - c1).astype(BF16).astype(F32)
    c3 = (c - c1 - c2).astype(BF16).astype(F32)
    return c1, c2, c3


def _diff_attn_body(lam_init, n_cast, slopes_ref, qt_ref, k_ref, vt_ref, lq1_ref, lk1_ref, lq2_ref, lk2_ref,
                    hn_ref, *refs):
    o_ref = refs[n_cast]
    m_scr, acc_scr, s_scr, mx_scr = refs[2 * n_cast + 1:]
    tq, tk = Q_BLOCK, KEY_BLOCK
    hd = 2 * DIFF_HEAD_DIM
    nh = ATT_HEADS_PER_STEP
    hg = pl.program_id(1)
    n_full = pl.program_id(2)

    def tri_mask(t):
        key = lax.broadcasted_iota(jnp.int32, (tq, 2 * tq), 0)
        qry = lax.broadcasted_iota(jnp.int32, (tq, 2 * tq), 1) % tq
        return jnp.where(key <= qry, t, NEG_INF)

    kj = lax.broadcasted_iota(jnp.int32, (tk, LANES), 0)
    kl = lax.broadcasted_iota(jnp.int32, (tk, LANES), 1)
    kfeat = jnp.where(kl < 3, kj // KEY_SPLIT, jnp.where(kl < 6, kj % KEY_SPLIT, 0)).astype(F32).astype(BF16)
    ones = jnp.ones((ONES_ROWS, tk), BF16)
    half = lax.broadcasted_iota(jnp.int32, (hd, tq), 0) < DIFF_HEAD_DIM
    frow = lax.broadcasted_iota(jnp.int32, (FEAT_ROWS, 2 * tq), 0)

    slopes, qfeats = [], []
    for j in range(nh):
        slope = slopes_ref[hg * nh + j] * LOG2E
        c1, c2, c3 = _bf16_pieces(jnp.full(frow.shape, slope, F32))
        s = float(KEY_SPLIT)
        qfeat = jnp.where(frow == 0, s * c1, jnp.where(frow == 1, s * c2, jnp.where(frow == 2, s * c3,
                jnp.where(frow == 3, c1, jnp.where(frow == 4, c2, jnp.where(frow == 5, c3, 0.0))))))
        slopes.append(slope)
        qfeats.append(jnp.concatenate([qfeat, jnp.zeros((hd - FEAT_ROWS, 2 * tq), F32)], axis=0).astype(BF16))

    def make_heads(slot):
        heads = []
        for j in range(nh):
            qt = qt_ref[slot, j * hd:(j + 1) * hd, :]
            zero = jnp.zeros_like(qt)
            q2t = jnp.concatenate([jnp.where(half, qt, zero), jnp.where(half, zero, qt)], axis=1)
            heads.append((slopes[j], jnp.concatenate([q2t, qfeats[j]], axis=0)))
        return heads

    def scores(heads, j, kb, rows=tk):
        k_blk = k_ref[pl.ds(pl.multiple_of(kb * tk, tk), rows), j * hd:(j + 1) * hd]
        return _dot(jnp.concatenate([k_blk, kfeat[:rows]], axis=1), heads[j][1])

    def stage(j, t):
        s_scr[j, :t.shape[0]] = t
        mx_scr[j] = jnp.max(t, axis=0, keepdims=True)

    def consume(heads, slot, j, kb, rows=tk):
        q_tile = 2 * n_full + slot
        shift = heads[j][0] * (kb * tk - q_tile * tq).astype(F32)
        m_old = m_scr[slot, j]
        m_new = jnp.maximum(m_old, mx_scr[j] + shift)
        alpha = jnp.exp2(m_old - m_new)
        p = jnp.exp2(s_scr[j, :rows] - (m_new - shift)).astype(BF16)
        vt_aug = jnp.concatenate([vt_ref[kb, j * hd:(j + 1) * hd, :rows], ones[:, :rows]], axis=0)
        acc_scr[slot, j] = alpha * acc_scr[slot, j] + _dot(vt_aug, p)
        m_scr[slot, j] = m_new

    def pipelined(heads, slot, kb, next_tile):
        for j in range(nh):
            t_next = next_tile(j)
            consume(heads, slot, j, kb)
            stage(j, t_next)

    def plain_stages(heads, slot):
        n_plain = jnp.maximum(n_full - 1, 0)

        def stages(first, count):
            for kb in range(count):
                pipelined(heads, slot, first + kb, lambda j, kb=kb: scores(heads, j, first + kb + 1))

        def trip(i, carry):
            stages(ATT_STAGES_PER_TRIP * i, ATT_STAGES_PER_TRIP)
            return carry

        lax.fori_loop(0, n_plain // ATT_STAGES_PER_TRIP, trip, 0)
        done = n_plain - n_plain % ATT_STAGES_PER_TRIP

        @pl.when(n_plain % ATT_STAGES_PER_TRIP >= 2)
        def _():
            stages(done, 2)

        @pl.when(n_plain % 2 == 1)
        def _():
            stages(n_plain - 1, 1)

    def finish(slot):
        lam = (jnp.exp(jnp.sum(lq1_ref[...] * lk1_ref[...], keepdims=True))
               - jnp.exp(jnp.sum(lq2_ref[...] * lk2_ref[...], keepdims=True)) + lam_init)
        for j in range(nh):
            acc = acc_scr[slot, j]
            o = acc[:hd] / acc[hd:hd + 1]
            ot = o[:, :tq] - lam * o[:, tq:]
            ot = ot * lax.rsqrt(jnp.mean(ot * ot, axis=0, keepdims=True) + EPS)
            ot = ot * hn_ref[...] * (1.0 - lam_init)
            o_ref[slot * tq:(slot + 1) * tq, j * hd:(j + 1) * hd] = ot.T.astype(o_ref.dtype)

    even, odd = make_heads(0), make_heads(1)
    last_even = lambda j: tri_mask(scores(even, j, n_full, rows=tq))

    def last_odd(j):
        t = scores(odd, j, n_full)
        return jnp.concatenate([t[:tq], tri_mask(t[tq:])], axis=0)

    def reset():
        m_scr[...] = jnp.full_like(m_scr, NEG_INF)
        acc_scr[...] = jnp.zeros_like(acc_scr)


    @pl.when(n_full == 0)
    def _():
        reset()
        for j in range(nh):
            stage(j, last_even(j))

    @pl.when(n_full > 0)
    def _():
        reset()
        for j in range(nh):
            stage(j, scores(even, j, 0))

    plain_stages(even, 0)

    @pl.when(n_full > 0)
    def _():
        pipelined(even, 0, n_full - 1, last_even)

    @pl.when(n_full == 0)
    def _():
        for j in range(nh):
            t_next = last_odd(j)
            consume(even, 0, j, n_full, rows=tq)
            stage(j, t_next)

    @pl.when(n_full > 0)
    def _():
        for j in range(nh):
            t_next = scores(odd, j, 0)
            consume(even, 0, j, n_full, rows=tq)
            stage(j, t_next)

    plain_stages(odd, 1)

    @pl.when(n_full > 0)
    def _():
        pipelined(odd, 1, n_full - 1, last_odd)

    _cast_blocks(refs[:n_cast], refs[n_cast + 1:2 * n_cast + 1])
    finish(0)
    for j in range(nh):
        consume(odd, 1, j, n_full)
    finish(1)


def _diff_attn(qt, k, vt, lq1, lk1, lq2, lk2, head_norm, batch, seq, layer_idx, cast=()):
    tq, tk = Q_BLOCK, KEY_BLOCK
    nh = ATT_HEADS_PER_STEP
    m, dq = k.shape
    hd = 2 * DIFF_HEAD_DIM
    assert tk == 2 * tq
    nq = seq // tk
    ng = DIFF_HEADS // nh
    cast_in, cast_out, cast_shapes = _cast_specs(cast, batch * ng * nq, lambda b, h, i: (b * ng + h) * nq + i)
    lam_init = 0.8 - 0.6 * math.exp(-0.3 * layer_idx)
    slopes = 2.0 ** (-8.0 * jnp.arange(1, DIFF_HEADS + 1, dtype=F32) / DIFF_HEADS)
    vec = lambda a: a.reshape(1, DIFF_HEAD_DIM).astype(F32)
    lam_spec = pl.BlockSpec((1, DIFF_HEAD_DIM), lambda b, h, i: (0, 0))
    return pl.pallas_call(
        functools.partial(_diff_attn_body, lam_init, len(cast)),
        grid=(batch, ng, nq),
        in_specs=[
            pl.BlockSpec(memory_space=pltpu.SMEM),
            pl.BlockSpec((2, nh * hd, tq), lambda b, h, i: (b * nq + i, h, 0)),
            pl.BlockSpec((seq, nh * hd), lambda b, h, i: (b, h)),
            pl.BlockSpec((seq // tk, nh * hd, tk), lambda b, h, i: (b, h, 0)),
            lam_spec, lam_spec, lam_spec, lam_spec,
            pl.BlockSpec((hd, 1), lambda b, h, i: (0, 0)),
        ] + cast_in,
        out_specs=[pl.BlockSpec((tk, nh * hd), lambda b, h, i: (b * nq + i, h))] + cast_out,
        out_shape=[jax.ShapeDtypeStruct((m, dq), BF16)] + cast_shapes,
        scratch_shapes=[
            pltpu.VMEM((2, nh, 1, 2 * tq), F32),
            pltpu.VMEM((2, nh, hd + ONES_ROWS, 2 * tq), F32),
            pltpu.VMEM((nh, tk, 2 * tq), F32),
            pltpu.VMEM((nh, 1, 2 * tq), F32),
        ],
        compiler_params=_params("parallel", "parallel", "arbitrary"),
        name="diff_attn",
    )(slopes, qt, k, vt, vec(lq1), vec(lk1), vec(lq2), vec(lk2), head_norm.reshape(hd, 1).astype(F32), *cast)


SGU_ROWS = 256
GELU_C = math.sqrt(2.0 / math.pi)


def _sgu_body(x_ref, g_ref, win_ref, bin_ref, vn_ref, ws_ref, bs_ref, y_ref):
    tm = x_ref.shape[0]
    width = vn_ref.shape[1]
    gd = width // SGU_GROUPS
    c = SGU_CHUNK
    nc = SGU_ROWS // c
    row = lax.broadcasted_iota(jnp.int32, (c, c), 0)
    col = lax.broadcasted_iota(jnp.int32, (c, c), 1)
    ws = [jnp.where(col <= row, ws_ref[gi], 0.0).astype(BF16) for gi in range(SGU_GROUPS)]
    w_in = win_ref[...].astype(BF16)
    uvs = [_dot(_rms(x_ref[r:r + SGU_ROWS], g_ref[...]).astype(BF16), w_in) + bin_ref[...]
           for r in range(0, tm, SGU_ROWS)]
    for i, uv in enumerate(uvs):
        uv = 0.5 * uv * (1.0 + jnp.tanh(uv * (GELU_C + (GELU_C * 0.044715) * (uv * uv))))
        u = uv[:, :width]
        v = _rms(uv[:, width:], vn_ref[...]).astype(BF16)
        cols = []
        for gi in range(SGU_GROUPS):
            vg = jnp.concatenate([v[n * c:(n + 1) * c, gi * gd:(gi + 1) * gd] for n in range(nc)], axis=1)
            sg = _dot(ws[gi], vg)
            bias = bs_ref[gi]
            cols.append(jnp.concatenate([sg[:, n * gd:(n + 1) * gd] + bias for n in range(nc)], axis=0))
        s = jnp.concatenate(cols, axis=1)
        y_ref[i * SGU_ROWS:(i + 1) * SGU_ROWS] = (u * s).astype(y_ref.dtype)


def _sgu(x, gain, w_in, b_in, v_norm, w_s, b_s, tm=1024):
    m, d = x.shape
    width = v_norm.shape[0]
    gd = width // SGU_GROUPS
    bs = jnp.broadcast_to(b_s[:, :, None], (SGU_GROUPS, SGU_CHUNK, gd)).astype(F32)
    fixed2 = lambda i: (0, 0)
    fixed3 = lambda i: (0, 0, 0)
    return pl.pallas_call(
        _sgu_body,
        grid=(m // tm,),
        in_specs=[
            pl.BlockSpec((tm, d), lambda i: (i, 0)),
            pl.BlockSpec((1, d), fixed2),
            pl.BlockSpec(w_in.shape, fixed2),
            pl.BlockSpec((1, 2 * width), fixed2),
            pl.BlockSpec((1, width), fixed2),
            pl.BlockSpec(w_s.shape, fixed3),
            pl.BlockSpec(bs.shape, fixed3),
        ],
        out_specs=pl.BlockSpec((tm, width), lambda i: (i, 0)),
        out_shape=jax.ShapeDtypeStruct((m, width), BF16),
        compiler_params=_params("parallel"),
        name="sgu",
    )(x, gain.reshape(1, d), w_in, b_in.reshape(1, 2 * width), v_norm.reshape(1, width),
      w_s, bs)


def _gla_mixer(x, batch, seq, norm1, w_in, gw1, gw2, gb, head_norm, cast=()):
    q, k, v, og, la, *cast_out = _gla_inproj(x, norm1, w_in, gw1, gw2, gb, cast)
    return _gla_core(q, k, v, og, la, head_norm, batch, seq), cast_out


def _diff_mixer(x, batch, seq, norm1, w_in, lq1, lk1, lq2, lk2, head_norm, layer_idx, cast=()):
    qt, k, vt = _diff_inproj(x, norm1, w_in)
    o, *cast_out = _diff_attn(qt, k, vt, lq1, lk1, lq2, lk2, head_norm, batch, seq, layer_idx, cast)
    return o, cast_out


def kernel(x, l0_norm1, l0_w_in, l0_gate_w1, l0_gate_w2, l0_gate_b, l0_head_norm, l0_w_out, l0_norm2, l0_mlp_w1, l0_mlp_w2, l1_norm1, l1_w_in, l1_lambda_q1, l1_lambda_k1, l1_lambda_q2, l1_lambda_k2, l1_head_norm, l1_w_out, l1_norm2, l1_mlp_w1, l1_mlp_w2, l2_norm1, l2_w_in, l2_b_in, l2_v_norm, l2_w_s, l2_b_s, l2_w_out, l2_norm2, l2_mlp_w1, l2_mlp_w2, l3_norm1, l3_w_in, l3_gate_w1, l3_gate_w2, l3_gate_b, l3_head_norm, l3_w_out, l3_norm2, l3_mlp_w1, l3_mlp_w2, final_norm):
    batch, seq, d = x.shape
    h = x.reshape(batch * seq, d)
    o, (w1, w2) = _gla_mixer(h, batch, seq, l0_norm1, l0_w_in, l0_gate_w1, l0_gate_w2, l0_gate_b, l0_head_norm,
                             cast=(l0_mlp_w1, l0_mlp_w2))
    h = _mlp(h, o, l0_w_out, l0_norm2, w1, w2)
    later = (l1_mlp_w1, l1_mlp_w2, l2_mlp_w1, l2_mlp_w2, l3_mlp_w1, l3_mlp_w2)
    o, later = _diff_mixer(h, batch, seq, l1_norm1, l1_w_in, l1_lambda_q1, l1_lambda_k1, l1_lambda_q2, l1_lambda_k2,
                           l1_head_norm, 1, cast=later)
    h = _mlp(h, o, l1_w_out, l1_norm2, later[0], later[1])
    o = _sgu(h, l2_norm1, l2_w_in, l2_b_in, l2_v_norm, l2_w_s, l2_b_s)
    h = _mlp(h, o, l2_w_out, l2_norm2, later[2], later[3])
    o, _ = _gla_mixer(h, batch, seq, l3_norm1, l3_w_in, l3_gate_w1, l3_gate_w2, l3_gate_b, l3_head_norm)
    h = _mlp(h, o, l3_w_out, l3_norm2, later[4], later[5], final_gain=final_norm)
    return h.reshape(batch, seq, d)
```
